```python
import math
import jax, jax.numpy as jnp
from jax import lax
import numpy as np

D_MODEL = 1024
BATCH = 2
SEQ = 8192
DEPTH = 4
DEC_BATCH = 128
DEC_SEQ = 1
PAST_LEN = 8192
PAGE_SIZE = 128

N_META = 16
HEAD_DIM = 64
ATT_WIDTH = D_MODEL // 2
ATT_HEADS = ATT_WIDTH // HEAD_DIM
ATT_KV_HEADS = ATT_HEADS // 4
KV_WIDTH = ATT_KV_HEADS * HEAD_DIM
WINDOW = 128
ATT_BLOCK = 128
ROPE_THETA = 10000.0
S5_WIDTH = D_MODEL // 4
S5_GROUP_CH = 16
S5_GROUPS = S5_WIDTH // S5_GROUP_CH
S5_STATE = 64
SSD_WIDTH = D_MODEL // 4
SSD_HEAD_DIM = 64
SSD_HEADS = SSD_WIDTH // SSD_HEAD_DIM
SSD_GROUPS = 2
SSD_STATE = 64
SSD_CONV = 4
SSD_CHUNK = 128
SSD_CONV_DIM = SSD_WIDTH + 2 * SSD_GROUPS * SSD_STATE
MIX_WIDTH = ATT_WIDTH + S5_WIDTH + SSD_WIDTH
N_IN = ATT_WIDTH + 2 * KV_WIDTH + S5_WIDTH + SSD_WIDTH + SSD_CONV_DIM + SSD_HEADS
FFN_HIDDEN = ((8 * D_MODEL + 3 * 256 - 1) // (3 * 256)) * 256
NORM_EPS = 1e-6

kernel_name = "hymba_swa_s5_ssd_decoder_step"

F32 = jnp.float32


def rmsnorm(x, g):
    xf = x.astype(F32)
    y = xf * lax.rsqrt(jnp.mean(xf * xf, axis=-1, keepdims=True) + NORM_EPS)
    return (y * g.astype(F32)).astype(x.dtype)


def rope(x, pos):
    half = HEAD_DIM // 2
    inv = ROPE_THETA ** (-jnp.arange(half, dtype=F32) / half)
    ang = pos.astype(F32)[:, None] * inv[None, :]
    cos = jnp.cos(ang)[:, None, :]
    sin = jnp.sin(ang)[:, None, :]
    xf = x.astype(F32)
    x1, x2 = xf[..., :half], xf[..., half:]
    return jnp.concatenate([x1 * cos - x2 * sin, x2 * cos + x1 * sin], axis=-1).astype(x.dtype)


def sink_attention(q, k, v, qpos, kpos, sinks):
    *lead, tq, nh, hd = q.shape
    kvh = k.shape[-2]
    rep = nh // kvh
    qg = q.reshape(*lead, tq, kvh, rep, hd)
    s = jnp.einsum('...qgrd,...kgd->...grqk', qg, k, preferred_element_type=F32) * (hd ** -0.5)
    delta = qpos[..., :, None] - kpos[..., None, :]
    ok = (delta >= 0) & (delta <= WINDOW) & (kpos[..., None, :] >= 0)
    s = jnp.where(ok[..., None, None, :, :], s, -jnp.inf)
    sink = sinks.astype(F32).reshape(kvh, rep)[:, :, None, None]
    m = jnp.maximum(jnp.max(s, axis=-1, keepdims=True), sink)
    p = jnp.exp(s - m)
    den = jnp.sum(p, axis=-1, keepdims=True) + jnp.exp(sink - m)
    w = (p / den).astype(v.dtype)
    o = jnp.einsum('...grqk,...kgd->...qgrd', w, v)
    return o.reshape(*lead, tq, nh * hd)


def attn_prompt(q, k, v, sinks):
    b, L = q.shape[:2]
    front = ATT_BLOCK - N_META
    pad = lambda t: jnp.pad(t, ((0, 0), (front, 0), (0, 0), (0, 0)))
    qp, kp, vp = pad(q), pad(k), pad(v)
    Lp = L + front
    nb = Lp // ATT_BLOCK
    pos = jnp.arange(Lp, dtype=jnp.int32) - front
    qb = qp.reshape(b, nb, ATT_BLOCK, ATT_HEADS, HEAD_DIM)
    kb = kp.reshape(b, nb, ATT_BLOCK, ATT_KV_HEADS, HEAD_DIM)
    vb = vp.reshape(b, nb, ATT_BLOCK, ATT_KV_HEADS, HEAD_DIM)
    two = lambda t: jnp.concatenate(
        [jnp.pad(t[:, :-1], ((0, 0), (1, 0), (0, 0), (0, 0), (0, 0))), t], axis=2)
    kk, vv = two(kb), two(vb)
    pb = pos.reshape(nb, ATT_BLOCK)
    kpos = jnp.concatenate([jnp.pad(pb[:-1], ((1, 0), (0, 0)), constant_values=-1), pb], axis=1)
    o = sink_attention(qb, kk, vv, pb, kpos, sinks)
    return o.reshape(b, Lp, ATT_WIDTH)[:, front:]


def attn_sample(q, k, v, ck, cv, sinks):
    T = q.shape[1]
    W = ck.shape[1]
    qpos = PAST_LEN + jnp.arange(T, dtype=jnp.int32)
    kpos = jnp.concatenate([PAST_LEN - W + jnp.arange(W, dtype=jnp.int32), qpos])
    kk = jnp.concatenate([ck.astype(k.dtype), k], axis=1)
    vv = jnp.concatenate([cv.astype(v.dtype), v], axis=1)
    o = sink_attention(q, kk, vv, qpos, kpos, sinks)
    return o, kk[:, -W:], vv[:, -W:]


def s5_scan(u, h0_re, h0_im, a_re, a_im, log_dt, b_re, b_im, c_re, c_im, d):
    dt = jnp.exp(log_dt.astype(F32))[:, None]
    ar, ai = a_re.astype(F32), a_im.astype(F32)
    mag = jnp.exp(dt * ar)
    abar_re, abar_im = mag * jnp.cos(dt * ai), mag * jnp.sin(dt * ai)
    den = ar * ar + ai * ai
    xr = abar_re - 1.0
    f_re = (xr * ar + abar_im * ai) / den
    f_im = (abar_im * ar - xr * ai) / den
    br, bi = b_re.astype(F32), b_im.astype(F32)
    bb_re = f_re[..., None] * br - f_im[..., None] * bi
    bb_im = f_re[..., None] * bi + f_im[..., None] * br
    bu_re = jnp.einsum('gnc,btgc->btgn', bb_re, u)
    bu_im = jnp.einsum('gnc,btgc->btgn', bb_im, u)
    a_re_t = jnp.broadcast_to(abar_re, bu_re.shape)
    a_im_t = jnp.broadcast_to(abar_im, bu_re.shape)

    def combine(e1, e2):
        a1r, a1i, b1r, b1i = e1
        a2r, a2i, b2r, b2i = e2
        return (a2r * a1r - a2i * a1i, a2r * a1i + a2i * a1r,
                a2r * b1r - a2i * b1i + b2r, a2r * b1i + a2i * b1r + b2i)

    cr, ci, hr, hi = lax.associative_scan(combine, (a_re_t, a_im_t, bu_re, bu_im), axis=1)
    h0r = h0_re.astype(F32)[:, None]
    h0i = h0_im.astype(F32)[:, None]
    hr = hr + cr * h0r - ci * h0i
    hi = hi + cr * h0i + ci * h0r
    y = (jnp.einsum('gcn,btgn->btgc', c_re.astype(F32), hr)
         - jnp.einsum('gcn,btgn->btgc', c_im.astype(F32), hi)
         + d.astype(F32).reshape(S5_GROUPS, S5_GROUP_CH) * u)
    return y, hr[:, -1], hi[:, -1]


def causal_conv(xbc, conv_state, w, bias):
    T = xbc.shape[1]
    xp = jnp.concatenate([conv_state.astype(xbc.dtype), xbc], axis=1)
    out = bias
    for j in range(SSD_CONV):
        out = out + xp[:, j:j + T] * w[j]
    return jax.nn.silu(out), xp[:, -(SSD_CONV - 1):]


def ssd_scan(x, dt, A, Bm, Cm, Dh, h0, front):
    b, T = x.shape[:2]
    total = front + T
    q = min(SSD_CHUNK, total)
    back = (-total) % q
    padt = lambda t: jnp.pad(t, [(0, 0), (front, back)] + [(0, 0)] * (t.ndim - 2))
    x, dt, Bm, Cm = padt(x), padt(dt), padt(Bm), padt(Cm)
    nc = (total + back) // q
    rep = SSD_HEADS // SSD_GROUPS
    ch = lambda t: t.reshape(b, nc, q, *t.shape[2:])
    Bc = ch(jnp.repeat(Bm, rep, axis=2))
    Cc = ch(jnp.repeat(Cm, rep, axis=2))
    xd = ch(x * dt[..., None])
    cs = jnp.cumsum(ch(dt * A), axis=2)
    seg = cs[:, :, :, None, :] - cs[:, :, None, :, :]
    causal = jnp.tril(jnp.ones((q, q), dtype=bool))[None, None, :, :, None]
    Lmat = jnp.exp(jnp.where(causal, seg, -jnp.inf))
    scores = jnp.einsum('bcihn,bcjhn->bcijh', Cc, Bc) * Lmat
    y_diag = jnp.einsum('bcijh,bcjhp->bcihp', scores, xd)
    decay = jnp.exp(cs[:, :, -1:, :] - cs)
    states = jnp.einsum('bcjhn,bcjh,bcjhp->bchpn', Bc, decay, xd)
    chunk_decay = jnp.exp(cs[:, :, -1, :])

    def step(h, inp):
        dcy, st = inp
        return dcy[:, :, None, None] * h + st, h

    h_last, h_prev = lax.scan(step, h0.astype(F32),
                              (jnp.moveaxis(chunk_decay, 1, 0), jnp.moveaxis(states, 1, 0)))
    h_prev = jnp.moveaxis(h_prev, 0, 1)
    y_off = jnp.einsum('bcihn,bchpn,bcih->bcihp', Cc, h_prev, jnp.exp(cs))
    y = (y_diag + y_off).reshape(b, nc * q, SSD_HEADS, SSD_HEAD_DIM) + Dh.astype(F32)[:, None] * x
    return y[:, front:front + T], h_last


def token_mixers(hn, lp, st, prompt):
    ck, cv, s5r0, s5i0, conv0, ssd0 = st
    b, T, _ = hn.shape
    sizes = (ATT_WIDTH, KV_WIDTH, KV_WIDTH, S5_WIDTH, SSD_WIDTH, SSD_CONV_DIM, SSD_HEADS)
    offs = [int(o) for o in np.cumsum(sizes)[:-1]]
    proj = hn @ lp['w_in']
    q, k, v, u, z, xbc, dtr = jnp.split(proj, offs, axis=-1)

    pos = jnp.arange(T, dtype=jnp.int32) + (0 if prompt else PAST_LEN)
    q = rope(q.reshape(b, T, ATT_HEADS, HEAD_DIM), pos)
    k = rope(k.reshape(b, T, ATT_KV_HEADS, HEAD_DIM), pos)
    v = v.reshape(b, T, ATT_KV_HEADS, HEAD_DIM)
    if prompt:
        o_att = attn_prompt(q, k, v, lp['attn_sinks'])
        nk, nv = k[:, -WINDOW:], v[:, -WINDOW:]
    else:
        o_att, nk, nv = attn_sample(q, k, v, ck, cv, lp['attn_sinks'])
    o_att = rmsnorm(o_att, lp['attn_out_g'])

    y5, s5r, s5i = s5_scan(u.astype(F32).reshape(b, T, S5_GROUPS, S5_GROUP_CH), s5r0, s5i0,
                           lp['s5_a_re'], lp['s5_a_im'], lp['s5_log_dt'], lp['s5_b_re'], lp['s5_b_im'],
                           lp['s5_c_re'], lp['s5_c_im'], lp['s5_d'])
    y5 = jax.nn.gelu(y5.reshape(b, T, S5_WIDTH))
    y5 = y5 * jax.nn.sigmoid(y5 @ lp['s5_glu_w'].astype(F32) + lp['s5_glu_b'].astype(F32))
    o_s5 = rmsnorm(y5, lp['s5_out_g']).astype(hn.dtype)

    xbc, nconv = causal_conv(xbc, conv0, lp['ssd_conv_w'], lp['ssd_conv_b'])
    xs, bm, cm = jnp.split(xbc, [SSD_WIDTH, SSD_WIDTH + SSD_GROUPS * SSD_STATE], axis=-1)
    dt = jax.nn.softplus(dtr.astype(F32) + lp['ssd_dt_bias'].astype(F32))
    A = -jnp.exp(lp['ssd_a_log'].astype(F32))
    front = SSD_CHUNK - N_META if prompt else 0
    yc, hssd = ssd_scan(xs.astype(F32).reshape(b, T, SSD_HEADS, SSD_HEAD_DIM), dt, A,
                        bm.astype(F32).reshape(b, T, SSD_GROUPS, SSD_STATE),
                        cm.astype(F32).reshape(b, T, SSD_GROUPS, SSD_STATE),
                        lp['ssd_d'], ssd0, front)
    yc = yc.reshape(b, T, SSD_WIDTH) * jax.nn.silu(z.astype(F32))
    o_ssd = rmsnorm(yc, lp['ssd_norm_g']).astype(hn.dtype)

    mixed = jnp.concatenate([o_att, o_s5, o_ssd], axis=-1) @ lp['w_out']
    return mixed, (nk, nv, s5r, s5i, nconv, hssd)


def swiglu(h, wg, wu, wd):
    return (jax.nn.silu(h @ wg) * (h @ wu)) @ wd


def decoder_layer(x, lp, st, prompt):
    mixed, new_st = token_mixers(rmsnorm(x, lp['ln1_g']), lp, st, prompt)
    x = x + mixed
    x = x + swiglu(rmsnorm(x, lp['ln2_g']), lp['w_gate'], lp['w_up'], lp['w_down'])
    return x, new_st


def setup_inputs(seed: int = 0) -> dict:
    key = jax.random.key(seed)
    ks = iter(jax.random.split(key, 48))
    nrm = lambda shape, scale: jax.random.normal(next(ks), shape, F32) * scale
    unif = lambda shape, lo, hi: jax.random.uniform(next(ks), shape, F32, lo, hi)
    w_c = min(WINDOW, PAST_LEN)
    n_idx = jnp.arange(S5_STATE, dtype=F32)
    dt0 = jnp.exp(unif((DEPTH, SSD_HEADS), math.log(1e-3), math.log(1e-1)))
    return {
        'x_prompt': nrm((BATCH, SEQ, D_MODEL), 1.0),
        'x_sample': nrm((DEC_BATCH, DEC_SEQ, D_MODEL), 1.0),
        'cache_k': nrm((DEPTH, DEC_BATCH, w_c, ATT_KV_HEADS, HEAD_DIM), 1.0),
        'cache_v': nrm((DEPTH, DEC_BATCH, w_c, ATT_KV_HEADS, HEAD_DIM), 1.0),
        'state_s5_re': nrm((DEPTH, DEC_BATCH, S5_GROUPS, S5_STATE), 0.1),
        'state_s5_im': nrm((DEPTH, DEC_BATCH, S5_GROUPS, S5_STATE), 0.1),
        'state_ssd_conv': nrm((DEPTH, DEC_BATCH, SSD_CONV - 1, SSD_CONV_DIM), 1.0),
        'state_ssd': nrm((DEPTH, DEC_BATCH, SSD_HEADS, SSD_HEAD_DIM, SSD_STATE), 0.1),
        'meta_tokens': nrm((N_META, D_MODEL), 1.0),
        'ln1_g': 1.0 + nrm((DEPTH, D_MODEL), 0.02),
        'w_in': nrm((DEPTH, D_MODEL, N_IN), D_MODEL ** -0.5),
        'attn_sinks': nrm((DEPTH, ATT_HEADS), 0.5),
        'attn_out_g': 1.0 + nrm((DEPTH, ATT_WIDTH), 0.02),
        's5_a_re': -0.5 + nrm((DEPTH, S5_GROUPS, S5_STATE), 0.01),
        's5_a_im': math.pi * n_idx + nrm((DEPTH, S5_GROUPS, S5_STATE), 0.01),
        's5_log_dt': unif((DEPTH, S5_GROUPS), math.log(1e-3), math.log(1e-1)),
        's5_b_re': nrm((DEPTH, S5_GROUPS, S5_STATE, S5_GROUP_CH), (2 * S5_GROUP_CH) ** -0.5),
        's5_b_im': nrm((DEPTH, S5_GROUPS, S5_STATE, S5_GROUP_CH), (2 * S5_GROUP_CH) ** -0.5),
        's5_c_re': nrm((DEPTH, S5_GROUPS, S5_GROUP_CH, S5_STATE), (2 * S5_STATE) ** -0.5),
        's5_c_im': nrm((DEPTH, S5_GROUPS, S5_GROUP_CH, S5_STATE), (2 * S5_STATE) ** -0.5),
        's5_d': nrm((DEPTH, S5_WIDTH), 1.0),
        's5_glu_w': nrm((DEPTH, S5_WIDTH, S5_WIDTH), S5_WIDTH ** -0.5),
        's5_glu_b': nrm((DEPTH, S5_WIDTH), 0.01),
        's5_out_g': 1.0 + nrm((DEPTH, S5_WIDTH), 0.02),
        'ssd_conv_w': nrm((DEPTH, SSD_CONV, SSD_CONV_DIM), SSD_CONV ** -0.5),
        'ssd_conv_b': nrm((DEPTH, SSD_CONV_DIM), 0.01),
        'ssd_dt_bias': dt0 + jnp.log(-jnp.expm1(-dt0)),
        'ssd_a_log': jnp.log(unif((DEPTH, SSD_HEADS), 1.0, 16.0)),
        'ssd_d': 1.0 + nrm((DEPTH, SSD_HEADS), 0.02),
        'ssd_norm_g': 1.0 + nrm((DEPTH, SSD_WIDTH), 0.02),
        'w_out': nrm((DEPTH, MIX_WIDTH, D_MODEL), MIX_WIDTH ** -0.5),
        'ln2_g': 1.0 + nrm((DEPTH, D_MODEL), 0.02),
        'w_gate': nrm((DEPTH, D_MODEL, FFN_HIDDEN), D_MODEL ** -0.5),
        'w_up': nrm((DEPTH, D_MODEL, FFN_HIDDEN), D_MODEL ** -0.5),
        'w_down': nrm((DEPTH, FFN_HIDDEN, D_MODEL), FFN_HIDDEN ** -0.5),
        'lnf_g': 1.0 + nrm((D_MODEL,), 0.02),
    }


def reference(x_prompt, x_sample, cache_k, cache_v, state_s5_re, state_s5_im, state_ssd_conv, state_ssd,
              meta_tokens, ln1_g, w_in, attn_sinks, attn_out_g, s5_a_re, s5_a_im, s5_log_dt,
              s5_b_re, s5_b_im, s5_c_re, s5_c_im, s5_d, s5_glu_w, s5_glu_b, s5_out_g,
              ssd_conv_w, ssd_conv_b, ssd_dt_bias, ssd_a_log, ssd_d, ssd_norm_g, w_out,
              ln2_g, w_gate, w_up, w_down, lnf_g):
    b = x_prompt.shape[0]
    meta = jnp.broadcast_to(meta_tokens[None].astype(x_prompt.dtype), (b, N_META, D_MODEL))
    xp = jnp.concatenate([meta, x_prompt], axis=1)
    xs = x_sample
    named = (('ln1_g', ln1_g), ('w_in', w_in), ('attn_sinks', attn_sinks), ('attn_out_g', attn_out_g),
             ('s5_a_re', s5_a_re), ('s5_a_im', s5_a_im), ('s5_log_dt', s5_log_dt),
             ('s5_b_re', s5_b_re), ('s5_b_im', s5_b_im), ('s5_c_re', s5_c_re), ('s5_c_im', s5_c_im),
             ('s5_d', s5_d), ('s5_glu_w', s5_glu_w), ('s5_glu_b', s5_glu_b), ('s5_out_g', s5_out_g),
             ('ssd_conv_w', ssd_conv_w), ('ssd_conv_b', ssd_conv_b), ('ssd_dt_bias', ssd_dt_bias),
             ('ssd_a_log', ssd_a_log), ('ssd_d', ssd_d), ('ssd_norm_g', ssd_norm_g), ('w_out', w_out),
             ('ln2_g', ln2_g), ('w_gate', w_gate), ('w_up', w_up), ('w_down', w_down))
    new_p = [[] for _ in range(6)]
    new_s = [[] for _ in range(6)]
    for l in range(DEPTH):
        lp = {name: arr[l] for name, arr in named}
        st_p = (None, None,
                jnp.zeros((b, S5_GROUPS, S5_STATE), F32), jnp.zeros((b, S5_GROUPS, S5_STATE), F32),
                jnp.zeros((b, SSD_CONV - 1, SSD_CONV_DIM), xp.dtype),
                jnp.zeros((b, SSD_HEADS, SSD_HEAD_DIM, SSD_STATE), F32))
        xp, sp = decoder_layer(xp, lp, st_p, True)
        st_s = (cache_k[l], cache_v[l], state_s5_re[l], state_s5_im[l], state_ssd_conv[l], state_ssd[l])
        xs, ss = decoder_layer(xs, lp, st_s, False)
        for i in range(6):
            new_p[i].append(sp[i])
            new_s[i].append(ss[i])
    y_prompt = rmsnorm(xp, lnf_g)[:, N_META:]
    y_sample = rmsnorm(xs, lnf_g)
    k_p, v_p, s5re_p, s5im_p, conv_p, ssd_p = [jnp.stack(t) for t in new_p]
    k_s, v_s, s5re_s, s5im_s, conv_s, ssd_s = [jnp.stack(t) for t in new_s]
    return (y_prompt, y_sample, k_p, v_p, s5re_p, s5im_p, conv_p, ssd_p,
            k_s, v_s, s5re_s, s5im_s, conv_s, ssd_s)
```

```python
import jax
import jax.numpy as jnp
from jax import lax
from jax.experimental import pallas as pl
from jax.experimental.pallas import tpu as pltpu

F32 = jnp.float32
BF16 = jnp.bfloat16

D_MODEL = 1024
N_META = 16
HEAD_DIM = 64
ATT_WIDTH = 512
ATT_HEADS = 8
KV_WIDTH = 128
S5_WIDTH = 256
S5_GROUPS = 16
S5_GROUP_CH = 16
S5_STATE = 64
S5_LANES = S5_GROUPS * S5_STATE
SSD_WIDTH = 256
SSD_HEADS = 4
SSD_HEAD_DIM = 64
SSD_STATE = 64
SSD_CONV = 4
SSD_CONV_DIM = 512
FFN_HIDDEN = 2816
NORM_EPS = 1e-6
ROPE_THETA = 10000.0
PAST_LEN = 8192
N_IN = 1796

BLK = 128
FRONT = BLK - N_META
TM = 512
SPECIAL = 4 * BLK
DEC = 128
DEC_ROW0 = 2 * BLK
DCH = 16
PROJ_W = 1920
Q0, K0, V0, U0, Z0, XBC0, DT0 = 0, 512, 640, 768, 1024, 1280, 1792
FFN_CHUNK = 256
NEG_INF = float("-inf")
MIB = 1024 * 1024


def _dot(a, b):
    return jnp.dot(a, b, preferred_element_type=F32)


def _dot_nt(a, b):
    return lax.dot_general(a, b, (((1,), (1,)), ((), ())), preferred_element_type=F32)


def _sigmoid(x):
    return 1.0 / (1.0 + jnp.exp(-x))


def _silu(x):
    return x * _sigmoid(x)


def _softplus(x):
    return jnp.maximum(x, 0.0) + jnp.log1p(jnp.exp(-jnp.abs(x)))


def _gelu_tanh(x):
    cdf = 0.5 * (1.0 + jnp.tanh(0.7978845608028654 * (x + 0.044715 * (x * x * x))))
    return x * cdf


def _rmsnorm(x, g):
    return x * lax.rsqrt(jnp.mean(x * x, axis=-1, keepdims=True) + NORM_EPS) * g


def _iota(shape, dim):
    return lax.broadcasted_iota(jnp.int32, shape, dim)


def _rope(x, cosv, sinv):
    first = (_iota(x.shape, 1) & 63) < 32
    partner = jnp.where(first, pltpu.roll(x, 96, 1), pltpu.roll(x, 32, 1))
    return x * cosv + partner * sinv


def _lane_bcast(x, widths):
    rows = x.shape[0]
    return jnp.concatenate(
        [jnp.broadcast_to(x[:, h:h + 1], (rows, w)) for h, w in enumerate(widths)], axis=1)


def _split3(x):
    hi = x.astype(BF16)
    r1 = x - hi.astype(F32)
    mid = r1.astype(BF16)
    lo = (r1 - mid.astype(F32)).astype(BF16)
    return hi, mid, lo


def _s5_prep_kernel(are_ref, aim_ref, ldt_ref, btre_ref, btim_ref, sc_ref, bbre_ref, bbim_ref):
    ar = are_ref[0]
    ai = aim_ref[0]
    dt = jnp.exp(ldt_ref[0])
    row = _iota((8, S5_LANES), 0)
    k = (row + 1).astype(F32)
    mag = jnp.exp((k * dt) * ar)
    ang = (k * dt) * ai
    pre = mag * jnp.cos(ang)
    pim = mag * jnp.sin(ang)
    sc_ref[0, 0:8, :] = pre
    sc_ref[0, 8:16, :] = pim
    for d, base in ((1, 16), (2, 32), (4, 48)):
        keep = row >= d
        sc_ref[0, base:base + 8, :] = jnp.where(keep, jnp.broadcast_to(pre[d - 1:d], (8, S5_LANES)), 0.0)
        sc_ref[0, base + 8:base + 16, :] = jnp.where(keep, jnp.broadcast_to(pim[d - 1:d], (8, S5_LANES)), 0.0)
    abar_re = pre[0:1]
    abar_im = pim[0:1]
    den = ar * ar + ai * ai
    xr = abar_re - 1.0
    f_re = (xr * ar + abar_im * ai) / den
    f_im = (abar_im * ar - xr * ai) / den
    br = btre_ref[0]
    bi = btim_ref[0]
    bbre_ref[0] = f_re * br - f_im * bi
    bbim_ref[0] = f_re * bi + f_im * br


def _s5_prepare(a_re, a_im, log_dt, b_re, b_im):
    depth = a_re.shape[0]
    flat = lambda t: t.reshape(depth, 1, S5_LANES)
    ldt = jnp.repeat(log_dt, S5_STATE, axis=-1).reshape(depth, 1, S5_LANES)
    bt = lambda t: jnp.transpose(t, (0, 3, 1, 2)).reshape(depth, S5_GROUP_CH, S5_LANES)
    vec = pl.BlockSpec((1, 1, S5_LANES), lambda l: (l, 0, 0))
    mat = pl.BlockSpec((1, S5_GROUP_CH, S5_LANES), lambda l: (l, 0, 0))
    return pl.pallas_call(
        _s5_prep_kernel,
        grid=(depth,),
        in_specs=[vec, vec, vec, mat, mat],
        out_specs=[pl.BlockSpec((1, 64, S5_LANES), lambda l: (l, 0, 0)), mat, mat],
        out_shape=[jax.ShapeDtypeStruct((depth, 64, S5_LANES), F32),
                   jax.ShapeDtypeStruct((depth, S5_GROUP_CH, S5_LANES), F32),
                   jax.ShapeDtypeStruct((depth, S5_GROUP_CH, S5_LANES), F32)],
        name="s5_prep",
    )(flat(a_re), flat(a_im), ldt, bt(b_re), bt(b_im))


def _s5_block_diag_in(bb):
    depth = bb.shape[0]
    same = (jnp.arange(S5_GROUPS)[:, None, None] ==
            (jnp.arange(S5_LANES) // S5_STATE)[None, None, :])
    out = jnp.where(same[None], bb[:, None, :, :], 0.0)
    return out.reshape(depth, S5_WIDTH, S5_LANES).astype(BF16)


def _s5_block_diag_out(c):
    depth = c.shape[0]
    ct = jnp.transpose(c, (0, 1, 3, 2))
    same = jnp.eye(S5_GROUPS, dtype=bool)[None, :, None, :, None]
    out = jnp.where(same, ct[:, :, :, None, :], 0.0)
    return out.reshape(depth, S5_LANES, S5_WIDTH).astype(BF16)


def _inproj_kernel(x_ref, g_ref, w_ref, o_ref):
    x = x_ref[...]
    h = _rmsnorm(x, g_ref[...])
    row = _iota((TM, 1), 0)
    first_tile = pl.program_id(0) == 0
    front = jnp.where(first_tile, FRONT, 0)
    tail = jnp.where(first_tile, 3 * BLK, TM)
    pad = (row < front) | ((row >= BLK) & (row < BLK + front)) | (row >= tail)
    h = jnp.where(pad, 0.0, h)
    o_ref[...] = _dot(h.astype(BF16), w_ref[...])


def _inproj(x, g, w):
    rows = x.shape[0]
    resident = lambda shape: pl.BlockSpec(shape, lambda i: (0, 0), pipeline_mode=pl.Buffered(1))
    return pl.pallas_call(
        _inproj_kernel,
        grid=(rows // TM,),
        in_specs=[pl.BlockSpec((TM, D_MODEL), lambda i: (i, 0)),
                  resident((1, D_MODEL)),
                  resident((D_MODEL, PROJ_W))],
        out_specs=pl.BlockSpec((TM, PROJ_W), lambda i: (i, 0)),
        out_shape=jax.ShapeDtypeStruct((rows, PROJ_W), F32),
        compiler_params=pltpu.CompilerParams(dimension_semantics=("arbitrary",),
                                             vmem_limit_bytes=32 * MIB),
        name="inproj",
    )(x, g, w)


def _outffn_body(mix_ref, x_ref, wo_ref, g2_ref, wg_ref, wu_ref, wd_ref):
    x1 = x_ref[...] + _dot(mix_ref[...].astype(BF16), wo_ref[...])
    h = _rmsnorm(x1, g2_ref[...]).astype(BF16)
    acc = jnp.zeros((TM, D_MODEL), F32)
    for j in range(FFN_HIDDEN // FFN_CHUNK):
        sl = slice(j * FFN_CHUNK, (j + 1) * FFN_CHUNK)
        gate = _dot(h, wg_ref[:, sl])
        up = _dot(h, wu_ref[:, sl])
        acc = acc + _dot((_silu(gate) * up).astype(BF16), wd_ref[sl, :])
    return x1 + acc


def _outffn_kernel(mix_ref, x_ref, wo_ref, g2_ref, wg_ref, wu_ref, wd_ref, o_ref):
    o_ref[...] = _outffn_body(mix_ref, x_ref, wo_ref, g2_ref, wg_ref, wu_ref, wd_ref)


def _outffn_final_kernel(mix_ref, x_ref, wo_ref, g2_ref, wg_ref, wu_ref, wd_ref, gf_ref,
                         ysp_ref, ymain_ref):
    y = _rmsnorm(_outffn_body(mix_ref, x_ref, wo_ref, g2_ref, wg_ref, wu_ref, wd_ref), gf_ref[...])
    ymain_ref[...] = y

    @pl.when(pl.program_id(0) == 0)
    def _():
        ysp_ref[...] = y


def _outffn(mix, x, wo, g2, wg, wu, wd, gf=None):
    rows = x.shape[0]
    resident = lambda shape: pl.BlockSpec(shape, lambda i: (0, 0), pipeline_mode=pl.Buffered(1))
    tile = pl.BlockSpec((TM, D_MODEL), lambda i: (i, 0))
    in_specs = [tile, tile,
                resident((D_MODEL, D_MODEL)), resident((1, D_MODEL)),
                resident((D_MODEL, FFN_HIDDEN)), resident((D_MODEL, FFN_HIDDEN)),
                resident((FFN_HIDDEN, D_MODEL))]
    params = pltpu.CompilerParams(dimension_semantics=("arbitrary",), vmem_limit_bytes=48 * MIB)
    if gf is None:
        return pl.pallas_call(
            _outffn_kernel, grid=(rows // TM,), in_specs=in_specs, out_specs=tile,
            out_shape=jax.ShapeDtypeStruct((rows, D_MODEL), F32),
            compiler_params=params, name="outffn",
        )(mix, x, wo, g2, wg, wu, wd)
    main_tile = pl.BlockSpec((TM, D_MODEL), lambda i: (jnp.maximum(i - 1, 0), 0))
    return pl.pallas_call(
        _outffn_final_kernel, grid=(rows // TM,),
        in_specs=in_specs + [resident((1, D_MODEL))],
        out_specs=[pl.BlockSpec((TM, D_MODEL), lambda i: (0, 0)), main_tile],
        out_shape=[jax.ShapeDtypeStruct((SPECIAL, D_MODEL), F32),
                   jax.ShapeDtypeStruct((rows - SPECIAL, D_MODEL), F32)],
        compiler_params=params, name="outffn_final",
    )(mix, x, wo, g2, wg, wu, wd, gf)


def _s5_tail(y, u, s5d_ref, gluw_ref, glub_ref, s5g_ref):
    y5 = _gelu_tanh(y + s5d_ref[...] * u)
    gate = _dot(y5.astype(BF16), gluw_ref[...]) + glub_ref[...]
    return _rmsnorm(y5 * _sigmoid(gate), s5g_ref[...])


def _pair_block_diag(t):
    left = _iota(t.shape, 1) < 64
    return jnp.concatenate([jnp.where(left, t, 0.0), jnp.where(left, 0.0, t)], axis=0)


def _prompt_mixer_kernel(
        sinks_ref, proj_ref, cos_ref, sin_ref, attg_ref,
        bre_ref, bim_ref, cre_ref, cim_ref, sc_ref, s5d_ref, gluw_ref, glub_ref, s5g_ref,
        convw_ref, convb_ref, dtb_ref, alog_ref, dssd_ref, ssdg_ref,
        mix_ref, kout_ref, vout_ref, s5re_ref, s5im_ref, convout_ref, ssdout_ref,
        kprev, vprev, hre, him, carry_re, carry_im, xbuf, hssd):
    c = pl.program_id(1)

    @pl.when(c == 0)
    def _():
        kprev[...] = jnp.zeros_like(kprev)
        vprev[...] = jnp.zeros_like(vprev)
        carry_re[...] = jnp.zeros_like(carry_re)
        carry_im[...] = jnp.zeros_like(carry_im)
        xbuf[0:8, :] = jnp.zeros((8, SSD_CONV_DIM), F32)
        hssd[...] = jnp.zeros_like(hssd)

    cosv = cos_ref[...]
    sinv = sin_ref[...]
    k_rot = _rope(proj_ref[:, K0:K0 + KV_WIDTH], cosv, sinv)
    v_new = proj_ref[:, V0:V0 + KV_WIDTH]
    kout_ref[0] = k_rot
    vout_ref[0] = v_new
    kk = jnp.concatenate([kprev[...], k_rot], axis=0)
    vv = jnp.concatenate([vprev[...], v_new], axis=0)
    kk_sw = pltpu.roll(kk, 64, 1)
    vv_sw = pltpu.roll(vv, 64, 1)
    left2 = _iota((2 * BLK, 128), 1) < 64
    keys = (jnp.where(left2, kk, kk_sw).astype(BF16),
            jnp.where(left2, kk_sw, kk).astype(BF16))
    vals = (jnp.concatenate([jnp.where(left2, vv, 0.0), jnp.where(left2, 0.0, vv_sw)], axis=0).astype(BF16),
            jnp.concatenate([jnp.where(left2, vv_sw, 0.0), jnp.where(left2, 0.0, vv)], axis=0).astype(BF16))
    kprev[...] = k_rot
    vprev[...] = v_new

    qrow = _iota((4 * BLK, 2 * BLK), 0) & (BLK - 1)
    col = _iota((4 * BLK, 2 * BLK), 1)
    kpos = (c - 1) * BLK + col - FRONT
    ok = (col >= qrow) & (col <= qrow + BLK) & (kpos >= 0)

    o_tiles = []
    for g in range(2):
        qs = []
        for j in (2 * g, 2 * g + 1):
            qt = _rope(proj_ref[:, Q0 + j * 128:Q0 + (j + 1) * 128], cosv, sinv) * (HEAD_DIM ** -0.5)
            qs.append(_pair_block_diag(qt))
        qg = jnp.concatenate(qs, axis=0).astype(BF16)
        s = jnp.where(ok, _dot_nt(qg, keys[g]), NEG_INF)
        sink = jnp.concatenate(
            [jnp.full((BLK, 1), sinks_ref[4 * g + r], F32) for r in range(4)], axis=0)
        m = jnp.maximum(jnp.max(s, axis=-1, keepdims=True), sink)
        p = jnp.exp(s - m)
        den = jnp.sum(p, axis=-1, keepdims=True) + jnp.exp(sink - m)
        w = (p / den).astype(BF16)
        for jj in range(2):
            wpair = jnp.concatenate([w[(2 * jj) * BLK:(2 * jj + 1) * BLK],
                                     w[(2 * jj + 1) * BLK:(2 * jj + 2) * BLK]], axis=1)
            o_tiles.append(_dot(wpair, vals[g]))
    o_att = jnp.concatenate(o_tiles, axis=1)
    mix_ref[:, 0:ATT_WIDTH] = _rmsnorm(o_att, attg_ref[...])

    u = proj_ref[:, U0:U0 + S5_WIDTH]
    ub = u.astype(BF16)
    hre[...] = _dot(ub, bre_ref[...])
    him[...] = _dot(ub, bim_ref[...])
    for lt in range(S5_LANES // 128):
        ls = slice(lt * 128, (lt + 1) * 128)
        p_re = sc_ref[0:8, ls]
        p_im = sc_ref[8:16, ls]
        steps = tuple((d, sc_ref[b0:b0 + 8, ls], sc_ref[b0 + 8:b0 + 16, ls])
                      for d, b0 in ((1, 16), (2, 32), (4, 48)))
        cr = carry_re[:, ls]
        ci = carry_im[:, ls]
        for i in range(BLK // 8):
            rs = slice(8 * i, 8 * i + 8)
            xr = hre[rs, ls]
            xi = him[rs, ls]
            for d, a_r, a_i in steps:
                sr = pltpu.roll(xr, d, 0)
                si = pltpu.roll(xi, d, 0)
                xr, xi = xr + a_r * sr - a_i * si, xi + a_r * si + a_i * sr
            hr = xr + p_re * cr - p_im * ci
            hi = xi + p_re * ci + p_im * cr
            hre[rs, ls] = hr
            him[rs, ls] = hi
            cr = jnp.broadcast_to(hr[7:8], (8, 128))
            ci = jnp.broadcast_to(hi[7:8], (8, 128))
        carry_re[:, ls] = cr
        carry_im[:, ls] = ci
    s5re_ref[0] = carry_re[0:1, :]
    s5im_ref[0] = carry_im[0:1, :]
    y = _dot(hre[...].astype(BF16), cre_ref[...]) - _dot(him[...].astype(BF16), cim_ref[...])
    mix_ref[:, ATT_WIDTH:ATT_WIDTH + S5_WIDTH] = _s5_tail(y, u, s5d_ref, gluw_ref, glub_ref, s5g_ref)

    xbc = proj_ref[:, XBC0:XBC0 + SSD_CONV_DIM]
    xbuf[8:8 + BLK, :] = xbc
    conv = convb_ref[...]
    for j in range(SSD_CONV):
        conv = conv + xbuf[5 + j:5 + j + BLK, :] * convw_ref[j:j + 1, :]
    xc = _silu(conv)
    convout_ref[0] = xbc[BLK - (SSD_CONV - 1):BLK, :]
    xbuf[0:8, :] = xbc[BLK - 8:BLK, :]
    xs = xc[:, 0:SSD_WIDTH]
    bm = xc[:, SSD_WIDTH:SSD_WIDTH + 128]
    cm = xc[:, SSD_WIDTH + 128:SSD_WIDTH + 256]

    lane = _iota((BLK, 128), 1)
    row = _iota((BLK, 128), 0)
    live = (lane < SSD_HEADS) & (row >= jnp.where(c == 0, FRONT, 0))
    dt = jnp.where(live, _softplus(proj_ref[:, DT0:DT0 + 128] + dtb_ref[...]), 0.0)
    dta = dt * (-jnp.exp(alog_ref[...]))
    causal = lane <= row
    tri = jnp.where(causal, 1.0, 0.0).astype(BF16)
    hi3, mid3, lo3 = _split3(dta)
    cs = _dot(tri, hi3) + _dot(tri, mid3) + _dot(tri, lo3)
    cs_t = cs.T
    cs_last = cs[BLK - 1:BLK, :]
    heads64 = (SSD_HEAD_DIM,) * SSD_HEADS
    xd = xs * _lane_bcast(dt, heads64)
    dxd = xd * _lane_bcast(jnp.exp(cs_last - cs), heads64)
    ecs = _lane_bcast(jnp.exp(cs), heads64)

    bmb = bm.astype(BF16)
    left = lane < 64
    cb = (_dot_nt(jnp.where(left, cm, 0.0).astype(BF16), bmb),
          _dot_nt(jnp.where(left, 0.0, cm).astype(BF16), bmb))
    scores = []
    for h in range(SSD_HEADS):
        seg = cs[:, h:h + 1] - cs_t[h:h + 1, :]
        scores.append((cb[h // 2] * jnp.exp(jnp.where(causal, seg, NEG_INF))).astype(BF16))
    y_diag = jnp.concatenate(
        [_dot(jnp.concatenate([scores[2 * j], scores[2 * j + 1]], axis=1),
              _pair_block_diag(xd[:, j * 128:(j + 1) * 128]).astype(BF16)) for j in range(2)], axis=1)

    h_prev = hssd[...]
    y_off = _dot_nt(cm.astype(BF16), h_prev.astype(BF16)) * ecs
    states = _dot(dxd.T.astype(BF16), bmb)
    own = (_iota((SSD_WIDTH, 128), 0) >> 7) == (_iota((SSD_WIDTH, 128), 1) >> 6)
    cd = jnp.exp(cs_last)
    cdm = jnp.concatenate(
        [jnp.broadcast_to(cd[:, h:h + 1], (SSD_HEAD_DIM, 128)) for h in range(SSD_HEADS)], axis=0)
    h_new = cdm * h_prev + jnp.where(own, states, 0.0)
    hssd[...] = h_new
    for h in range(SSD_HEADS):
        g0 = (h // 2) * SSD_STATE
        ssdout_ref[0, h] = h_new[h * SSD_HEAD_DIM:(h + 1) * SSD_HEAD_DIM, g0:g0 + SSD_STATE]

    yssd = y_diag + y_off + dssd_ref[...] * xs
    yc = yssd * _silu(proj_ref[:, Z0:Z0 + SSD_WIDTH])
    mix_ref[:, ATT_WIDTH + S5_WIDTH:] = _rmsnorm(yc, ssdg_ref[...])


def _prompt_mixers(proj, lp, tabs, batch, nc):
    rows = proj.shape[0]

    def blk(b, c):
        return jnp.where(c == 0, b, 4 + b * (nc - 1) + c - 1)

    const = lambda shape: pl.BlockSpec(shape, lambda b, c: (0,) * len(shape))
    in_specs = [
        pl.BlockSpec(memory_space=pltpu.SMEM),
        pl.BlockSpec((BLK, PROJ_W), lambda b, c: (blk(b, c), 0)),
        pl.BlockSpec((BLK, 128), lambda b, c: (c, 0)),
        pl.BlockSpec((BLK, 128), lambda b, c: (c, 0)),
        const((1, ATT_WIDTH)),
        const((S5_WIDTH, S5_LANES)), const((S5_WIDTH, S5_LANES)),
        const((S5_LANES, S5_WIDTH)), const((S5_LANES, S5_WIDTH)),
        const((64, S5_LANES)), const((1, S5_WIDTH)), const((S5_WIDTH, S5_WIDTH)),
        const((1, S5_WIDTH)), const((1, S5_WIDTH)),
        const((SSD_CONV, SSD_CONV_DIM)), const((1, SSD_CONV_DIM)),
        const((1, 128)), const((1, 128)), const((1, SSD_WIDTH)), const((1, SSD_WIDTH)),
    ]
    per_b = lambda shape: pl.BlockSpec((1,) + shape, lambda b, c: (b,) + (0,) * len(shape))
    out_specs = [
        pl.BlockSpec((BLK, D_MODEL), lambda b, c: (blk(b, c), 0)),
        per_b((BLK, KV_WIDTH)), per_b((BLK, KV_WIDTH)),
        per_b((1, S5_LANES)), per_b((1, S5_LANES)),
        per_b((SSD_CONV - 1, SSD_CONV_DIM)),
        per_b((SSD_HEADS, SSD_HEAD_DIM, SSD_STATE)),
    ]
    out_shape = [
        jax.ShapeDtypeStruct((rows, D_MODEL), F32),
        jax.ShapeDtypeStruct((batch, BLK, KV_WIDTH), F32),
        jax.ShapeDtypeStruct((batch, BLK, KV_WIDTH), F32),
        jax.ShapeDtypeStruct((batch, 1, S5_LANES), F32),
        jax.ShapeDtypeStruct((batch, 1, S5_LANES), F32),
        jax.ShapeDtypeStruct((batch, SSD_CONV - 1, SSD_CONV_DIM), F32),
        jax.ShapeDtypeStruct((batch, SSD_HEADS, SSD_HEAD_DIM, SSD_STATE), F32),
    ]
    scratch = [
        pltpu.VMEM((BLK, KV_WIDTH), F32), pltpu.VMEM((BLK, KV_WIDTH), F32),
        pltpu.VMEM((BLK, S5_LANES), F32), pltpu.VMEM((BLK, S5_LANES), F32),
        pltpu.VMEM((8, S5_LANES), F32), pltpu.VMEM((8, S5_LANES), F32),
        pltpu.VMEM((BLK + 8, SSD_CONV_DIM), F32),
        pltpu.VMEM((SSD_WIDTH, 128), F32),
    ]
    return pl.pallas_call(
        _prompt_mixer_kernel,
        grid=(batch, nc),
        in_specs=in_specs, out_specs=out_specs, out_shape=out_shape, scratch_shapes=scratch,
        compiler_params=pltpu.CompilerParams(dimension_semantics=("arbitrary", "arbitrary"),
                                             vmem_limit_bytes=40 * MIB),
        name="prompt_mixers",
    )(lp["sinks"], proj, tabs["cos_p"], tabs["sin_p"], lp["attn_out_g"],
      lp["bre"], lp["bim"], lp["cre"], lp["cim"], lp["sc"], lp["s5_d"], lp["glu_w"], lp["glu_b"],
      lp["s5_out_g"], lp["conv_w"], lp["conv_b"], lp["dt_bias"], lp["a_log"], lp["ssd_d"],
      lp["ssd_norm_g"])


def _decode_mixer_kernel(
        sinks_ref, mix_any, proj_ref, cos_ref, sin_ref, attg_ref,
        bre_ref, bim_ref, cre_ref, cim_ref, sc_ref, s5d_ref, gluw_ref, glub_ref, s5g_ref,
        convw_ref, convb_ref, dtb_ref, alog_ref, dssd_ref, ssdg_ref,
        ck_ref, cv_ref, s5re_in, s5im_in, conv_in, ssd_in,
        mix_ref, kout_ref, vout_ref, s5re_ref, s5im_ref, convout_ref, ssdout_ref,
        y_scr):
    del mix_any
    i = pl.program_id(0)

    @pl.when(i >= DEC // DCH)
    def _():
        mix_ref[...] = jnp.zeros_like(mix_ref)

    @pl.when(i < DEC // DCH)
    def _():
        cosv = cos_ref[...]
        sinv = sin_ref[...]
        k_rot = _rope(proj_ref[:, K0:K0 + KV_WIDTH], cosv, sinv)
        v_new = proj_ref[:, V0:V0 + KV_WIDTH]
        for b in range(DCH):
            kout_ref[b, 0:BLK - 1, :] = ck_ref[b, 1:BLK, :]
            kout_ref[b, BLK - 1:BLK, :] = k_rot[b:b + 1, :]
            vout_ref[b, 0:BLK - 1, :] = cv_ref[b, 1:BLK, :]
            vout_ref[b, BLK - 1:BLK, :] = v_new[b:b + 1, :]
        left = _iota((DCH, 128), 1) < 64
        qs = []
        for h in range(ATT_HEADS):
            j, e, g = h // 2, h % 2, h // 4
            qt = _rope(proj_ref[:, Q0 + j * 128:Q0 + (j + 1) * 128], cosv, sinv) * (HEAD_DIM ** -0.5)
            if e != g:
                qt = pltpu.roll(qt, 64, 1)
            qs.append(jnp.where(left == (g == 0), qt, 0.0))
        qx = jnp.concatenate(qs, axis=0).astype(BF16)
        kmat = ck_ref[...].reshape(DCH * BLK, KV_WIDTH).astype(BF16)
        vmat = cv_ref[...].reshape(DCH * BLK, KV_WIDTH).astype(BF16)
        s_old = _dot_nt(qx, kmat)
        s_new = _dot_nt(qx, k_rot.astype(BF16))
        rseq = _iota((ATT_HEADS * DCH, DCH * BLK), 0) & (DCH - 1)
        same = rseq == (_iota((ATT_HEADS * DCH, DCH * BLK), 1) >> 7)
        same_new = (_iota((ATT_HEADS * DCH, DCH), 0) & (DCH - 1)) == _iota((ATT_HEADS * DCH, DCH), 1)
        s_old = jnp.where(same, s_old, NEG_INF)
        s_new = jnp.where(same_new, s_new, NEG_INF)
        sink = jnp.concatenate(
            [jnp.full((DCH, 1), sinks_ref[h], F32) for h in range(ATT_HEADS)], axis=0)
        m = jnp.maximum(jnp.maximum(jnp.max(s_old, axis=-1, keepdims=True),
                                    jnp.max(s_new, axis=-1, keepdims=True)), sink)
        p_old = jnp.exp(s_old - m)
        p_new = jnp.exp(s_new - m)
        den = (jnp.sum(p_old, axis=-1, keepdims=True) + jnp.sum(p_new, axis=-1, keepdims=True)
               + jnp.exp(sink - m))
        o = (_dot((p_old / den).astype(BF16), vmat)
             + _dot((p_new / den).astype(BF16), v_new.astype(BF16)))
        o_tiles = []
        for j in range(ATT_HEADS // 2):
            g = j // 2
            a = o[(2 * j) * DCH:(2 * j + 1) * DCH]
            bb = o[(2 * j + 1) * DCH:(2 * j + 2) * DCH]
            if g == 1:
                a = pltpu.roll(a, 64, 1)
            else:
                bb = pltpu.roll(bb, 64, 1)
            o_tiles.append(jnp.where(left, a, bb))
        o_att = jnp.concatenate(o_tiles, axis=1)
        mix_ref[:, 0:ATT_WIDTH] = _rmsnorm(o_att, attg_ref[...])

        u = proj_ref[:, U0:U0 + S5_WIDTH]
        ub = u.astype(BF16)
        a_re = sc_ref[0:1, :]
        a_im = sc_ref[8:9, :]
        h0r = s5re_in[...]
        h0i = s5im_in[...]
        hr = _dot(ub, bre_ref[...]) + a_re * h0r - a_im * h0i
        hi = _dot(ub, bim_ref[...]) + a_re * h0i + a_im * h0r
        s5re_ref[...] = hr
        s5im_ref[...] = hi
        y = _dot(hr.astype(BF16), cre_ref[...]) - _dot(hi.astype(BF16), cim_ref[...])
        mix_ref[:, ATT_WIDTH:ATT_WIDTH + S5_WIDTH] = _s5_tail(y, u, s5d_ref, gluw_ref, glub_ref, s5g_ref)

        xbc = proj_ref[:, XBC0:XBC0 + SSD_CONV_DIM]
        conv = convb_ref[...]
        for j in range(SSD_CONV - 1):
            conv = conv + conv_in[:, j * SSD_CONV_DIM:(j + 1) * SSD_CONV_DIM] * convw_ref[j:j + 1, :]
        conv = conv + xbc * convw_ref[SSD_CONV - 1:SSD_CONV, :]
        convout_ref[:, 0:2 * SSD_CONV_DIM] = conv_in[:, SSD_CONV_DIM:3 * SSD_CONV_DIM]
        convout_ref[:, 2 * SSD_CONV_DIM:] = xbc
        xc = _silu(conv)
        xs = xc[:, 0:SSD_WIDTH]
        bm = xc[:, SSD_WIDTH:SSD_WIDTH + 128]
        cm = xc[:, SSD_WIDTH + 128:SSD_WIDTH + 256]
        dt = _softplus(proj_ref[:, DT0:DT0 + 128] + dtb_ref[...])
        decay = jnp.exp(dt * (-jnp.exp(alog_ref[...])))
        heads64 = (SSD_HEAD_DIM,) * SSD_HEADS
        xd = xs * _lane_bcast(dt[:, 0:SSD_HEADS], heads64)
        eye = jnp.where(_iota((SSD_WIDTH, SSD_WIDTH), 0) == _iota((SSD_WIDTH, SSD_WIDTH), 1), 1.0, 0.0)
        upper = _iota((1, SSD_WIDTH), 1) < 128
        for b in range(DCH):
            diag = (eye * xd[b:b + 1, :]).astype(BF16)
            bexp = jnp.concatenate(
                [jnp.broadcast_to(bm[b:b + 1, 0:SSD_STATE], (128, SSD_STATE)),
                 jnp.broadcast_to(bm[b:b + 1, SSD_STATE:2 * SSD_STATE], (128, SSD_STATE))], axis=0)
            outer = _dot(diag, bexp.astype(BF16))
            dcol = jnp.concatenate(
                [jnp.broadcast_to(decay[b:b + 1, h:h + 1], (SSD_HEAD_DIM, SSD_STATE))
                 for h in range(SSD_HEADS)], axis=0)
            h_new = dcol * ssd_in[b] + outer
            ssdout_ref[b] = h_new
            c2 = jnp.concatenate([cm[b:b + 1, 0:SSD_STATE], cm[b:b + 1, SSD_STATE:2 * SSD_STATE],
                                  jnp.zeros((14, SSD_STATE), F32)], axis=0)
            r = _dot_nt(c2.astype(BF16), h_new.astype(BF16))
            y_scr[b:b + 1, :] = jnp.where(upper, r[0:1, :], r[1:2, :])
        yssd = y_scr[...] + dssd_ref[...] * xs
        yc = yssd * _silu(proj_ref[:, Z0:Z0 + SSD_WIDTH])
        mix_ref[:, ATT_WIDTH + S5_WIDTH:] = _rmsnorm(yc, ssdg_ref[...])


def _decode_mixers(mix, proj, lp, tabs, ck, cv, s5re, s5im, conv, ssd):
    rows = proj.shape[0]
    nreal = DEC // DCH
    dec_blk0 = DEC_ROW0 // DCH
    clamp = lambda i: jnp.minimum(i, nreal - 1)
    const = lambda shape: pl.BlockSpec(shape, lambda i: (0,) * len(shape))
    seq2 = lambda w: pl.BlockSpec((DCH, w), lambda i: (clamp(i), 0))
    seq3 = lambda a, w: pl.BlockSpec((DCH, a, w), lambda i: (clamp(i), 0, 0))
    in_specs = [
        pl.BlockSpec(memory_space=pltpu.SMEM),
        pl.BlockSpec(memory_space=pl.ANY),
        pl.BlockSpec((DCH, PROJ_W), lambda i: (dec_blk0 + clamp(i), 0)),
        const((1, 128)), const((1, 128)),
        const((1, ATT_WIDTH)),
        const((S5_WIDTH, S5_LANES)), const((S5_WIDTH, S5_LANES)),
        const((S5_LANES, S5_WIDTH)), const((S5_LANES, S5_WIDTH)),
        const((64, S5_LANES)), const((1, S5_WIDTH)), const((S5_WIDTH, S5_WIDTH)),
        const((1, S5_WIDTH)), const((1, S5_WIDTH)),
        const((SSD_CONV, SSD_CONV_DIM)), const((1, SSD_CONV_DIM)),
        const((1, 128)), const((1, 128)), const((1, SSD_WIDTH)), const((1, SSD_WIDTH)),
        seq3(BLK, KV_WIDTH), seq3(BLK, KV_WIDTH),
        seq2(S5_LANES), seq2(S5_LANES), seq2((SSD_CONV - 1) * SSD_CONV_DIM),
        seq3(SSD_WIDTH, SSD_STATE),
    ]
    out_specs = [
        pl.BlockSpec((DCH, D_MODEL), lambda i: (dec_blk0 + i, 0)),
        seq3(BLK, KV_WIDTH), seq3(BLK, KV_WIDTH),
        seq2(S5_LANES), seq2(S5_LANES), seq2((SSD_CONV - 1) * SSD_CONV_DIM),
        seq3(SSD_WIDTH, SSD_STATE),
    ]
    out_shape = [
        jax.ShapeDtypeStruct((rows, D_MODEL), F32),
        jax.ShapeDtypeStruct((DEC, BLK, KV_WIDTH), F32),
        jax.ShapeDtypeStruct((DEC, BLK, KV_WIDTH), F32),
        jax.ShapeDtypeStruct((DEC, S5_LANES), F32),
        jax.ShapeDtypeStruct((DEC, S5_LANES), F32),
        jax.ShapeDtypeStruct((DEC, (SSD_CONV - 1) * SSD_CONV_DIM), F32),
        jax.ShapeDtypeStruct((DEC, SSD_WIDTH, SSD_STATE), F32),
    ]
    return pl.pallas_call(
        _decode_mixer_kernel,
        grid=(2 * nreal,),
        in_specs=in_specs, out_specs=out_specs, out_shape=out_shape,
        scratch_shapes=[pltpu.VMEM((DCH, SSD_WIDTH), F32)],
        input_output_aliases={1: 0},
        compiler_params=pltpu.CompilerParams(dimension_semantics=("arbitrary",),
                                             vmem_limit_bytes=40 * MIB),
        name="decode_mixers",
    )(lp["sinks"], mix, proj, tabs["cos_d"], tabs["sin_d"], lp["attn_out_g"],
      lp["bre"], lp["bim"], lp["cre"], lp["cim"], lp["sc"], lp["s5_d"], lp["glu_w"], lp["glu_b"],
      lp["s5_out_g"], lp["conv_w"], lp["conv_b"], lp["dt_bias"], lp["a_log"], lp["ssd_d"],
      lp["ssd_norm_g"], ck, cv, s5re, s5im, conv, ssd)


def _rope_tables(nc):
    half = HEAD_DIM // 2
    inv = ROPE_THETA ** (-jnp.arange(half, dtype=F32) / half)

    def tab(pos):
        ang = pos.astype(F32)[:, None] * inv[None, :]
        cos = jnp.cos(ang)
        sin = jnp.sin(ang)
        return jnp.tile(cos, (1, 4)), jnp.concatenate([-sin, sin, -sin, sin], axis=1)

    cos_p, sin_p = tab(jnp.arange(nc * BLK, dtype=jnp.int32) - FRONT)
    cos_d, sin_d = tab(jnp.full((1,), PAST_LEN, dtype=jnp.int32))
    return {"cos_p": cos_p, "sin_p": sin_p, "cos_d": cos_d, "sin_d": sin_d}


def kernel(x_prompt, x_sample, cache_k, cache_v, state_s5_re, state_s5_im, state_ssd_conv, state_ssd,
           meta_tokens, ln1_g, w_in, attn_sinks, attn_out_g, s5_a_re, s5_a_im, s5_log_dt,
           s5_b_re, s5_b_im, s5_c_re, s5_c_im, s5_d, s5_glu_w, s5_glu_b, s5_out_g,
           ssd_conv_w, ssd_conv_b, ssd_dt_bias, ssd_a_log, ssd_d, ssd_norm_g, w_out,
           ln2_g, w_gate, w_up, w_down, lnf_g):
    batch, seq, _ = x_prompt.shape
    depth = w_in.shape[0]
    assert batch == 2 and x_sample.shape[0] == DEC and x_sample.shape[1] == 1
    assert seq % TM == 0 and cache_k.shape[2] == BLK
    nc = seq // BLK + 1

    zeros_front = jnp.zeros((FRONT, D_MODEL), F32)
    x = jnp.concatenate([zeros_front, meta_tokens, zeros_front, meta_tokens,
                         x_sample.reshape(DEC, D_MODEL), jnp.zeros((BLK, D_MODEL), F32),
                         x_prompt.reshape(batch * seq, D_MODEL)], axis=0)

    tabs = _rope_tables(nc)
    sc, bb_re, bb_im = _s5_prepare(s5_a_re, s5_a_im, s5_log_dt, s5_b_re, s5_b_im)
    bre, bim = _s5_block_diag_in(bb_re), _s5_block_diag_in(bb_im)
    cre, cim = _s5_block_diag_out(s5_c_re), _s5_block_diag_out(s5_c_im)
    w_in_p = jnp.pad(w_in, ((0, 0), (0, 0), (0, PROJ_W - N_IN))).astype(BF16)
    w_out_b, w_gate_b, w_up_b, w_down_b = (t.astype(BF16) for t in (w_out, w_gate, w_up, w_down))
    glu_w_b = s5_glu_w.astype(BF16)
    pad_heads = lambda t: jnp.pad(t, ((0, 0), (0, 128 - SSD_HEADS)))[:, None, :]
    dt_bias_p, a_log_p = pad_heads(ssd_dt_bias), pad_heads(ssd_a_log)
    ssd_d_p = jnp.repeat(ssd_d, SSD_HEAD_DIM, axis=-1)[:, None, :]
    row = lambda t: t[:, None, :]

    ck = cache_k.reshape(depth, DEC, BLK, KV_WIDTH)
    cv = cache_v.reshape(depth, DEC, BLK, KV_WIDTH)
    s5re0 = state_s5_re.reshape(depth, DEC, S5_LANES)
    s5im0 = state_s5_im.reshape(depth, DEC, S5_LANES)
    conv0 = state_ssd_conv.reshape(depth, DEC, (SSD_CONV - 1) * SSD_CONV_DIM)
    ssd0 = state_ssd.reshape(depth, DEC, SSD_WIDTH, SSD_STATE)

    outs_p = [[] for _ in range(6)]
    outs_s = [[] for _ in range(6)]
    y_special = y_main = None
    for l in range(depth):
        lp = {
            "sinks": attn_sinks[l], "attn_out_g": row(attn_out_g)[l],
            "bre": bre[l], "bim": bim[l], "cre": cre[l], "cim": cim[l], "sc": sc[l],
            "s5_d": row(s5_d)[l], "glu_w": glu_w_b[l], "glu_b": row(s5_glu_b)[l],
            "s5_out_g": row(s5_out_g)[l],
            "conv_w": ssd_conv_w[l], "conv_b": row(ssd_conv_b)[l],
            "dt_bias": dt_bias_p[l], "a_log": a_log_p[l], "ssd_d": ssd_d_p[l],
            "ssd_norm_g": row(ssd_norm_g)[l],
        }
        proj = _inproj(x, row(ln1_g)[l], w_in_p[l])
        res_p = _prompt_mixers(proj, lp, tabs, batch, nc)
        res_s = _decode_mixers(res_p[0], proj, lp, tabs, ck[l], cv[l], s5re0[l], s5im0[l],
                               conv0[l], ssd0[l])
        for i in range(6):
            outs_p[i].append(res_p[i + 1])
            outs_s[i].append(res_s[i + 1])
        ffn_args = (res_s[0], x, w_out_b[l], row(ln2_g)[l], w_gate_b[l], w_up_b[l], w_down_b[l])
        if l + 1 < depth:
            x = _outffn(*ffn_args)
        else:
            y_special, y_main = _outffn(*ffn_args, gf=lnf_g[None, :])

    y_prompt = y_main.reshape(batch, seq, D_MODEL)
    y_sample = y_special[DEC_ROW0:DEC_ROW0 + DEC].reshape(DEC, 1, D_MODEL)
    kv_p = lambda ts: jnp.stack(ts).reshape(depth, batch, BLK, 2, HEAD_DIM)
    kv_s = lambda ts: jnp.stack(ts).reshape(depth, DEC, BLK, 2, HEAD_DIM)
    s5_p = lambda ts: jnp.stack(ts).reshape(depth, batch, S5_GROUPS, S5_STATE)
    s5_s = lambda ts: jnp.stack(ts).reshape(depth, DEC, S5_GROUPS, S5_STATE)
    return (y_prompt, y_sample,
            kv_p(outs_p[0]), kv_p(outs_p[1]), s5_p(outs_p[2]), s5_p(outs_p[3]),
            jnp.stack(outs_p[4]), jnp.stack(outs_p[5]),
            kv_s(outs_s[0]), kv_s(outs_s[1]), s5_s(outs_s[2]), s5_s(outs_s[3]),
            jnp.stack(outs_s[4]).reshape(depth, DEC, SSD_CONV - 1, SSD_CONV_DIM),
            jnp.stack(outs_s[5]).reshape(depth, DEC, SSD_HEADS, SSD_HEAD_DIM, SSD_STATE))
```

```python
import functools

import jax
import jax.numpy as jnp
from jax import lax
from jax.experimental import pallas as pl
from jax.experimental.pallas import tpu as pltpu

F32 = jnp.float32
BF16 = jnp.bfloat16

D_MODEL = 1024
N_META = 16
HEAD_DIM = 64
ATT_WIDTH = 512
ATT_HEADS = 8
KV_WIDTH = 128
S5_WIDTH = 256
S5_GROUPS = 16
S5_GROUP_CH = 16
S5_STATE = 64
S5_LANES = S5_GROUPS * S5_STATE
SSD_WIDTH = 256
SSD_HEADS = 4
SSD_HEAD_DIM = 64
SSD_STATE = 64
SSD_CONV = 4
SSD_CONV_DIM = 512
FFN_HIDDEN = 2816
NORM_EPS = 1e-6
ROPE_THETA = 10000.0
PAST_LEN = 8192
N_IN = 1796

BLK = 128
FRONT = BLK - N_META
TM = 512
SPECIAL = 4 * BLK
DEC = 128
DEC_ROW0 = 2 * BLK
DCH = 16
PROJ_W = 1920
Q0, K0, V0, U0, Z0, XBC0, DT0 = 0, 512, 640, 768, 1024, 1280, 1792
FFN_CHUNK = 256
NEG_INF = float("-inf")
MIB = 1024 * 1024


def _dot(a, b):
    return jnp.dot(a, b, preferred_element_type=F32)


def _dot_nt(a, b):
    return lax.dot_general(a, b, (((1,), (1,)), ((), ())), preferred_element_type=F32)


def _sigmoid(x):
    return 1.0 / (1.0 + jnp.exp(-x))


def _silu(x):
    return x * _sigmoid(x)


def _softplus(x):
    return jnp.maximum(x, 0.0) + jnp.log1p(jnp.exp(-jnp.abs(x)))


def _gelu_tanh(x):
    cdf = 0.5 * (1.0 + jnp.tanh(0.7978845608028654 * (x + 0.044715 * (x * x * x))))
    return x * cdf


def _rmsnorm(x, g):
    return x * lax.rsqrt(jnp.mean(x * x, axis=-1, keepdims=True) + NORM_EPS) * g


def _iota(shape, dim):
    return lax.broadcasted_iota(jnp.int32, shape, dim)


def _rope(x, cosv, sinv):
    first = (_iota(x.shape, 1) & 63) < 32
    partner = jnp.where(first, pltpu.roll(x, 96, 1), pltpu.roll(x, 32, 1))
    return x * cosv + partner * sinv


def _lane_bcast(x, widths):
    rows = x.shape[0]
    return jnp.concatenate(
        [jnp.broadcast_to(x[:, h:h + 1], (rows, w)) for h, w in enumerate(widths)], axis=1)


def _split3(x):
    hi = x.astype(BF16)
    r1 = x - hi.astype(F32)
    mid = r1.astype(BF16)
    lo = (r1 - mid.astype(F32)).astype(BF16)
    return hi, mid, lo


def _s5_prep_kernel(are_ref, aim_ref, ldt_ref, btre_ref, btim_ref, sc_ref, bbre_ref, bbim_ref):
    ar = are_ref[0]
    ai = aim_ref[0]
    dt = jnp.exp(ldt_ref[0])
    row = _iota((8, S5_LANES), 0)
    k = (row + 1).astype(F32)
    mag = jnp.exp((k * dt) * ar)
    ang = (k * dt) * ai
    pre = mag * jnp.cos(ang)
    pim = mag * jnp.sin(ang)
    sc_ref[0, 0:8, :] = pre
    sc_ref[0, 8:16, :] = pim
    for d, base in ((1, 16), (2, 32), (4, 48)):
        keep = row >= d
        sc_ref[0, base:base + 8, :] = jnp.where(keep, jnp.broadcast_to(pre[d - 1:d], (8, S5_LANES)), 0.0)
        sc_ref[0, base + 8:base + 16, :] = jnp.where(keep, jnp.broadcast_to(pim[d - 1:d], (8, S5_LANES)), 0.0)
    abar_re = pre[0:1]
    abar_im = pim[0:1]
    den = ar * ar + ai * ai
    xr = abar_re - 1.0
    f_re = (xr * ar + abar_im * ai) / den
    f_im = (abar_im * ar - xr * ai) / den
    br = btre_ref[0]
    bi = btim_ref[0]
    bbre_ref[0] = f_re * br - f_im * bi
    bbim_ref[0] = f_re * bi + f_im * br


def _s5_prepare(a_re, a_im, log_dt, b_re, b_im):
    depth = a_re.shape[0]
    flat = lambda t: t.reshape(depth, 1, S5_LANES)
    ldt = jnp.repeat(log_dt, S5_STATE, axis=-1).reshape(depth, 1, S5_LANES)
    bt = lambda t: jnp.transpose(t, (0, 3, 1, 2)).reshape(depth, S5_GROUP_CH, S5_LANES)
    vec = pl.BlockSpec((1, 1, S5_LANES), lambda l: (l, 0, 0))
    mat = pl.BlockSpec((1, S5_GROUP_CH, S5_LANES), lambda l: (l, 0, 0))
    return pl.pallas_call(
        _s5_prep_kernel,
        grid=(depth,),
        in_specs=[vec, vec, vec, mat, mat],
        out_specs=[pl.BlockSpec((1, 64, S5_LANES), lambda l: (l, 0, 0)), mat, mat],
        out_shape=[jax.ShapeDtypeStruct((depth, 64, S5_LANES), F32),
                   jax.ShapeDtypeStruct((depth, S5_GROUP_CH, S5_LANES), F32),
                   jax.ShapeDtypeStruct((depth, S5_GROUP_CH, S5_LANES), F32)],
        name="s5_prep",
    )(flat(a_re), flat(a_im), ldt, bt(b_re), bt(b_im))


def _s5_block_diag_in(bb):
    depth = bb.shape[0]
    same = (jnp.arange(S5_GROUPS)[:, None, None] ==
            (jnp.arange(S5_LANES) // S5_STATE)[None, None, :])
    out = jnp.where(same[None], bb[:, None, :, :], 0.0)
    return out.reshape(depth, S5_WIDTH, S5_LANES).astype(BF16)


def _s5_block_diag_out(c):
    depth = c.shape[0]
    ct = jnp.transpose(c, (0, 1, 3, 2))
    same = jnp.eye(S5_GROUPS, dtype=bool)[None, :, None, :, None]
    out = jnp.where(same, ct[:, :, :, None, :], 0.0)
    return out.reshape(depth, S5_LANES, S5_WIDTH).astype(BF16)


def _inproj_kernel(x_ref, g_ref, w_ref, o_ref):
    x = x_ref[...]
    h = _rmsnorm(x, g_ref[...])
    row = _iota((TM, 1), 0)
    first_tile = pl.program_id(0) == 0
    front = jnp.where(first_tile, FRONT, 0)
    tail = jnp.where(first_tile, 3 * BLK, TM)
    pad = (row < front) | ((row >= BLK) & (row < BLK + front)) | (row >= tail)
    h = jnp.where(pad, 0.0, h)
    o_ref[...] = _dot(h.astype(BF16), w_ref[...])


def _layer_resident(layer, shape):
    return pl.BlockSpec((None,) + shape, lambda i: (layer,) + (0,) * len(shape),
                        pipeline_mode=pl.Buffered(1))


def _inproj(x, g, w, layer):
    rows = x.shape[0]
    resident = functools.partial(_layer_resident, layer)
    return pl.pallas_call(
        _inproj_kernel,
        grid=(rows // TM,),
        in_specs=[pl.BlockSpec((TM, D_MODEL), lambda i: (i, 0)),
                  resident((1, D_MODEL)),
                  resident((D_MODEL, PROJ_W))],
        out_specs=pl.BlockSpec((TM, PROJ_W), lambda i: (i, 0)),
        out_shape=jax.ShapeDtypeStruct((rows, PROJ_W), F32),
        compiler_params=pltpu.CompilerParams(dimension_semantics=("arbitrary",),
                                             vmem_limit_bytes=32 * MIB),
        name="inproj",
    )(x, g, w)


def _outffn_body(mix_ref, x_ref, wo_ref, g2_ref, wg_ref, wu_ref, wd_ref):
    x1 = x_ref[...] + _dot(mix_ref[...].astype(BF16), wo_ref[...])
    h = _rmsnorm(x1, g2_ref[...]).astype(BF16)
    acc = jnp.zeros((TM, D_MODEL), F32)
    for j in range(FFN_HIDDEN // FFN_CHUNK):
        sl = slice(j * FFN_CHUNK, (j + 1) * FFN_CHUNK)
        gate = _dot(h, wg_ref[:, sl])
        up = _dot(h, wu_ref[:, sl])
        acc = acc + _dot((_silu(gate) * up).astype(BF16), wd_ref[sl, :])
    return x1 + acc


def _outffn_kernel(mix_ref, x_ref, wo_ref, g2_ref, wg_ref, wu_ref, wd_ref, o_ref):
    o_ref[...] = _outffn_body(mix_ref, x_ref, wo_ref, g2_ref, wg_ref, wu_ref, wd_ref)


def _outffn_final_kernel(mix_ref, x_ref, wo_ref, g2_ref, wg_ref, wu_ref, wd_ref, gf_ref,
                         ysp_ref, ymain_ref):
    y = _rmsnorm(_outffn_body(mix_ref, x_ref, wo_ref, g2_ref, wg_ref, wu_ref, wd_ref), gf_ref[...])
    ymain_ref[...] = y

    @pl.when(pl.program_id(0) == 0)
    def _():
        ysp_ref[...] = y


def _outffn(mix, x, wo, g2, wg, wu, wd, layer, gf=None):
    rows = x.shape[0]
    resident = functools.partial(_layer_resident, layer)
    tile = pl.BlockSpec((TM, D_MODEL), lambda i: (i, 0))
    in_specs = [tile, tile,
                resident((D_MODEL, D_MODEL)), resident((1, D_MODEL)),
                resident((D_MODEL, FFN_HIDDEN)), resident((D_MODEL, FFN_HIDDEN)),
                resident((FFN_HIDDEN, D_MODEL))]
    params = pltpu.CompilerParams(dimension_semantics=("arbitrary",), vmem_limit_bytes=48 * MIB)
    if gf is None:
        return pl.pallas_call(
            _outffn_kernel, grid=(rows // TM,), in_specs=in_specs, out_specs=tile,
            out_shape=jax.ShapeDtypeStruct((rows, D_MODEL), F32),
            compiler_params=params, name="outffn",
        )(mix, x, wo, g2, wg, wu, wd)
    main_tile = pl.BlockSpec((TM, D_MODEL), lambda i: (jnp.maximum(i - 1, 0), 0))
    return pl.pallas_call(
        _outffn_final_kernel, grid=(rows // TM,),
        in_specs=in_specs + [pl.BlockSpec((1, D_MODEL), lambda i: (0, 0))],
        out_specs=[pl.BlockSpec((TM, D_MODEL), lambda i: (0, 0)), main_tile],
        out_shape=[jax.ShapeDtypeStruct((SPECIAL, D_MODEL), F32),
                   jax.ShapeDtypeStruct((rows - SPECIAL, D_MODEL), F32)],
        compiler_params=params, name="outffn_final",
    )(mix, x, wo, g2, wg, wu, wd, gf)


def _s5_tail(y, u, s5d_ref, gluw_ref, glub_ref, s5g_ref):
    y5 = _gelu_tanh(y + s5d_ref[...] * u)
    gate = _dot(y5.astype(BF16), gluw_ref[...]) + glub_ref[...]
    return _rmsnorm(y5 * _sigmoid(gate), s5g_ref[...])


def _pair_block_diag(t):
    left = _iota(t.shape, 1) < 64
    return jnp.concatenate([jnp.where(left, t, 0.0), jnp.where(left, 0.0, t)], axis=0)


def _prompt_mixer_kernel(
        sinks_all, proj_ref, cos_ref, sin_ref, attg_ref,
        bre_ref, bim_ref, cre_ref, cim_ref, sc_ref, s5d_ref, gluw_ref, glub_ref, s5g_ref,
        convw_ref, convb_ref, dtb_ref, alog_ref, dssd_ref, ssdg_ref,
        mix_ref, kout_ref, vout_ref, s5re_ref, s5im_ref, convout_ref, ssdout_ref,
        kprev, vprev, hre, him, carry_re, carry_im, xbuf, hssd, *, layer):
    sinks_ref = sinks_all.at[layer]
    c = pl.program_id(1)

    @pl.when(c == 0)
    def _():
        kprev[...] = jnp.zeros_like(kprev)
        vprev[...] = jnp.zeros_like(vprev)
        carry_re[...] = jnp.zeros_like(carry_re)
        carry_im[...] = jnp.zeros_like(carry_im)
        xbuf[0:8, :] = jnp.zeros((8, SSD_CONV_DIM), F32)
        hssd[...] = jnp.zeros_like(hssd)

    cosv = cos_ref[...]
    sinv = sin_ref[...]
    k_rot = _rope(proj_ref[:, K0:K0 + KV_WIDTH], cosv, sinv)
    vt_new = proj_ref[:, V0:V0 + KV_WIDTH].T
    kout_ref[0] = k_rot.T
    vout_ref[0] = vt_new
    kk = jnp.concatenate([kprev[...], k_rot], axis=0).astype(BF16)
    vvt = jnp.concatenate([vprev[...], vt_new], axis=1).astype(BF16)
    kprev[...] = k_rot
    vprev[...] = vt_new

    qts = [(_rope(proj_ref[:, Q0 + j * 128:Q0 + (j + 1) * 128], cosv, sinv) * (HEAD_DIM ** -0.5)).T
           for j in range(ATT_HEADS // 2)]
    krow = _iota((2 * BLK, 4 * BLK), 0)
    qcol = _iota((2 * BLK, 4 * BLK), 1) & (BLK - 1)
    kpos = (c - 1) * BLK + krow - FRONT
    ok = (krow >= qcol) & (krow <= qcol + BLK) & (kpos >= 0)
    zero_half = jnp.zeros((HEAD_DIM, BLK), F32)

    o_rows = []
    for g in range(2):
        blocks = []
        for r in range(4):
            h = 4 * g + r
            qh = qts[h // 2][(h % 2) * HEAD_DIM:(h % 2 + 1) * HEAD_DIM, :]
            blocks.append(jnp.concatenate([qh, zero_half] if g == 0 else [zero_half, qh], axis=0))
        qg = jnp.concatenate(blocks, axis=1).astype(BF16)
        s = jnp.where(ok, _dot(kk, qg), NEG_INF)
        sink = jnp.concatenate(
            [jnp.full((1, BLK), sinks_ref[4 * g + r], F32) for r in range(4)], axis=1)
        m = jnp.maximum(jnp.max(s, axis=0, keepdims=True), sink)
        p = jnp.exp(s - m)
        den = jnp.sum(p, axis=0, keepdims=True) + jnp.exp(sink - m)
        w = (p * (1.0 / den)).astype(BF16)
        og = _dot(vvt, w)
        for r in range(4):
            o_rows.append(og[g * HEAD_DIM:(g + 1) * HEAD_DIM, r * BLK:(r + 1) * BLK])
    o_att = jnp.concatenate(o_rows, axis=0).T
    mix_ref[:, 0:ATT_WIDTH] = _rmsnorm(o_att, attg_ref[...])

    u = proj_ref[:, U0:U0 + S5_WIDTH]
    ub = u.astype(BF16)
    hre[...] = _dot(ub, bre_ref[...])
    him[...] = _dot(ub, bim_ref[...])
    for lt in range(S5_LANES // 128):
        ls = slice(lt * 128, (lt + 1) * 128)
        p_re = sc_ref[0:8, ls]
        p_im = sc_ref[8:16, ls]
        steps = tuple((d, sc_ref[b0:b0 + 8, ls], sc_ref[b0 + 8:b0 + 16, ls])
                      for d, b0 in ((1, 16), (2, 32), (4, 48)))
        cr = carry_re[:, ls]
        ci = carry_im[:, ls]
        for i in range(BLK // 8):
            rs = slice(8 * i, 8 * i + 8)
            xr = hre[rs, ls]
            xi = him[rs, ls]
            for d, a_r, a_i in steps:
                sr = pltpu.roll(xr, d, 0)
                si = pltpu.roll(xi, d, 0)
                xr, xi = xr + a_r * sr - a_i * si, xi + a_r * si + a_i * sr
            hr = xr + p_re * cr - p_im * ci
            hi = xi + p_re * ci + p_im * cr
            hre[rs, ls] = hr
            him[rs, ls] = hi
            cr = jnp.broadcast_to(hr[7:8], (8, 128))
            ci = jnp.broadcast_to(hi[7:8], (8, 128))
        carry_re[:, ls] = cr
        carry_im[:, ls] = ci
    s5re_ref[0] = carry_re[0:1, :]
    s5im_ref[0] = carry_im[0:1, :]
    y = _dot(hre[...].astype(BF16), cre_ref[...]) - _dot(him[...].astype(BF16), cim_ref[...])
    mix_ref[:, ATT_WIDTH:ATT_WIDTH + S5_WIDTH] = _s5_tail(y, u, s5d_ref, gluw_ref, glub_ref, s5g_ref)

    xbc = proj_ref[:, XBC0:XBC0 + SSD_CONV_DIM]
    xbuf[8:8 + BLK, :] = xbc
    conv = convb_ref[...]
    for j in range(SSD_CONV):
        conv = conv + xbuf[5 + j:5 + j + BLK, :] * convw_ref[j:j + 1, :]
    xc = _silu(conv)
    convout_ref[0] = xbc[BLK - (SSD_CONV - 1):BLK, :]
    xbuf[0:8, :] = xbc[BLK - 8:BLK, :]
    xs = xc[:, 0:SSD_WIDTH]
    bm = xc[:, SSD_WIDTH:SSD_WIDTH + 128]
    cm = xc[:, SSD_WIDTH + 128:SSD_WIDTH + 256]

    lane = _iota((BLK, 128), 1)
    row = _iota((BLK, 128), 0)
    live = (lane < SSD_HEADS) & (row >= jnp.where(c == 0, FRONT, 0))
    dt = jnp.where(live, _softplus(proj_ref[:, DT0:DT0 + 128] + dtb_ref[...]), 0.0)
    dta = dt * (-jnp.exp(alog_ref[...]))
    causal = lane <= row
    tri = jnp.where(causal, 1.0, 0.0).astype(BF16)
    hi3, mid3, lo3 = _split3(dta)
    cs = _dot(tri, hi3) + _dot(tri, mid3) + _dot(tri, lo3)
    cs_t = cs.T
    cs_last = cs[BLK - 1:BLK, :]
    heads64 = (SSD_HEAD_DIM,) * SSD_HEADS
    xd = xs * _lane_bcast(dt, heads64)
    dxd = xd * _lane_bcast(jnp.exp(cs_last - cs), heads64)
    ecs = _lane_bcast(jnp.exp(cs), heads64)

    bmb = bm.astype(BF16)
    left = lane < 64
    cb = (_dot_nt(jnp.where(left, cm, 0.0).astype(BF16), bmb),
          _dot_nt(jnp.where(left, 0.0, cm).astype(BF16), bmb))
    scores = []
    for h in range(SSD_HEADS):
        seg = cs[:, h:h + 1] - cs_t[h:h + 1, :]
        scores.append((cb[h // 2] * jnp.exp(jnp.where(causal, seg, NEG_INF))).astype(BF16))
    y_diag = jnp.concatenate(
        [_dot(jnp.concatenate([scores[2 * j], scores[2 * j + 1]], axis=1),
              _pair_block_diag(xd[:, j * 128:(j + 1) * 128]).astype(BF16)) for j in range(2)], axis=1)

    h_prev = hssd[...]
    y_off = _dot_nt(cm.astype(BF16), h_prev.astype(BF16)) * ecs
    states = _dot(dxd.T.astype(BF16), bmb)
    own = (_iota((SSD_WIDTH, 128), 0) >> 7) == (_iota((SSD_WIDTH, 128), 1) >> 6)
    cd = jnp.exp(cs_last)
    cdm = jnp.concatenate(
        [jnp.broadcast_to(cd[:, h:h + 1], (SSD_HEAD_DIM, 128)) for h in range(SSD_HEADS)], axis=0)
    h_new = cdm * h_prev + jnp.where(own, states, 0.0)
    hssd[...] = h_new
    for h in range(SSD_HEADS):
        g0 = (h // 2) * SSD_STATE
        ssdout_ref[0, h] = h_new[h * SSD_HEAD_DIM:(h + 1) * SSD_HEAD_DIM, g0:g0 + SSD_STATE]

    yssd = y_diag + y_off + dssd_ref[...] * xs
    yc = yssd * _silu(proj_ref[:, Z0:Z0 + SSD_WIDTH])
    mix_ref[:, ATT_WIDTH + S5_WIDTH:] = _rmsnorm(yc, ssdg_ref[...])


def _prompt_mixers(proj, pp, tabs, batch, nc, layer):
    rows = proj.shape[0]

    def blk(b, c):
        return jnp.where(c == 0, b, 4 + b * (nc - 1) + c - 1)

    const = lambda shape: pl.BlockSpec((None,) + shape, lambda b, c: (layer,) + (0,) * len(shape))
    in_specs = [
        pl.BlockSpec(memory_space=pltpu.SMEM),
        pl.BlockSpec((BLK, PROJ_W), lambda b, c: (blk(b, c), 0)),
        pl.BlockSpec((BLK, 128), lambda b, c: (c, 0)),
        pl.BlockSpec((BLK, 128), lambda b, c: (c, 0)),
        const((1, ATT_WIDTH)),
        const((S5_WIDTH, S5_LANES)), const((S5_WIDTH, S5_LANES)),
        const((S5_LANES, S5_WIDTH)), const((S5_LANES, S5_WIDTH)),
        const((64, S5_LANES)), const((1, S5_WIDTH)), const((S5_WIDTH, S5_WIDTH)),
        const((1, S5_WIDTH)), const((1, S5_WIDTH)),
        const((SSD_CONV, SSD_CONV_DIM)), const((1, SSD_CONV_DIM)),
        const((1, 128)), const((1, 128)), const((1, SSD_WIDTH)), const((1, SSD_WIDTH)),
    ]
    per_b = lambda shape: pl.BlockSpec((1,) + shape, lambda b, c: (b,) + (0,) * len(shape))
    out_specs = [
        pl.BlockSpec((BLK, D_MODEL), lambda b, c: (blk(b, c), 0)),
        per_b((BLK, KV_WIDTH)), per_b((BLK, KV_WIDTH)),
        per_b((1, S5_LANES)), per_b((1, S5_LANES)),
        per_b((SSD_CONV - 1, SSD_CONV_DIM)),
        per_b((SSD_HEADS, SSD_HEAD_DIM, SSD_STATE)),
    ]
    out_shape = [
        jax.ShapeDtypeStruct((rows, D_MODEL), F32),
        jax.ShapeDtypeStruct((batch, BLK, KV_WIDTH), F32),
        jax.ShapeDtypeStruct((batch, BLK, KV_WIDTH), F32),
        jax.ShapeDtypeStruct((batch, 1, S5_LANES), F32),
        jax.ShapeDtypeStruct((batch, 1, S5_LANES), F32),
        jax.ShapeDtypeStruct((batch, SSD_CONV - 1, SSD_CONV_DIM), F32),
        jax.ShapeDtypeStruct((batch, SSD_HEADS, SSD_HEAD_DIM, SSD_STATE), F32),
    ]
    scratch = [
        pltpu.VMEM((BLK, KV_WIDTH), F32), pltpu.VMEM((BLK, KV_WIDTH), F32),
        pltpu.VMEM((BLK, S5_LANES), F32), pltpu.VMEM((BLK, S5_LANES), F32),
        pltpu.VMEM((8, S5_LANES), F32), pltpu.VMEM((8, S5_LANES), F32),
        pltpu.VMEM((BLK + 8, SSD_CONV_DIM), F32),
        pltpu.VMEM((SSD_WIDTH, 128), F32),
    ]
    return pl.pallas_call(
        functools.partial(_prompt_mixer_kernel, layer=layer),
        grid=(batch, nc),
        in_specs=in_specs, out_specs=out_specs, out_shape=out_shape, scratch_shapes=scratch,
        compiler_params=pltpu.CompilerParams(dimension_semantics=("arbitrary", "arbitrary"),
                                             vmem_limit_bytes=40 * MIB),
        name="prompt_mixers",
    )(pp["sinks"], proj, tabs["cos_p"], tabs["sin_p"], pp["attn_out_g"],
      pp["bre"], pp["bim"], pp["cre"], pp["cim"], pp["sc"], pp["s5_d"], pp["glu_w"], pp["glu_b"],
      pp["s5_out_g"], pp["conv_w"], pp["conv_b"], pp["dt_bias"], pp["a_log"], pp["ssd_d"],
      pp["ssd_norm_g"])


def _decode_mixer_kernel(*refs, layer, chained):
    n_in = 27 + (6 if chained else 0)
    (sinks_all, _, proj_ref, cos_ref, sin_ref, attg_ref,
     bret_ref, bimt_ref, cret_ref, cimt_ref, abar_ref, s5d_ref, gluw_ref, glub_ref, s5g_ref,
     convw_ref, convb_ref, dtb_ref, alog_ref, dssd_ref, ssdg_ref,
     kt_ref, vt_ref, s5re_in, s5im_in, conv_in, ssd_in) = refs[:27]
    (mix_ref, ktout_ref, vtout_ref, s5re_ref, s5im_ref, convout_ref, ssdout_ref,
     oatt_scr, xs_scr, xdt_scr, bt_scr, ct_scr, dcyt_scr, yt_scr) = refs[n_in:]
    sinks_ref = sinks_all.at[layer]
    i = pl.program_id(0)
    heads64 = (SSD_HEAD_DIM,) * SSD_HEADS

    @pl.when(i == 0)
    def _():
        mix_ref[DEC:, :] = jnp.zeros((mix_ref.shape[0] - DEC, D_MODEL), F32)

        u = proj_ref[:, U0:U0 + S5_WIDTH]
        ut = u.T.astype(BF16)
        a_re = abar_ref[0]
        a_im = abar_ref[1]
        h0r = s5re_in[...]
        h0i = s5im_in[...]
        hr = _dot(bret_ref[...], ut) + a_re * h0r - a_im * h0i
        hi = _dot(bimt_ref[...], ut) + a_re * h0i + a_im * h0r
        s5re_ref[...] = hr
        s5im_ref[...] = hi
        yt = _dot(cret_ref[...], hr.astype(BF16)) - _dot(cimt_ref[...], hi.astype(BF16))
        mix_ref[0:DEC, ATT_WIDTH:ATT_WIDTH + S5_WIDTH] = _s5_tail(
            yt.T, u, s5d_ref, gluw_ref, glub_ref, s5g_ref)

        xbc = proj_ref[:, XBC0:XBC0 + SSD_CONV_DIM]
        conv = convb_ref[...]
        for j in range(SSD_CONV - 1):
            conv = conv + conv_in[j] * convw_ref[j:j + 1, :]
        conv = conv + xbc * convw_ref[SSD_CONV - 1:SSD_CONV, :]
        convout_ref[0] = conv_in[1]
        convout_ref[1] = conv_in[2]
        convout_ref[2] = xbc
        xc = _silu(conv)
        xs = xc[:, 0:SSD_WIDTH]
        dt = _softplus(proj_ref[:, DT0:DT0 + 128] + dtb_ref[...])
        decay = jnp.exp(dt * (-jnp.exp(alog_ref[...])))
        xs_scr[...] = xs
        xdt_scr[...] = (xs * _lane_bcast(dt, heads64)).T
        bt_scr[...] = xc[:, SSD_WIDTH:SSD_WIDTH + 128].T
        ct_scr[...] = xc[:, SSD_WIDTH + 128:SSD_WIDTH + 256].T
        dcyt_scr[...] = decay.T[0:8, :]

    r0 = pl.multiple_of(i * DCH, DCH)
    cosv = cos_ref[...]
    sinv = sin_ref[...]
    k_rot = _rope(proj_ref[pl.ds(r0, DCH), K0:K0 + KV_WIDTH], cosv, sinv)
    v_new = proj_ref[pl.ds(r0, DCH), V0:V0 + KV_WIDTH]
    pad_rows = jnp.zeros((BLK - DCH, KV_WIDTH), F32)
    knew_t = jnp.concatenate([k_rot, pad_rows], axis=0).T
    vnew_t = jnp.concatenate([v_new, pad_rows], axis=0).T
    last = _iota((KV_WIDTH, BLK), 1) == BLK - 1
    for b in range(DCH):
        ktout_ref[b] = jnp.where(last, knew_t[:, b:b + 1], pltpu.roll(kt_ref[b], BLK - 1, 1))
        vtout_ref[b] = jnp.where(last, vnew_t[:, b:b + 1], pltpu.roll(vt_ref[b], BLK - 1, 1))
    left = _iota((DCH, 128), 1) < 64
    qs = []
    for h in range(ATT_HEADS):
        j, e, g = h // 2, h % 2, h // 4
        qt = _rope(proj_ref[pl.ds(r0, DCH), Q0 + j * 128:Q0 + (j + 1) * 128], cosv, sinv)
        qt = qt * (HEAD_DIM ** -0.5)
        if e != g:
            qt = pltpu.roll(qt, 64, 1)
        qs.append(jnp.where(left == (g == 0), qt, 0.0))
    qx = jnp.concatenate(qs, axis=0).astype(BF16)
    kt_cat = jnp.concatenate([kt_ref[b] for b in range(DCH)], axis=1).astype(BF16)
    vt_cat = jnp.concatenate([vt_ref[b] for b in range(DCH)], axis=1).astype(BF16)
    s_old = _dot(qx, kt_cat)
    s_new = _dot_nt(qx, k_rot.astype(BF16))
    rseq = _iota((ATT_HEADS * DCH, DCH * BLK), 0) & (DCH - 1)
    same = rseq == (_iota((ATT_HEADS * DCH, DCH * BLK), 1) >> 7)
    same_new = (_iota((ATT_HEADS * DCH, DCH), 0) & (DCH - 1)) == _iota((ATT_HEADS * DCH, DCH), 1)
    s_old = jnp.where(same, s_old, NEG_INF)
    s_new = jnp.where(same_new, s_new, NEG_INF)
    sink = jnp.concatenate(
        [jnp.full((DCH, 1), sinks_ref[h], F32) for h in range(ATT_HEADS)], axis=0)
    m = jnp.maximum(jnp.maximum(jnp.max(s_old, axis=-1, keepdims=True),
                                jnp.max(s_new, axis=-1, keepdims=True)), sink)
    p_old = jnp.exp(s_old - m)
    p_new = jnp.exp(s_new - m)
    den = (jnp.sum(p_old, axis=-1, keepdims=True) + jnp.sum(p_new, axis=-1, keepdims=True)
           + jnp.exp(sink - m))
    o = (_dot_nt((p_old / den).astype(BF16), vt_cat)
         + _dot((p_new / den).astype(BF16), v_new.astype(BF16)))
    o_tiles = []
    for j in range(ATT_HEADS // 2):
        g = j // 2
        a = o[(2 * j) * DCH:(2 * j + 1) * DCH]
        bb = o[(2 * j + 1) * DCH:(2 * j + 2) * DCH]
        if g == 1:
            a = pltpu.roll(a, 64, 1)
        else:
            bb = pltpu.roll(bb, 64, 1)
        o_tiles.append(jnp.where(left, a, bb))
    oatt_scr[pl.ds(r0, DCH), :] = jnp.concatenate(o_tiles, axis=1)

    rows_per_step = SSD_WIDTH // (DEC // DCH)
    head = i // (SSD_HEAD_DIM // rows_per_step)
    g0 = pl.multiple_of((head // 2) * SSD_STATE, SSD_STATE)
    dcy = dcyt_scr[pl.ds(head, 1), :]
    btg = bt_scr[pl.ds(g0, SSD_STATE), :]
    ctg = ct_scr[pl.ds(g0, SSD_STATE), :]
    for rr in range(rows_per_step):
        row = i * rows_per_step + rr
        h_new = dcy * ssd_in[rr] + xdt_scr[pl.ds(row, 1), :] * btg
        ssdout_ref[rr] = h_new
        yt_scr[pl.ds(row, 1), :] = jnp.sum(ctg * h_new, axis=0, keepdims=True)

    @pl.when(i == DEC // DCH - 1)
    def _():
        mix_ref[0:DEC, 0:ATT_WIDTH] = _rmsnorm(oatt_scr[...], attg_ref[...])
        xs = xs_scr[...]
        yssd = yt_scr[...].T + dssd_ref[...] * xs
        yc = yssd * _silu(proj_ref[:, Z0:Z0 + SSD_WIDTH])
        mix_ref[0:DEC, ATT_WIDTH + S5_WIDTH:] = _rmsnorm(yc, ssdg_ref[...])


def _decode_mixers(mix, proj, pp, tabs, states, prev, layer):
    rows = proj.shape[0]
    depth = states[0].shape[0]
    nsteps = DEC // DCH
    chained = prev is not None
    const = lambda shape: pl.BlockSpec((None,) + shape, lambda i: (layer,) + (0,) * len(shape))
    plain = lambda shape: pl.BlockSpec(shape, lambda i: (0,) * len(shape))
    rows_per_step = SSD_WIDTH // nsteps
    state_specs = [
        pl.BlockSpec((None, DCH, KV_WIDTH, BLK), lambda i: (layer, i, 0, 0)),
        pl.BlockSpec((None, DCH, KV_WIDTH, BLK), lambda i: (layer, i, 0, 0)),
        const((S5_LANES, DEC)), const((S5_LANES, DEC)),
        const((SSD_CONV - 1, DEC, SSD_CONV_DIM)),
        pl.BlockSpec((None, rows_per_step, SSD_STATE, DEC), lambda i: (layer, i, 0, 0)),
    ]
    in_specs = [
        pl.BlockSpec(memory_space=pltpu.SMEM),
        pl.BlockSpec(memory_space=pl.ANY),
        pl.BlockSpec((DEC, PROJ_W), lambda i: (DEC_ROW0 // DEC, 0)),
        plain((1, 128)), plain((1, 128)),
        const((1, ATT_WIDTH)),
        const((S5_LANES, S5_WIDTH)), const((S5_LANES, S5_WIDTH)),
        const((S5_WIDTH, S5_LANES)), const((S5_WIDTH, S5_LANES)),
        const((2, S5_LANES, DEC)), const((1, S5_WIDTH)), const((S5_WIDTH, S5_WIDTH)),
        const((1, S5_WIDTH)), const((1, S5_WIDTH)),
        const((SSD_CONV, SSD_CONV_DIM)), const((1, SSD_CONV_DIM)),
        const((1, 128)), const((1, 128)), const((1, SSD_WIDTH)), const((1, SSD_WIDTH)),
    ] + state_specs + ([pl.BlockSpec(memory_space=pl.ANY)] * 6 if chained else [])
    out_specs = [pl.BlockSpec((2 * DEC, D_MODEL), lambda i: (DEC_ROW0 // (2 * DEC), 0))] + state_specs
    out_shape = [jax.ShapeDtypeStruct((rows, D_MODEL), F32)] + [
        jax.ShapeDtypeStruct(s.shape, F32) for s in states]
    scratch = [
        pltpu.VMEM((DEC, ATT_WIDTH), F32), pltpu.VMEM((DEC, SSD_WIDTH), F32),
        pltpu.VMEM((SSD_WIDTH, DEC), F32), pltpu.VMEM((128, DEC), F32), pltpu.VMEM((128, DEC), F32),
        pltpu.VMEM((8, DEC), F32), pltpu.VMEM((SSD_WIDTH, DEC), F32),
    ]
    aliases = {1: 0}
    if chained:
        aliases.update({27 + k: 1 + k for k in range(6)})
    assert depth > layer
    return pl.pallas_call(
        functools.partial(_decode_mixer_kernel, layer=layer, chained=chained),
        grid=(nsteps,),
        in_specs=in_specs, out_specs=out_specs, out_shape=out_shape,
        scratch_shapes=scratch,
        input_output_aliases=aliases,
        compiler_params=pltpu.CompilerParams(dimension_semantics=("arbitrary",),
                                             vmem_limit_bytes=40 * MIB),
        name="decode_mixers",
    )(pp["sinks"], mix, proj, tabs["cos_d"], tabs["sin_d"], pp["attn_out_g"],
      pp["bre_t"], pp["bim_t"], pp["cre_t"], pp["cim_t"], pp["abar_t"], pp["s5_d"], pp["glu_w"],
      pp["glu_b"], pp["s5_out_g"], pp["conv_w"], pp["conv_b"], pp["dt_bias"], pp["a_log"],
      pp["ssd_d"], pp["ssd_norm_g"], *states, *(prev if chained else ()))


def _rope_tables(nc):
    half = HEAD_DIM // 2
    inv = ROPE_THETA ** (-jnp.arange(half, dtype=F32) / half)

    def tab(pos):
        ang = pos.astype(F32)[:, None] * inv[None, :]
        cos = jnp.cos(ang)
        sin = jnp.sin(ang)
        return jnp.tile(cos, (1, 4)), jnp.concatenate([-sin, sin, -sin, sin], axis=1)

    cos_p, sin_p = tab(jnp.arange(nc * BLK, dtype=jnp.int32) - FRONT)
    cos_d, sin_d = tab(jnp.full((1,), PAST_LEN, dtype=jnp.int32))
    return {"cos_p": cos_p, "sin_p": sin_p, "cos_d": cos_d, "sin_d": sin_d}


def kernel(x_prompt, x_sample, cache_k, cache_v, state_s5_re, state_s5_im, state_ssd_conv, state_ssd,
           meta_tokens, ln1_g, w_in, attn_sinks, attn_out_g, s5_a_re, s5_a_im, s5_log_dt,
           s5_b_re, s5_b_im, s5_c_re, s5_c_im, s5_d, s5_glu_w, s5_glu_b, s5_out_g,
           ssd_conv_w, ssd_conv_b, ssd_dt_bias, ssd_a_log, ssd_d, ssd_norm_g, w_out,
           ln2_g, w_gate, w_up, w_down, lnf_g):
    batch, seq, _ = x_prompt.shape
    depth = w_in.shape[0]
    assert batch == 2 and x_sample.shape[0] == DEC and x_sample.shape[1] == 1
    assert seq % TM == 0 and cache_k.shape[2] == BLK
    nc = seq // BLK + 1

    zeros_front = jnp.zeros((FRONT, D_MODEL), F32)
    x = jnp.concatenate([zeros_front, meta_tokens, zeros_front, meta_tokens,
                         x_sample.reshape(DEC, D_MODEL), jnp.zeros((BLK, D_MODEL), F32),
                         x_prompt.reshape(batch * seq, D_MODEL)], axis=0)

    tabs = _rope_tables(nc)
    sc, bb_re, bb_im = _s5_prepare(s5_a_re, s5_a_im, s5_log_dt, s5_b_re, s5_b_im)
    bre, bim = _s5_block_diag_in(bb_re), _s5_block_diag_in(bb_im)
    cre, cim = _s5_block_diag_out(s5_c_re), _s5_block_diag_out(s5_c_im)
    w_in_p = jnp.pad(w_in, ((0, 0), (0, 0), (0, PROJ_W - N_IN))).astype(BF16)
    w_out_b, w_gate_b, w_up_b, w_down_b = (t.astype(BF16) for t in (w_out, w_gate, w_up, w_down))
    pad_heads = lambda t: jnp.pad(t, ((0, 0), (0, 128 - SSD_HEADS)))[:, None, :]
    row = lambda t: t[:, None, :]
    abar_t = jnp.broadcast_to(jnp.stack([sc[:, 0], sc[:, 8]], axis=1)[..., None],
                              (depth, 2, S5_LANES, DEC))
    pp = {
        "sinks": attn_sinks, "attn_out_g": row(attn_out_g),
        "bre": bre, "bim": bim, "cre": cre, "cim": cim, "sc": sc,
        "bre_t": jnp.swapaxes(bre, 1, 2), "bim_t": jnp.swapaxes(bim, 1, 2),
        "cre_t": jnp.swapaxes(cre, 1, 2), "cim_t": jnp.swapaxes(cim, 1, 2), "abar_t": abar_t,
        "s5_d": row(s5_d), "glu_w": s5_glu_w.astype(BF16), "glu_b": row(s5_glu_b),
        "s5_out_g": row(s5_out_g),
        "conv_w": ssd_conv_w, "conv_b": row(ssd_conv_b),
        "dt_bias": pad_heads(ssd_dt_bias), "a_log": pad_heads(ssd_a_log),
        "ssd_d": row(jnp.repeat(ssd_d, SSD_HEAD_DIM, axis=-1)), "ssd_norm_g": row(ssd_norm_g),
    }
    ln1_r, ln2_r = row(ln1_g), row(ln2_g)

    states = (
        jnp.transpose(cache_k, (0, 1, 3, 4, 2)).reshape(depth, DEC, KV_WIDTH, BLK),
        jnp.transpose(cache_v, (0, 1, 3, 4, 2)).reshape(depth, DEC, KV_WIDTH, BLK),
        jnp.transpose(state_s5_re, (0, 2, 3, 1)).reshape(depth, S5_LANES, DEC),
        jnp.transpose(state_s5_im, (0, 2, 3, 1)).reshape(depth, S5_LANES, DEC),
        jnp.transpose(state_ssd_conv, (0, 2, 1, 3)),
        jnp.transpose(state_ssd, (0, 2, 3, 4, 1)).reshape(depth, SSD_WIDTH, SSD_STATE, DEC),
    )

    outs_p = [[] for _ in range(6)]
    outs_s = None
    y_special = y_main = None
    for l in range(depth):
        proj = _inproj(x, ln1_r, w_in_p, l)
        res_p = _prompt_mixers(proj, pp, tabs, batch, nc, l)
        res_s = _decode_mixers(res_p[0], proj, pp, tabs, states, outs_s, l)
        outs_s = res_s[1:]
        for i in range(6):
            outs_p[i].append(res_p[i + 1])
        ffn_args = (res_s[0], x, w_out_b, ln2_r, w_gate_b, w_up_b, w_down_b, l)
        if l + 1 < depth:
            x = _outffn(*ffn_args)
        else:
            y_special, y_main = _outffn(*ffn_args, gf=lnf_g[None, :])

    y_prompt = y_main.reshape(batch, seq, D_MODEL)
    y_sample = y_special[DEC_ROW0:DEC_ROW0 + DEC].reshape(DEC, 1, D_MODEL)
    kv_p = lambda ts: jnp.transpose(
        jnp.stack(ts).reshape(depth, batch, 2, HEAD_DIM, BLK), (0, 1, 4, 2, 3))
    s5_p = lambda ts: jnp.stack(ts).reshape(depth, batch, S5_GROUPS, S5_STATE)
    kt_s, vt_s, s5re_s, s5im_s, conv_s, ssd_s = outs_s
    kv_s = lambda t: jnp.transpose(t.reshape(depth, DEC, 2, HEAD_DIM, BLK), (0, 1, 4, 2, 3))
    s5_s = lambda t: jnp.transpose(t.reshape(depth, S5_GROUPS, S5_STATE, DEC), (0, 3, 1, 2))
    return (y_prompt, y_sample,
            kv_p(outs_p[0]), kv_p(outs_p[1]), s5_p(outs_p[2]), s5_p(outs_p[3]),
            jnp.stack(outs_p[4]), jnp.stack(outs_p[5]),
            kv_s(kt_s), kv_s(vt_s), s5_s(s5re_s), s5_s(s5im_s),
            jnp.transpose(conv_s, (0, 2, 1, 3)),
            jnp.transpose(ssd_s.reshape(depth, SSD_HEADS, SSD_HEAD_DIM, SSD_STATE, DEC),
                          (0, 4, 1, 2, 3)))
```

```python
import functools

import jax
import jax.numpy as jnp
from jax import lax
from jax.experimental import pallas as pl
from jax.experimental.pallas import tpu as pltpu

F32 = jnp.float32
BF16 = jnp.bfloat16

D_MODEL = 1024
N_META = 16
HEAD_DIM = 64
ATT_WIDTH = 512
ATT_HEADS = 8
KV_WIDTH = 128
S5_WIDTH = 256
S5_GROUPS = 16
S5_GROUP_CH = 16
S5_STATE = 64
S5_LANES = S5_GROUPS * S5_STATE
SSD_WIDTH = 256
SSD_HEADS = 4
SSD_HEAD_DIM = 64
SSD_STATE = 64
SSD_CONV = 4
SSD_CONV_DIM = 512
FFN_HIDDEN = 2816
NORM_EPS = 1e-6
ROPE_THETA = 10000.0
PAST_LEN = 8192
N_IN = 1796

BLK = 128
FRONT = BLK - N_META
TM = 512
SPECIAL = 4 * BLK
DEC = 128
DEC_ROW0 = 2 * BLK
DCH = 16
PROJ_W = 1920
Q0, K0, V0, U0, Z0, XBC0, DT0 = 0, 512, 640, 768, 1024, 1280, 1792
FFN_CHUNK = 256
SEG = BLK // 8
SC_ABAR_RE, SC_ABAR_IM, SC_SEG1, SC_SEG2, SC_SEG4, SC_CARRY, SC_ROWS = 0, 8, 16, 32, 48, 64, 80
NEG_INF = float("-inf")
MIB = 1024 * 1024


def _dot(a, b):
    return jnp.dot(a, b, preferred_element_type=F32)


def _dot_nt(a, b):
    return lax.dot_general(a, b, (((1,), (1,)), ((), ())), preferred_element_type=F32)


def _sigmoid(x):
    return 1.0 / (1.0 + jnp.exp(-x))


def _silu(x):
    return x * _sigmoid(x)


def _softplus(x):
    return jnp.maximum(x, 0.0) + jnp.log1p(jnp.exp(-jnp.abs(x)))


def _gelu_tanh(x):
    cdf = 0.5 * (1.0 + jnp.tanh(0.7978845608028654 * (x + 0.044715 * (x * x * x))))
    return x * cdf


def _rmsnorm(x, g):
    return x * lax.rsqrt(jnp.mean(x * x, axis=-1, keepdims=True) + NORM_EPS) * g


def _iota(shape, dim):
    return lax.broadcasted_iota(jnp.int32, shape, dim)


def _rope(x, cosv, sinv):
    first = (_iota(x.shape, 1) & 63) < 32
    partner = jnp.where(first, pltpu.roll(x, 96, 1), pltpu.roll(x, 32, 1))
    return x * cosv + partner * sinv


def _lane_bcast(x, widths):
    rows = x.shape[0]
    return jnp.concatenate(
        [jnp.broadcast_to(x[:, h:h + 1], (rows, w)) for h, w in enumerate(widths)], axis=1)


def _split3(x):
    hi = x.astype(BF16)
    r1 = x - hi.astype(F32)
    mid = r1.astype(BF16)
    lo = (r1 - mid.astype(F32)).astype(BF16)
    return hi, mid, lo


def _s5_prep_kernel(are_ref, aim_ref, ldt_ref, btre_ref, btim_ref, sc_ref, bbre_ref, bbim_ref):
    ar = are_ref[0]
    ai = aim_ref[0]
    dt = jnp.exp(ldt_ref[0])

    def power(k):
        kf = k.astype(F32)
        mag = jnp.exp((kf * dt) * ar)
        ang = (kf * dt) * ai
        return mag * jnp.cos(ang), mag * jnp.sin(ang)

    pre, pim = power(jnp.full((8, S5_LANES), 1, jnp.int32))
    sc_ref[0, SC_ABAR_RE:SC_ABAR_RE + 8, :] = pre
    sc_ref[0, SC_ABAR_IM:SC_ABAR_IM + 8, :] = pim
    row = _iota((8, S5_LANES), 0)
    for d, base in ((1, SC_SEG1), (2, SC_SEG2), (4, SC_SEG4)):
        sre, sim = power(jnp.full((8, S5_LANES), SEG * d, jnp.int32))
        sc_ref[0, base:base + 8, :] = jnp.where(row >= d, sre, 0.0)
        sc_ref[0, base + 8:base + 16, :] = jnp.where(row >= d, sim, 0.0)
    qre, qim = power(SEG * (row + 1))
    sc_ref[0, SC_CARRY:SC_CARRY + 8, :] = qre
    sc_ref[0, SC_CARRY + 8:SC_CARRY + 16, :] = qim
    abar_re = pre[0:1]
    abar_im = pim[0:1]
    den = ar * ar + ai * ai
    xr = abar_re - 1.0
    f_re = (xr * ar + abar_im * ai) / den
    f_im = (abar_im * ar - xr * ai) / den
    br = btre_ref[0]
    bi = btim_ref[0]
    bbre_ref[0] = f_re * br - f_im * bi
    bbim_ref[0] = f_re * bi + f_im * br


def _s5_prepare(a_re, a_im, log_dt, b_re, b_im):
    depth = a_re.shape[0]
    flat = lambda t: t.reshape(depth, 1, S5_LANES)
    ldt = jnp.repeat(log_dt, S5_STATE, axis=-1).reshape(depth, 1, S5_LANES)
    bt = lambda t: jnp.transpose(t, (0, 3, 1, 2)).reshape(depth, S5_GROUP_CH, S5_LANES)
    vec = pl.BlockSpec((1, 1, S5_LANES), lambda l: (l, 0, 0))
    mat = pl.BlockSpec((1, S5_GROUP_CH, S5_LANES), lambda l: (l, 0, 0))
    return pl.pallas_call(
        _s5_prep_kernel,
        grid=(depth,),
        in_specs=[vec, vec, vec, mat, mat],
        out_specs=[pl.BlockSpec((1, SC_ROWS, S5_LANES), lambda l: (l, 0, 0)), mat, mat],
        out_shape=[jax.ShapeDtypeStruct((depth, SC_ROWS, S5_LANES), F32),
                   jax.ShapeDtypeStruct((depth, S5_GROUP_CH, S5_LANES), F32),
                   jax.ShapeDtypeStruct((depth, S5_GROUP_CH, S5_LANES), F32)],
        name="s5_prep",
    )(flat(a_re), flat(a_im), ldt, bt(b_re), bt(b_im))


def _s5_block_diag_in(bb):
    depth = bb.shape[0]
    same = (jnp.arange(S5_GROUPS)[:, None, None] ==
            (jnp.arange(S5_LANES) // S5_STATE)[None, None, :])
    out = jnp.where(same[None], bb[:, None, :, :], 0.0)
    return out.reshape(depth, S5_WIDTH, S5_LANES).astype(BF16)


def _s5_block_diag_out(c):
    depth = c.shape[0]
    ct = jnp.transpose(c, (0, 1, 3, 2))
    same = jnp.eye(S5_GROUPS, dtype=bool)[None, :, None, :, None]
    out = jnp.where(same, ct[:, :, :, None, :], 0.0)
    return out.reshape(depth, S5_LANES, S5_WIDTH).astype(BF16)


def _inproj_kernel(x_ref, g_ref, w_ref, o_ref):
    x = x_ref[...]
    h = _rmsnorm(x, g_ref[...])
    row = _iota((TM, 1), 0)
    first_tile = pl.program_id(0) == 0
    front = jnp.where(first_tile, FRONT, 0)
    tail = jnp.where(first_tile, 3 * BLK, TM)
    pad = (row < front) | ((row >= BLK) & (row < BLK + front)) | (row >= tail)
    h = jnp.where(pad, 0.0, h)
    o_ref[...] = _dot(h.astype(BF16), w_ref[...])


def _layer_resident(layer, shape):
    return pl.BlockSpec((None,) + shape, lambda i: (layer,) + (0,) * len(shape),
                        pipeline_mode=pl.Buffered(1))


def _inproj(x, g, w, layer):
    rows = x.shape[0]
    resident = functools.partial(_layer_resident, layer)
    return pl.pallas_call(
        _inproj_kernel,
        grid=(rows // TM,),
        in_specs=[pl.BlockSpec((TM, D_MODEL), lambda i: (i, 0)),
                  resident((1, D_MODEL)),
                  resident((D_MODEL, PROJ_W))],
        out_specs=pl.BlockSpec((TM, PROJ_W), lambda i: (i, 0)),
        out_shape=jax.ShapeDtypeStruct((rows, PROJ_W), F32),
        compiler_params=pltpu.CompilerParams(dimension_semantics=("arbitrary",),
                                             vmem_limit_bytes=32 * MIB),
        name="inproj",
    )(x, g, w)


def _outffn_body(mixs_ref, mixm_ref, x_ref, wo_ref, g2_ref, wg_ref, wu_ref, wd_ref):
    mix = jnp.where(pl.program_id(0) == 0, mixs_ref[...], mixm_ref[...])
    x1 = x_ref[...] + _dot(mix.astype(BF16), wo_ref[...])
    h = _rmsnorm(x1, g2_ref[...]).astype(BF16)
    acc = jnp.zeros((TM, D_MODEL), F32)
    for j in range(FFN_HIDDEN // FFN_CHUNK):
        sl = slice(j * FFN_CHUNK, (j + 1) * FFN_CHUNK)
        gate = _dot(h, wg_ref[:, sl])
        up = _dot(h, wu_ref[:, sl])
        acc = acc + _dot((_silu(gate) * up).astype(BF16), wd_ref[sl, :])
    return x1 + acc


def _outffn_kernel(*refs):
    refs[-1][...] = _outffn_body(*refs[:-1])


def _outffn_final_kernel(*refs):
    gf_ref, ysp_ref, ymain_ref = refs[-3:]
    y = _rmsnorm(_outffn_body(*refs[:-3]), gf_ref[...])
    ymain_ref[...] = y

    @pl.when(pl.program_id(0) == 0)
    def _():
        ysp_ref[...] = y


def _outffn(mix_special, mix_main, x, wo, g2, wg, wu, wd, layer, gf=None):
    rows = x.shape[0]
    resident = functools.partial(_layer_resident, layer)
    tile = pl.BlockSpec((TM, D_MODEL), lambda i: (i, 0))
    main_tile = pl.BlockSpec((TM, D_MODEL), lambda i: (jnp.maximum(i - 1, 0), 0))
    mix_main = mix_main.reshape(rows - SPECIAL, D_MODEL)
    in_specs = [pl.BlockSpec((TM, D_MODEL), lambda i: (0, 0)), main_tile, tile,
                resident((D_MODEL, D_MODEL)), resident((1, D_MODEL)),
                resident((D_MODEL, FFN_HIDDEN)), resident((D_MODEL, FFN_HIDDEN)),
                resident((FFN_HIDDEN, D_MODEL))]
    params = pltpu.CompilerParams(dimension_semantics=("arbitrary",), vmem_limit_bytes=48 * MIB)
    if gf is None:
        return pl.pallas_call(
            _outffn_kernel, grid=(rows // TM,), in_specs=in_specs, out_specs=tile,
            out_shape=jax.ShapeDtypeStruct((rows, D_MODEL), F32),
            compiler_params=params, name="outffn",
        )(mix_special, mix_main, x, wo, g2, wg, wu, wd)
    return pl.pallas_call(
        _outffn_final_kernel, grid=(rows // TM,),
        in_specs=in_specs + [pl.BlockSpec((1, D_MODEL), lambda i: (0, 0))],
        out_specs=[pl.BlockSpec((TM, D_MODEL), lambda i: (0, 0)), main_tile],
        out_shape=[jax.ShapeDtypeStruct((SPECIAL, D_MODEL), F32),
                   jax.ShapeDtypeStruct((rows - SPECIAL, D_MODEL), F32)],
        compiler_params=params, name="outffn_final",
    )(mix_special, mix_main, x, wo, g2, wg, wu, wd, gf)


def _s5_tail(y, u, s5d_ref, gluw_ref, glub_ref, s5g_ref):
    y5 = _gelu_tanh(y + s5d_ref[...] * u)
    gate = _dot(y5.astype(BF16), gluw_ref[...]) + glub_ref[...]
    return _rmsnorm(y5 * _sigmoid(gate), s5g_ref[...])


def _pair_block_diag(t):
    left = _iota(t.shape, 1) < 64
    return jnp.concatenate([jnp.where(left, t, 0.0), jnp.where(left, 0.0, t)], axis=0)


N_MIXER_PARAMS = 20


def _prompt_mixer_kernel(*refs, layer, batch):
    sinks_all = refs[0]
    proj_refs = refs[1:1 + batch]
    shared = refs[1 + batch:1 + batch + N_MIXER_PARAMS]
    n_in = 1 + batch + N_MIXER_PARAMS
    mixs_ref, mixm_ref = refs[n_in:n_in + 2]
    outs = refs[n_in + 2:n_in + 8]
    scratch = refs[n_in + 8:]
    c = pl.program_id(0)

    @pl.when(c == 0)
    def _():
        for s in scratch:
            s[...] = jnp.zeros_like(s)

    live = [mixer(c, sinks_all.at[layer], proj_refs[b], *shared, mixm_ref.at[b],
                  *(o.at[b] for o in outs), *(s.at[b] for s in scratch))
            for mixer in (_prompt_attention, _prompt_s5, _prompt_ssd) for b in range(batch)]
    while live:
        live = [gen for gen in live if next(gen, "done") != "done"]

    @pl.when(c == 0)
    def _():
        mixs_ref[...] = mixm_ref[...]


def _prompt_attention(
        c, sinks_ref, proj_ref, cos_ref, sin_ref, attg_ref,
        bre_ref, bim_ref, cre_ref, cim_ref, sc_ref, s5d_ref, gluw_ref, glub_ref, s5g_ref,
        convw_ref, convb_ref, dtb_ref, alog_ref, dssd_ref, ssdg_ref, perm_ref, permt_ref,
        mix_ref, kout_ref, vout_ref, s5re_ref, s5im_ref, convout_ref, ssdout_ref,
        kprev, vprev, hre, him, carry_re, carry_im, xbuf, hssd):
    cosv = cos_ref[...]
    sinv = sin_ref[...]
    k_rot = _rope(proj_ref[:, K0:K0 + KV_WIDTH], cosv, sinv)
    vt_new = proj_ref[:, V0:V0 + KV_WIDTH].T
    kout_ref[...] = k_rot.T
    vout_ref[...] = vt_new
    kk = jnp.concatenate([kprev[...], k_rot], axis=0).astype(BF16)
    vvt = jnp.concatenate([vprev[...], vt_new], axis=1).astype(BF16)
    kprev[...] = k_rot
    vprev[...] = vt_new

    qts = [(_rope(proj_ref[:, Q0 + j * 128:Q0 + (j + 1) * 128], cosv, sinv) * (HEAD_DIM ** -0.5)).T
           for j in range(ATT_HEADS // 2)]
    krow = _iota((2 * BLK, BLK), 0)
    qcol = _iota((2 * BLK, BLK), 1)
    kpos = (c - 1) * BLK + krow - FRONT
    ok = (krow >= qcol) & (krow <= qcol + BLK) & (kpos >= 0)
    bias1 = jnp.where(ok, 0.0, NEG_INF)
    bias = jnp.concatenate([bias1] * 4, axis=1)
    zero_half = jnp.zeros((HEAD_DIM, BLK), F32)
    yield

    o_rows = []
    for g in range(2):
        blocks = []
        for r in range(4):
            h = 4 * g + r
            qh = qts[h // 2][(h % 2) * HEAD_DIM:(h % 2 + 1) * HEAD_DIM, :]
            blocks.append(jnp.concatenate([qh, zero_half] if g == 0 else [zero_half, qh], axis=0))
        qg = jnp.concatenate(blocks, axis=1).astype(BF16)
        s = _dot(kk, qg) + bias
        sink = jnp.concatenate(
            [jnp.full((1, BLK), sinks_ref[4 * g + r], F32) for r in range(4)], axis=1)
        yield
        m = jnp.maximum(jnp.max(s, axis=0, keepdims=True), sink)
        p = jnp.exp(s - m)
        yield
        inv_den = 1.0 / (jnp.sum(p, axis=0, keepdims=True) + jnp.exp(sink - m))
        og = _dot(vvt, p.astype(BF16))
        for r in range(4):
            cols = slice(r * BLK, (r + 1) * BLK)
            o_rows.append(og[g * HEAD_DIM:(g + 1) * HEAD_DIM, cols] * inv_den[:, cols])
        yield
    o_att = jnp.concatenate(o_rows, axis=0).T
    mix_ref[:, 0:ATT_WIDTH] = _rmsnorm(o_att, attg_ref[...])


def _prompt_s5(
        c, sinks_ref, proj_ref, cos_ref, sin_ref, attg_ref,
        bre_ref, bim_ref, cre_ref, cim_ref, sc_ref, s5d_ref, gluw_ref, glub_ref, s5g_ref,
        convw_ref, convb_ref, dtb_ref, alog_ref, dssd_ref, ssdg_ref, perm_ref, permt_ref,
        mix_ref, kout_ref, vout_ref, s5re_ref, s5im_ref, convout_ref, ssdout_ref,
        kprev, vprev, hre, him, carry_re, carry_im, xbuf, hssd):
    u = proj_ref[:, U0:U0 + S5_WIDTH]
    ub = _dot(perm_ref[...], u.astype(BF16)).astype(BF16)
    hre[...] = _dot(ub, bre_ref[...])
    him[...] = _dot(ub, bim_ref[...])
    yield
    first_seg = _iota((8, 128), 0) == 0
    for lt in range(S5_LANES // 128):
        ls = slice(lt * 128, (lt + 1) * 128)
        a_r = sc_ref[SC_ABAR_RE:SC_ABAR_RE + 8, ls]
        a_i = sc_ref[SC_ABAR_IM:SC_ABAR_IM + 8, ls]
        er = hre[0:8, ls]
        ei = him[0:8, ls]
        for t in range(1, SEG):
            rs = slice(8 * t, 8 * t + 8)
            er, ei = hre[rs, ls] + a_r * er - a_i * ei, him[rs, ls] + a_r * ei + a_i * er
        yield
        for d, b0 in ((1, SC_SEG1), (2, SC_SEG2), (4, SC_SEG4)):
            s_r = sc_ref[b0:b0 + 8, ls]
            s_i = sc_ref[b0 + 8:b0 + 16, ls]
            pr = pltpu.roll(er, d, 0)
            pi = pltpu.roll(ei, d, 0)
            er, ei = er + s_r * pr - s_i * pi, ei + s_r * pi + s_i * pr
        cr = carry_re[:, ls]
        ci = carry_im[:, ls]
        q_r = sc_ref[SC_CARRY:SC_CARRY + 8, ls]
        q_i = sc_ref[SC_CARRY + 8:SC_CARRY + 16, ls]
        tr = er + q_r * cr - q_i * ci
        ti = ei + q_r * ci + q_i * cr
        hr = jnp.where(first_seg, cr, pltpu.roll(tr, 1, 0))
        hi = jnp.where(first_seg, ci, pltpu.roll(ti, 1, 0))
        carry_re[:, ls] = jnp.broadcast_to(tr[7:8], (8, 128))
        carry_im[:, ls] = jnp.broadcast_to(ti[7:8], (8, 128))
        for t in range(SEG):
            rs = slice(8 * t, 8 * t + 8)
            hr, hi = hre[rs, ls] + a_r * hr - a_i * hi, him[rs, ls] + a_r * hi + a_i * hr
            hre[rs, ls] = hr
            him[rs, ls] = hi
        yield
    s5re_ref[...] = carry_re[0:1, :]
    s5im_ref[...] = carry_im[0:1, :]
    y_perm = _dot(hre[...].astype(BF16), cre_ref[...]) - _dot(him[...].astype(BF16), cim_ref[...])
    yield
    y3 = _split3(y_perm)
    y = _dot(permt_ref[...], y3[0]) + _dot(permt_ref[...], y3[1]) + _dot(permt_ref[...], y3[2])
    mix_ref[:, ATT_WIDTH:ATT_WIDTH + S5_WIDTH] = _s5_tail(y, u, s5d_ref, gluw_ref, glub_ref, s5g_ref)


def _prompt_ssd(
        c, sinks_ref, proj_ref, cos_ref, sin_ref, attg_ref,
        bre_ref, bim_ref, cre_ref, cim_ref, sc_ref, s5d_ref, gluw_ref, glub_ref, s5g_ref,
        convw_ref, convb_ref, dtb_ref, alog_ref, dssd_ref, ssdg_ref, perm_ref, permt_ref,
        mix_ref, kout_ref, vout_ref, s5re_ref, s5im_ref, convout_ref, ssdout_ref,
        kprev, vprev, hre, him, carry_re, carry_im, xbuf, hssd):
    xbc = proj_ref[:, XBC0:XBC0 + SSD_CONV_DIM]
    xbuf[8:8 + BLK, :] = xbc
    conv = convb_ref[...]
    for j in range(SSD_CONV):
        conv = conv + xbuf[5 + j:5 + j + BLK, :] * convw_ref[j:j + 1, :]
    yield
    xc = _silu(conv)
    convout_ref[...] = xbc[BLK - (SSD_CONV - 1):BLK, :]
    xbuf[0:8, :] = xbc[BLK - 8:BLK, :]
    xs = xc[:, 0:SSD_WIDTH]
    bm = xc[:, SSD_WIDTH:SSD_WIDTH + 128]
    cm = xc[:, SSD_WIDTH + 128:SSD_WIDTH + 256]
    yield

    lane = _iota((BLK, 128), 1)
    row = _iota((BLK, 128), 0)
    live = (lane < SSD_HEADS) & (row >= jnp.where(c == 0, FRONT, 0))
    dt = jnp.where(live, _softplus(proj_ref[:, DT0:DT0 + 128] + dtb_ref[...]), 0.0)
    dta = dt * (-jnp.exp(alog_ref[...]))
    causal = lane <= row
    tri = jnp.where(causal, 1.0, 0.0).astype(BF16)
    hi3, mid3, lo3 = _split3(dta)
    cs = _dot(tri, hi3) + _dot(tri, mid3) + _dot(tri, lo3)
    yield
    cs_t = cs.T
    cs_last = cs[BLK - 1:BLK, :]
    heads64 = (SSD_HEAD_DIM,) * SSD_HEADS
    xd = xs * _lane_bcast(dt, heads64)
    dxd = xd * _lane_bcast(jnp.exp(cs_last - cs), heads64)
    ecs = _lane_bcast(jnp.exp(cs), heads64)
    yield

    bmb = bm.astype(BF16)
    left = lane < 64
    cb = (_dot_nt(jnp.where(left, cm, 0.0).astype(BF16), bmb),
          _dot_nt(jnp.where(left, 0.0, cm).astype(BF16), bmb))
    scores = []
    for h in range(SSD_HEADS):
        seg = cs[:, h:h + 1] - cs_t[h:h + 1, :]
        scores.append((cb[h // 2] * jnp.exp(jnp.where(causal, seg, NEG_INF))).astype(BF16))
        yield
    y_diag = jnp.concatenate(
        [_dot(jnp.concatenate([scores[2 * j], scores[2 * j + 1]], axis=1),
              _pair_block_diag(xd[:, j * 128:(j + 1) * 128]).astype(BF16)) for j in range(2)], axis=1)
    yield

    h_prev = hssd[...]
    y_off = _dot_nt(cm.astype(BF16), h_prev.astype(BF16)) * ecs
    yield
    states = _dot(dxd.T.astype(BF16), bmb)
    own = (_iota((SSD_WIDTH, 128), 0) >> 7) == (_iota((SSD_WIDTH, 128), 1) >> 6)
    cd = jnp.exp(cs_last)
    cdm = jnp.concatenate(
        [jnp.broadcast_to(cd[:, h:h + 1], (SSD_HEAD_DIM, 128)) for h in range(SSD_HEADS)], axis=0)
    h_new = cdm * h_prev + jnp.where(own, states, 0.0)
    hssd[...] = h_new
    for h in range(SSD_HEADS):
        g0 = (h // 2) * SSD_STATE
        ssdout_ref[h] = h_new[h * SSD_HEAD_DIM:(h + 1) * SSD_HEAD_DIM, g0:g0 + SSD_STATE]
    yield

    yssd = y_diag + y_off + dssd_ref[...] * xs
    yc = yssd * _silu(proj_ref[:, Z0:Z0 + SSD_WIDTH])
    mix_ref[:, ATT_WIDTH + S5_WIDTH:] = _rmsnorm(yc, ssdg_ref[...])


def _prompt_mixers(proj, pp, tabs, batch, nc, layer):
    def blk(b):
        return lambda c: (jnp.where(c == 0, b, 4 + b * (nc - 1) + c - 1), 0)

    const = lambda shape: pl.BlockSpec((None,) + shape, lambda c: (layer,) + (0,) * len(shape))
    whole = lambda shape: pl.BlockSpec(shape, lambda c: (0,) * len(shape))
    in_specs = [pl.BlockSpec(memory_space=pltpu.SMEM)]
    in_specs += [pl.BlockSpec((BLK, PROJ_W), blk(b)) for b in range(batch)]
    in_specs += [
        pl.BlockSpec((BLK, 128), lambda c: (c, 0)),
        pl.BlockSpec((BLK, 128), lambda c: (c, 0)),
        const((1, ATT_WIDTH)),
        const((S5_WIDTH, S5_LANES)), const((S5_WIDTH, S5_LANES)),
        const((S5_LANES, S5_WIDTH)), const((S5_LANES, S5_WIDTH)),
        const((SC_ROWS, S5_LANES)), const((1, S5_WIDTH)), const((S5_WIDTH, S5_WIDTH)),
        const((1, S5_WIDTH)), const((1, S5_WIDTH)),
        const((SSD_CONV, SSD_CONV_DIM)), const((1, SSD_CONV_DIM)),
        const((1, 128)), const((1, 128)), const((1, SSD_WIDTH)), const((1, SSD_WIDTH)),
        whole((BLK, BLK)), whole((BLK, BLK)),
    ]
    state_shapes = [(BLK, KV_WIDTH), (BLK, KV_WIDTH), (1, S5_LANES), (1, S5_LANES),
                    (SSD_CONV - 1, SSD_CONV_DIM), (SSD_HEADS, SSD_HEAD_DIM, SSD_STATE)]
    out_specs = [
        pl.BlockSpec((batch, BLK, D_MODEL), lambda c: (0, 0, 0)),
        pl.BlockSpec((batch, BLK, D_MODEL), lambda c: (0, jnp.maximum(c - 1, 0), 0)),
    ] + [whole((batch,) + s) for s in state_shapes]
    out_shape = [
        jax.ShapeDtypeStruct((SPECIAL // BLK, BLK, D_MODEL), F32),
        jax.ShapeDtypeStruct((batch, (nc - 1) * BLK, D_MODEL), F32),
    ] + [jax.ShapeDtypeStruct((batch,) + s, F32) for s in state_shapes]
    scratch = [pltpu.VMEM((batch,) + s, F32) for s in (
        (BLK, KV_WIDTH), (BLK, KV_WIDTH), (BLK, S5_LANES), (BLK, S5_LANES),
        (8, S5_LANES), (8, S5_LANES), (BLK + 8, SSD_CONV_DIM), (SSD_WIDTH, 128))]
    res = pl.pallas_call(
        functools.partial(_prompt_mixer_kernel, layer=layer, batch=batch),
        grid=(nc,),
        in_specs=in_specs, out_specs=out_specs, out_shape=out_shape, scratch_shapes=scratch,
        compiler_params=pltpu.CompilerParams(dimension_semantics=("arbitrary",),
                                             vmem_limit_bytes=48 * MIB),
        name="prompt_mixers",
    )(pp["sinks"], *([proj] * batch), tabs["cos_p"], tabs["sin_p"], pp["attn_out_g"],
      pp["bre"], pp["bim"], pp["cre"], pp["cim"], pp["sc"], pp["s5_d"], pp["glu_w"], pp["glu_b"],
      pp["s5_out_g"], pp["conv_w"], pp["conv_b"], pp["dt_bias"], pp["a_log"], pp["ssd_d"],
      pp["ssd_norm_g"], tabs["perm"], tabs["perm_t"])
    return [res[0].reshape(SPECIAL, D_MODEL)] + list(res[1:])


def _decode_mixer_kernel(*refs, layer, chained):
    n_in = 27 + (6 if chained else 0)
    (sinks_all, _, proj_ref, cos_ref, sin_ref, attg_ref,
     bret_ref, bimt_ref, cret_ref, cimt_ref, abar_ref, s5d_ref, gluw_ref, glub_ref, s5g_ref,
     convw_ref, convb_ref, dtb_ref, alog_ref, dssd_ref, ssdg_ref,
     kt_ref, vt_ref, s5re_in, s5im_in, conv_in, ssd_in) = refs[:27]
    (mix_ref, ktout_ref, vtout_ref, s5re_ref, s5im_ref, convout_ref, ssdout_ref,
     oatt_scr, xs_scr, xdt_scr, bt_scr, ct_scr, dcyt_scr, yt_scr) = refs[n_in:]
    sinks_ref = sinks_all.at[layer]
    i = pl.program_id(0)
    heads64 = (SSD_HEAD_DIM,) * SSD_HEADS

    @pl.when(i == 0)
    def _():
        mix_ref[DEC:, :] = jnp.zeros((mix_ref.shape[0] - DEC, D_MODEL), F32)

        u = proj_ref[:, U0:U0 + S5_WIDTH]
        ut = u.T.astype(BF16)
        a_re = abar_ref[0]
        a_im = abar_ref[1]
        h0r = s5re_in[...]
        h0i = s5im_in[...]
        hr = _dot(bret_ref[...], ut) + a_re * h0r - a_im * h0i
        hi = _dot(bimt_ref[...], ut) + a_re * h0i + a_im * h0r
        s5re_ref[...] = hr
        s5im_ref[...] = hi
        yt = _dot(cret_ref[...], hr.astype(BF16)) - _dot(cimt_ref[...], hi.astype(BF16))
        mix_ref[0:DEC, ATT_WIDTH:ATT_WIDTH + S5_WIDTH] = _s5_tail(
            yt.T, u, s5d_ref, gluw_ref, glub_ref, s5g_ref)

        xbc = proj_ref[:, XBC0:XBC0 + SSD_CONV_DIM]
        conv = convb_ref[...]
        for j in range(SSD_CONV - 1):
            conv = conv + conv_in[j] * convw_ref[j:j + 1, :]
        conv = conv + xbc * convw_ref[SSD_CONV - 1:SSD_CONV, :]
        convout_ref[0] = conv_in[1]
        convout_ref[1] = conv_in[2]
        convout_ref[2] = xbc
        xc = _silu(conv)
        xs = xc[:, 0:SSD_WIDTH]
        dt = _softplus(proj_ref[:, DT0:DT0 + 128] + dtb_ref[...])
        decay = jnp.exp(dt * (-jnp.exp(alog_ref[...])))
        xs_scr[...] = xs
        xdt_scr[...] = (xs * _lane_bcast(dt, heads64)).T
        bt_scr[...] = xc[:, SSD_WIDTH:SSD_WIDTH + 128].T
        ct_scr[...] = xc[:, SSD_WIDTH + 128:SSD_WIDTH + 256].T
        dcyt_scr[...] = decay.T[0:8, :]

    r0 = pl.multiple_of(i * DCH, DCH)
    cosv = cos_ref[...]
    sinv = sin_ref[...]
    k_rot = _rope(proj_ref[pl.ds(r0, DCH), K0:K0 + KV_WIDTH], cosv, sinv)
    v_new = proj_ref[pl.ds(r0, DCH), V0:V0 + KV_WIDTH]
    pad_rows = jnp.zeros((BLK - DCH, KV_WIDTH), F32)
    knew_t = jnp.concatenate([k_rot, pad_rows], axis=0).T
    vnew_t = jnp.concatenate([v_new, pad_rows], axis=0).T
    last = _iota((KV_WIDTH, BLK), 1) == BLK - 1
    for b in range(DCH):
        ktout_ref[b] = jnp.where(last, knew_t[:, b:b + 1], pltpu.roll(kt_ref[b], BLK - 1, 1))
        vtout_ref[b] = jnp.where(last, vnew_t[:, b:b + 1], pltpu.roll(vt_ref[b], BLK - 1, 1))
    left = _iota((DCH, 128), 1) < 64
    qs = []
    for h in range(ATT_HEADS):
        j, e, g = h // 2, h % 2, h // 4
        qt = _rope(proj_ref[pl.ds(r0, DCH), Q0 + j * 128:Q0 + (j + 1) * 128], cosv, sinv)
        qt = qt * (HEAD_DIM ** -0.5)
        if e != g:
            qt = pltpu.roll(qt, 64, 1)
        qs.append(jnp.where(left == (g == 0), qt, 0.0))
    qx = jnp.concatenate(qs, axis=0).astype(BF16)
    kt_cat = jnp.concatenate([kt_ref[b] for b in range(DCH)], axis=1).astype(BF16)
    vt_cat = jnp.concatenate([vt_ref[b] for b in range(DCH)], axis=1).astype(BF16)
    s_old = _dot(qx, kt_cat)
    s_new = _dot_nt(qx, k_rot.astype(BF16))
    rseq = _iota((ATT_HEADS * DCH, DCH * BLK), 0) & (DCH - 1)
    same = rseq == (_iota((ATT_HEADS * DCH, DCH * BLK), 1) >> 7)
    same_new = (_iota((ATT_HEADS * DCH, DCH), 0) & (DCH - 1)) == _iota((ATT_HEADS * DCH, DCH), 1)
    s_old = jnp.where(same, s_old, NEG_INF)
    s_new = jnp.where(same_new, s_new, NEG_INF)
    sink = jnp.concatenate(
        [jnp.full((DCH, 1), sinks_ref[h], F32) for h in range(ATT_HEADS)], axis=0)
    m = jnp.maximum(jnp.maximum(jnp.max(s_old, axis=-1, keepdims=True),
                                jnp.max(s_new, axis=-1, keepdims=True)), sink)
    p_old = jnp.exp(s_old - m)
    p_new = jnp.exp(s_new - m)
    den = (jnp.sum(p_old, axis=-1, keepdims=True) + jnp.sum(p_new, axis=-1, keepdims=True)
           + jnp.exp(sink - m))
    o = (_dot_nt((p_old / den).astype(BF16), vt_cat)
         + _dot((p_new / den).astype(BF16), v_new.astype(BF16)))
    o_tiles = []
    for j in range(ATT_HEADS // 2):
        g = j // 2
        a = o[(2 * j) * DCH:(2 * j + 1) * DCH]
        bb = o[(2 * j + 1) * DCH:(2 * j + 2) * DCH]
        if g == 1:
            a = pltpu.roll(a, 64, 1)
        else:
            bb = pltpu.roll(bb, 64, 1)
        o_tiles.append(jnp.where(left, a, bb))
    oatt_scr[pl.ds(r0, DCH), :] = jnp.concatenate(o_tiles, axis=1)

    rows_per_step = SSD_WIDTH // (DEC // DCH)
    head = i // (SSD_HEAD_DIM // rows_per_step)
    g0 = pl.multiple_of((head // 2) * SSD_STATE, SSD_STATE)
    dcy = dcyt_scr[pl.ds(head, 1), :]
    btg = bt_scr[pl.ds(g0, SSD_STATE), :]
    ctg = ct_scr[pl.ds(g0, SSD_STATE), :]
    for rr in range(rows_per_step):
        row = i * rows_per_step + rr
        h_new = dcy * ssd_in[rr] + xdt_scr[pl.ds(row, 1), :] * btg
        ssdout_ref[rr] = h_new
        yt_scr[pl.ds(row, 1), :] = jnp.sum(ctg * h_new, axis=0, keepdims=True)

    @pl.when(i == DEC // DCH - 1)
    def _():
        mix_ref[0:DEC, 0:ATT_WIDTH] = _rmsnorm(oatt_scr[...], attg_ref[...])
        xs = xs_scr[...]
        yssd = yt_scr[...].T + dssd_ref[...] * xs
        yc = yssd * _silu(proj_ref[:, Z0:Z0 + SSD_WIDTH])
        mix_ref[0:DEC, ATT_WIDTH + S5_WIDTH:] = _rmsnorm(yc, ssdg_ref[...])


def _decode_mixers(mix, proj, pp, tabs, states, prev, layer):
    rows = proj.shape[0]
    depth = states[0].shape[0]
    nsteps = DEC // DCH
    chained = prev is not None
    const = lambda shape: pl.BlockSpec((None,) + shape, lambda i: (layer,) + (0,) * len(shape))
    plain = lambda shape: pl.BlockSpec(shape, lambda i: (0,) * len(shape))
    rows_per_step = SSD_WIDTH // nsteps
    state_specs = [
        pl.BlockSpec((None, DCH, KV_WIDTH, BLK), lambda i: (layer, i, 0, 0)),
        pl.BlockSpec((None, DCH, KV_WIDTH, BLK), lambda i: (layer, i, 0, 0)),
        const((S5_LANES, DEC)), const((S5_LANES, DEC)),
        const((SSD_CONV - 1, DEC, SSD_CONV_DIM)),
        pl.BlockSpec((None, rows_per_step, SSD_STATE, DEC), lambda i: (layer, i, 0, 0)),
    ]
    in_specs = [
        pl.BlockSpec(memory_space=pltpu.SMEM),
        pl.BlockSpec(memory_space=pl.ANY),
        pl.BlockSpec((DEC, PROJ_W), lambda i: (DEC_ROW0 // DEC, 0)),
        plain((1, 128)), plain((1, 128)),
        const((1, ATT_WIDTH)),
        const((S5_LANES, S5_WIDTH)), const((S5_LANES, S5_WIDTH)),
        const((S5_WIDTH, S5_LANES)), const((S5_WIDTH, S5_LANES)),
        const((2, S5_LANES, DEC)), const((1, S5_WIDTH)), const((S5_WIDTH, S5_WIDTH)),
        const((1, S5_WIDTH)), const((1, S5_WIDTH)),
        const((SSD_CONV, SSD_CONV_DIM)), const((1, SSD_CONV_DIM)),
        const((1, 128)), const((1, 128)), const((1, SSD_WIDTH)), const((1, SSD_WIDTH)),
    ] + state_specs + ([pl.BlockSpec(memory_space=pl.ANY)] * 6 if chained else [])
    out_specs = [pl.BlockSpec((2 * DEC, D_MODEL), lambda i: (DEC_ROW0 // (2 * DEC), 0))] + state_specs
    out_shape = [jax.ShapeDtypeStruct(mix.shape, F32)] + [
        jax.ShapeDtypeStruct(s.shape, F32) for s in states]
    scratch = [
        pltpu.VMEM((DEC, ATT_WIDTH), F32), pltpu.VMEM((DEC, SSD_WIDTH), F32),
        pltpu.VMEM((SSD_WIDTH, DEC), F32), pltpu.VMEM((128, DEC), F32), pltpu.VMEM((128, DEC), F32),
        pltpu.VMEM((8, DEC), F32), pltpu.VMEM((SSD_WIDTH, DEC), F32),
    ]
    aliases = {1: 0}
    if chained:
        aliases.update({27 + k: 1 + k for k in range(6)})
    assert depth > layer
    return pl.pallas_call(
        functools.partial(_decode_mixer_kernel, layer=layer, chained=chained),
        grid=(nsteps,),
        in_specs=in_specs, out_specs=out_specs, out_shape=out_shape,
        scratch_shapes=scratch,
        input_output_aliases=aliases,
        compiler_params=pltpu.CompilerParams(dimension_semantics=("arbitrary",),
                                             vmem_limit_bytes=40 * MIB),
        name="decode_mixers",
    )(pp["sinks"], mix, proj, tabs["cos_d"], tabs["sin_d"], pp["attn_out_g"],
      pp["bre_t"], pp["bim_t"], pp["cre_t"], pp["cim_t"], pp["abar_t"], pp["s5_d"], pp["glu_w"],
      pp["glu_b"], pp["s5_out_g"], pp["conv_w"], pp["conv_b"], pp["dt_bias"], pp["a_log"],
      pp["ssd_d"], pp["ssd_norm_g"], *states, *(prev if chained else ()))


def _rope_tables(nc):
    half = HEAD_DIM // 2
    inv = ROPE_THETA ** (-jnp.arange(half, dtype=F32) / half)

    def tab(pos):
        ang = pos.astype(F32)[:, None] * inv[None, :]
        cos = jnp.cos(ang)
        sin = jnp.sin(ang)
        return jnp.tile(cos, (1, 4)), jnp.concatenate([-sin, sin, -sin, sin], axis=1)

    cos_p, sin_p = tab(jnp.arange(nc * BLK, dtype=jnp.int32) - FRONT)
    cos_d, sin_d = tab(jnp.full((1,), PAST_LEN, dtype=jnp.int32))
    rho = jnp.arange(BLK)
    perm = (jnp.arange(BLK)[None, :] == ((rho % 8) * SEG + rho // 8)[:, None]).astype(BF16)
    return {"cos_p": cos_p, "sin_p": sin_p, "cos_d": cos_d, "sin_d": sin_d,
            "perm": perm, "perm_t": perm.T}


def kernel(x_prompt, x_sample, cache_k, cache_v, state_s5_re, state_s5_im, state_ssd_conv, state_ssd,
           meta_tokens, ln1_g, w_in, attn_sinks, attn_out_g, s5_a_re, s5_a_im, s5_log_dt,
           s5_b_re, s5_b_im, s5_c_re, s5_c_im, s5_d, s5_glu_w, s5_glu_b, s5_out_g,
           ssd_conv_w, ssd_conv_b, ssd_dt_bias, ssd_a_log, ssd_d, ssd_norm_g, w_out,
           ln2_g, w_gate, w_up, w_down, lnf_g):
    batch, seq, _ = x_prompt.shape
    depth = w_in.shape[0]
    assert batch == 2 and x_sample.shape[0] == DEC and x_sample.shape[1] == 1
    assert seq % TM == 0 and cache_k.shape[2] == BLK
    nc = seq // BLK + 1

    zeros_front = jnp.zeros((FRONT, D_MODEL), F32)
    x = jnp.concatenate([zeros_front, meta_tokens, zeros_front, meta_tokens,
                         x_sample.reshape(DEC, D_MODEL), jnp.zeros((BLK, D_MODEL), F32),
                         x_prompt.reshape(batch * seq, D_MODEL)], axis=0)

    tabs = _rope_tables(nc)
    sc, bb_re, bb_im = _s5_prepare(s5_a_re, s5_a_im, s5_log_dt, s5_b_re, s5_b_im)
    bre, bim = _s5_block_diag_in(bb_re), _s5_block_diag_in(bb_im)
    cre, cim = _s5_block_diag_out(s5_c_re), _s5_block_diag_out(s5_c_im)
    w_in_p = jnp.pad(w_in, ((0, 0), (0, 0), (0, PROJ_W - N_IN))).astype(BF16)
    w_out_b, w_gate_b, w_up_b, w_down_b = (t.astype(BF16) for t in (w_out, w_gate, w_up, w_down))
    pad_heads = lambda t: jnp.pad(t, ((0, 0), (0, 128 - SSD_HEADS)))[:, None, :]
    row = lambda t: t[:, None, :]
    abar_t = jnp.broadcast_to(jnp.stack([sc[:, SC_ABAR_RE], sc[:, SC_ABAR_IM]], axis=1)[..., None],
                              (depth, 2, S5_LANES, DEC))
    pp = {
        "sinks": attn_sinks, "attn_out_g": row(attn_out_g),
        "bre": bre, "bim": bim, "cre": cre, "cim": cim, "sc": sc,
        "bre_t": jnp.swapaxes(bre, 1, 2), "bim_t": jnp.swapaxes(bim, 1, 2),
        "cre_t": jnp.swapaxes(cre, 1, 2), "cim_t": jnp.swapaxes(cim, 1, 2), "abar_t": abar_t,
        "s5_d": row(s5_d), "glu_w": s5_glu_w.astype(BF16), "glu_b": row(s5_glu_b),
        "s5_out_g": row(s5_out_g),
        "conv_w": ssd_conv_w, "conv_b": row(ssd_conv_b),
        "dt_bias": pad_heads(ssd_dt_bias), "a_log": pad_heads(ssd_a_log),
        "ssd_d": row(jnp.repeat(ssd_d, SSD_HEAD_DIM, axis=-1)), "ssd_norm_g": row(ssd_norm_g),
    }
    ln1_r, ln2_r = row(ln1_g), row(ln2_g)

    states = (
        jnp.transpose(cache_k, (0, 1, 3, 4, 2)).reshape(depth, DEC, KV_WIDTH, BLK),
        jnp.transpose(cache_v, (0, 1, 3, 4, 2)).reshape(depth, DEC, KV_WIDTH, BLK),
        jnp.transpose(state_s5_re, (0, 2, 3, 1)).reshape(depth, S5_LANES, DEC),
        jnp.transpose(state_s5_im, (0, 2, 3, 1)).reshape(depth, S5_LANES, DEC),
        jnp.transpose(state_ssd_conv, (0, 2, 1, 3)),
        jnp.transpose(state_ssd, (0, 2, 3, 4, 1)).reshape(depth, SSD_WIDTH, SSD_STATE, DEC),
    )

    outs_p = [[] for _ in range(6)]
    outs_s = None
    y_special = y_main = None
    for l in range(depth):
        proj = _inproj(x, ln1_r, w_in_p, l)
        res_p = _prompt_mixers(proj, pp, tabs, batch, nc, l)
        res_s = _decode_mixers(res_p[0], proj, pp, tabs, states, outs_s, l)
        outs_s = res_s[1:]
        for i in range(6):
            outs_p[i].append(res_p[i + 2])
        ffn_args = (res_s[0], res_p[1], x, w_out_b, ln2_r, w_gate_b, w_up_b, w_down_b, l)
        if l + 1 < depth:
            x = _outffn(*ffn_args)
        else:
            y_special, y_main = _outffn(*ffn_args, gf=lnf_g[None, :])

    y_prompt = y_main.reshape(batch, seq, D_MODEL)
    y_sample = y_special[DEC_ROW0:DEC_ROW0 + DEC].reshape(DEC, 1, D_MODEL)
    kv_p = lambda ts: jnp.transpose(
        jnp.stack(ts).reshape(depth, batch, 2, HEAD_DIM, BLK), (0, 1, 4, 2, 3))
    s5_p = lambda ts: jnp.stack(ts).reshape(depth, batch, S5_GROUPS, S5_STATE)
    kt_s, vt_s, s5re_s, s5im_s, conv_s, ssd_s = outs_s
    kv_s = lambda t: jnp.transpose(t.reshape(depth, DEC, 2, HEAD_DIM, BLK), (0, 1, 4, 2, 3))
    s5_s = lambda t: jnp.transpose(t.reshape(depth, S5_GROUPS, S5_STATE, DEC), (0, 3, 1, 2))
    return (y_prompt, y_sample,
            kv_p(outs_p[0]), kv_p(outs_p[1]), s5_p(outs_p[2]), s5_p(outs_p[3]),
            jnp.stack(outs_p[4]), jnp.stack(outs_p[5]),
            kv_s(kt_s), kv_s(vt_s), s5_s(s5re_s), s5_s(s5im_s),
            jnp.transpose(conv_s, (0, 2, 1, 3)),
            jnp.transpose(ssd_s.reshape(depth, SSD_HEADS, SSD_HEAD_DIM, SSD_STATE, DEC),
                          (0, 4, 1, 2, 3)))
```

```python
import functools

import jax
import jax.numpy as jnp
import numpy as np
from jax import lax
from jax.experimental import pallas as pl
from jax.experimental.pallas import tpu as pltpu

F32 = jnp.float32
BF16 = jnp.bfloat16

D_MODEL = 1024
N_META = 16
HEAD_DIM = 64
ATT_WIDTH = 512
ATT_HEADS = 8
KV_WIDTH = 128
S5_WIDTH = 256
S5_GROUPS = 16
S5_GROUP_CH = 16
S5_STATE = 64
S5_LANES = S5_GROUPS * S5_STATE
SSD_WIDTH = 256
SSD_HEADS = 4
SSD_HEAD_DIM = 64
SSD_STATE = 64
SSD_CONV = 4
SSD_CONV_DIM = 512
FFN_HIDDEN = 2816
NORM_EPS = 1e-6
ROPE_THETA = 10000.0
PAST_LEN = 8192
N_IN = 1796

BLK = 128
FRONT = BLK - N_META
TM = 512
SPECIAL = 4 * BLK
DEC = 128
DEC_ROW0 = 2 * BLK
DCH = 16
PROJ_W = 1920
Q0, K0, V0, U0, Z0, XBC0, DT0 = 0, 512, 640, 768, 1024, 1280, 1792
FFN_CHUNK = 256
SEG = BLK // 8
SC_ABAR_RE, SC_ABAR_IM, SC_SEG1, SC_SEG2, SC_SEG4, SC_CARRY, SC_ROWS = 0, 8, 16, 32, 48, 64, 80
NEG_INF = float("-inf")
MIB = 1024 * 1024


def _dot(a, b):
    return jnp.dot(a, b, preferred_element_type=F32)


def _dot_nt(a, b):
    return lax.dot_general(a, b, (((1,), (1,)), ((), ())), preferred_element_type=F32)


def _sigmoid(x):
    return 1.0 / (1.0 + jnp.exp(-x))


def _silu(x):
    return x * _sigmoid(x)


def _softplus(x):
    return jnp.maximum(x, 0.0) + jnp.log1p(jnp.exp(-jnp.abs(x)))


def _gelu_tanh(x):
    return x * _sigmoid((2.0 * 0.7978845608028654) * (x + 0.044715 * (x * x * x)))


def _rmsnorm(x, g):
    return x * lax.rsqrt(jnp.mean(x * x, axis=-1, keepdims=True) + NORM_EPS) * g


def _iota(shape, dim):
    return lax.broadcasted_iota(jnp.int32, shape, dim)


def _rope(x, cosv, sinv):
    first = (_iota(x.shape, 1) & 63) < 32
    partner = jnp.where(first, pltpu.roll(x, 96, 1), pltpu.roll(x, 32, 1))
    return x * cosv + partner * sinv


def _lane_bcast(x, widths):
    rows = x.shape[0]
    return jnp.concatenate(
        [jnp.broadcast_to(x[:, h:h + 1], (rows, w)) for h, w in enumerate(widths)], axis=1)


def _split3(x):
    hi = x.astype(BF16)
    r1 = x - hi.astype(F32)
    mid = r1.astype(BF16)
    lo = (r1 - mid.astype(F32)).astype(BF16)
    return hi, mid, lo


def _s5_prep_kernel(are_ref, aim_ref, ldt_ref, btre_ref, btim_ref, sc_ref, bbre_ref, bbim_ref):
    ar = are_ref[0]
    ai = aim_ref[0]
    dt = jnp.exp(ldt_ref[0])

    def power(k):
        kf = k.astype(F32)
        mag = jnp.exp((kf * dt) * ar)
        ang = (kf * dt) * ai
        return mag * jnp.cos(ang), mag * jnp.sin(ang)

    pre, pim = power(jnp.full((8, S5_LANES), 1, jnp.int32))
    sc_ref[0, SC_ABAR_RE:SC_ABAR_RE + 8, :] = pre
    sc_ref[0, SC_ABAR_IM:SC_ABAR_IM + 8, :] = pim
    row = _iota((8, S5_LANES), 0)
    for d, base in ((1, SC_SEG1), (2, SC_SEG2), (4, SC_SEG4)):
        sre, sim = power(jnp.full((8, S5_LANES), SEG * d, jnp.int32))
        sc_ref[0, base:base + 8, :] = jnp.where(row >= d, sre, 0.0)
        sc_ref[0, base + 8:base + 16, :] = jnp.where(row >= d, sim, 0.0)
    qre, qim = power(SEG * (row + 1))
    sc_ref[0, SC_CARRY:SC_CARRY + 8, :] = qre
    sc_ref[0, SC_CARRY + 8:SC_CARRY + 16, :] = qim
    abar_re = pre[0:1]
    abar_im = pim[0:1]
    den = ar * ar + ai * ai
    xr = abar_re - 1.0
    f_re = (xr * ar + abar_im * ai) / den
    f_im = (abar_im * ar - xr * ai) / den
    br = btre_ref[0]
    bi = btim_ref[0]
    bbre_ref[0] = f_re * br - f_im * bi
    bbim_ref[0] = f_re * bi + f_im * br


def _s5_prepare(a_re, a_im, log_dt, b_re, b_im):
    depth = a_re.shape[0]
    flat = lambda t: t.reshape(depth, 1, S5_LANES)
    ldt = jnp.repeat(log_dt, S5_STATE, axis=-1).reshape(depth, 1, S5_LANES)
    bt = lambda t: jnp.transpose(t, (0, 3, 1, 2)).reshape(depth, S5_GROUP_CH, S5_LANES)
    vec = pl.BlockSpec((1, 1, S5_LANES), lambda l: (l, 0, 0))
    mat = pl.BlockSpec((1, S5_GROUP_CH, S5_LANES), lambda l: (l, 0, 0))
    return pl.pallas_call(
        _s5_prep_kernel,
        grid=(depth,),
        in_specs=[vec, vec, vec, mat, mat],
        out_specs=[pl.BlockSpec((1, SC_ROWS, S5_LANES), lambda l: (l, 0, 0)), mat, mat],
        out_shape=[jax.ShapeDtypeStruct((depth, SC_ROWS, S5_LANES), F32),
                   jax.ShapeDtypeStruct((depth, S5_GROUP_CH, S5_LANES), F32),
                   jax.ShapeDtypeStruct((depth, S5_GROUP_CH, S5_LANES), F32)],
        name="s5_prep",
    )(flat(a_re), flat(a_im), ldt, bt(b_re), bt(b_im))


def _s5_block_diag_in(bb):
    depth = bb.shape[0]
    same = (jnp.arange(S5_GROUPS)[:, None, None] ==
            (jnp.arange(S5_LANES) // S5_STATE)[None, None, :])
    out = jnp.where(same[None], bb[:, None, :, :], 0.0)
    return out.reshape(depth, S5_WIDTH, S5_LANES).astype(BF16)


def _s5_block_diag_out(c):
    depth = c.shape[0]
    ct = jnp.transpose(c, (0, 1, 3, 2))
    same = jnp.eye(S5_GROUPS, dtype=bool)[None, :, None, :, None]
    out = jnp.where(same, ct[:, :, :, None, :], 0.0)
    return out.reshape(depth, S5_LANES, S5_WIDTH).astype(BF16)


def _inproj_tile(x, g_ref, w_ref):
    h = _rmsnorm(x, g_ref[...])
    row = _iota((TM, 1), 0)
    first_tile = pl.program_id(0) == 0
    front = jnp.where(first_tile, FRONT, 0)
    tail = jnp.where(first_tile, 3 * BLK, TM)
    pad = (row < front) | ((row >= BLK) & (row < BLK + front)) | (row >= tail)
    h = jnp.where(pad, 0.0, h)
    return _dot(h.astype(BF16), w_ref[...])


def _inproj_kernel(x_ref, g_ref, w_ref, o_ref):
    o_ref[...] = _inproj_tile(x_ref[...], g_ref, w_ref)


def _layer_resident(layer, shape):
    return pl.BlockSpec((None,) + shape, lambda i: (layer,) + (0,) * len(shape),
                        pipeline_mode=pl.Buffered(1))


def _inproj(x, g, w, layer):
    rows = x.shape[0]
    resident = functools.partial(_layer_resident, layer)
    return pl.pallas_call(
        _inproj_kernel,
        grid=(rows // TM,),
        in_specs=[pl.BlockSpec((TM, D_MODEL), lambda i: (i, 0)),
                  resident((1, D_MODEL)),
                  resident((D_MODEL, PROJ_W))],
        out_specs=pl.BlockSpec((TM, PROJ_W), lambda i: (i, 0)),
        out_shape=jax.ShapeDtypeStruct((rows, PROJ_W), F32),
        compiler_params=pltpu.CompilerParams(dimension_semantics=("arbitrary",),
                                             vmem_limit_bytes=32 * MIB),
        name="inproj",
    )(x, g, w)


def _outffn_body(mixs_ref, mixm_ref, x_ref, wo_ref, g2_ref, wg_ref, wu_ref, wd_ref):
    mix = jnp.where(pl.program_id(0) == 0, mixs_ref[...], mixm_ref[...])
    x1 = x_ref[...] + _dot(mix.astype(BF16), wo_ref[...])
    h = _rmsnorm(x1, g2_ref[...]).astype(BF16)
    acc = jnp.zeros((TM, D_MODEL), F32)
    for j in range(FFN_HIDDEN // FFN_CHUNK):
        sl = slice(j * FFN_CHUNK, (j + 1) * FFN_CHUNK)
        gate = _dot(h, wg_ref[:, sl])
        up = _dot(h, wu_ref[:, sl])
        acc = acc + _dot((_silu(gate) * up).astype(BF16), wd_ref[sl, :])
    return x1 + acc


def _outffn_kernel(*refs):
    g1_ref, win_ref, x_out_ref, proj_ref = refs[-4:]
    x_new = _outffn_body(*refs[:-4])
    x_out_ref[...] = x_new
    proj_ref[...] = _inproj_tile(x_new, g1_ref, win_ref)


def _outffn_final_kernel(*refs):
    gf_ref, ysp_ref, ymain_ref = refs[-3:]
    y = _rmsnorm(_outffn_body(*refs[:-3]), gf_ref[...])
    ymain_ref[...] = y

    @pl.when(pl.program_id(0) == 0)
    def _():
        ysp_ref[...] = y


def _outffn(mix_special, mix_main, x, wo, g2, wg, wu, wd, layer, g1=None, w_in=None, gf=None):
    rows = x.shape[0]
    resident = functools.partial(_layer_resident, layer)
    tile = pl.BlockSpec((TM, D_MODEL), lambda i: (i, 0))
    main_tile = pl.BlockSpec((TM, D_MODEL), lambda i: (jnp.maximum(i - 1, 0), 0))
    mix_main = mix_main.reshape(rows - SPECIAL, D_MODEL)
    in_specs = [pl.BlockSpec((TM, D_MODEL), lambda i: (0, 0), pipeline_mode=pl.Buffered(1)),
                main_tile, tile,
                resident((D_MODEL, D_MODEL)), resident((1, D_MODEL)),
                resident((D_MODEL, FFN_HIDDEN)), resident((D_MODEL, FFN_HIDDEN)),
                resident((FFN_HIDDEN, D_MODEL))]
    if gf is None:
        nxt = functools.partial(_layer_resident, layer + 1)
        return pl.pallas_call(
            _outffn_kernel, grid=(rows // TM,),
            in_specs=in_specs + [nxt((1, D_MODEL)), nxt((D_MODEL, PROJ_W))],
            out_specs=[tile, pl.BlockSpec((TM, PROJ_W), lambda i: (i, 0))],
            out_shape=[jax.ShapeDtypeStruct((rows, D_MODEL), F32),
                       jax.ShapeDtypeStruct((rows, PROJ_W), F32)],
            compiler_params=pltpu.CompilerParams(dimension_semantics=("arbitrary",),
                                                 vmem_limit_bytes=58 * MIB),
            name="outffn",
        )(mix_special, mix_main, x, wo, g2, wg, wu, wd, g1, w_in)
    params = pltpu.CompilerParams(dimension_semantics=("arbitrary",), vmem_limit_bytes=48 * MIB)
    return pl.pallas_call(
        _outffn_final_kernel, grid=(rows // TM,),
        in_specs=in_specs + [pl.BlockSpec((1, D_MODEL), lambda i: (0, 0))],
        out_specs=[pl.BlockSpec((TM, D_MODEL), lambda i: (0, 0)), main_tile],
        out_shape=[jax.ShapeDtypeStruct((SPECIAL, D_MODEL), F32),
                   jax.ShapeDtypeStruct((rows - SPECIAL, D_MODEL), F32)],
        compiler_params=params, name="outffn_final",
    )(mix_special, mix_main, x, wo, g2, wg, wu, wd, gf)


def _s5_tail(y, u, s5d_ref, gluw_ref, glub_ref, s5g_ref):
    y5 = _gelu_tanh(y + s5d_ref[...] * u)
    gate = _dot(y5.astype(BF16), gluw_ref[...]) + glub_ref[...]
    return _rmsnorm(y5 * _sigmoid(gate), s5g_ref[...])


def _pair_block_diag(t):
    left = _iota(t.shape, 1) < 64
    return jnp.concatenate([jnp.where(left, t, 0.0), jnp.where(left, 0.0, t)], axis=0)


N_MIXER_PARAMS = 18


def _prompt_mixer_kernel(*refs, layer, batch):
    sinks_all = refs[0]
    proj_refs = refs[1:1 + batch]
    shared = refs[1 + batch:1 + batch + N_MIXER_PARAMS]
    n_in = 1 + batch + N_MIXER_PARAMS
    mixs_ref, mixm_ref = refs[n_in:n_in + 2]
    outs = refs[n_in + 2:n_in + 8]
    scratch = refs[n_in + 8:]
    c = pl.program_id(0)

    @pl.when(c == 0)
    def _():
        for s in scratch:
            s[...] = jnp.zeros_like(s)

    live = [mixer(c, sinks_all.at[layer], proj_refs[b], *shared, mixm_ref.at[b],
                  *(o.at[b] for o in outs), *(s.at[b] for s in scratch))
            for mixer in (_prompt_attention, _prompt_s5, _prompt_ssd) for b in range(batch)]
    while live:
        live = [gen for gen in live if next(gen, "done") != "done"]

    @pl.when(c == 0)
    def _():
        mixs_ref[...] = mixm_ref[...]


def _prompt_attention(
        c, sinks_ref, proj_ref, cos_ref, sin_ref, attg_ref,
        bre_ref, bim_ref, cre_ref, cim_ref, sc_ref, s5d_ref, gluw_ref, glub_ref, s5g_ref,
        convw_ref, convb_ref, dtb_ref, alog_ref, dssd_ref, ssdg_ref,
        mix_ref, kout_ref, vout_ref, s5re_ref, s5im_ref, convout_ref, ssdout_ref,
        kprev, vprev, hre, him, carry_re, carry_im, xbuf, hssd, rows_scr):
    cosv = cos_ref[...]
    sinv = sin_ref[...]
    k_rot = _rope(proj_ref[:, K0:K0 + KV_WIDTH], cosv, sinv)
    vt_new = proj_ref[:, V0:V0 + KV_WIDTH].T
    kout_ref[...] = k_rot.T
    vout_ref[...] = vt_new
    kk = jnp.concatenate([kprev[...], k_rot], axis=0).astype(BF16)
    vvt = jnp.concatenate([vprev[...], vt_new], axis=1).astype(BF16)
    kprev[...] = k_rot
    vprev[...] = vt_new

    qts = [(_rope(proj_ref[:, Q0 + j * 128:Q0 + (j + 1) * 128], cosv, sinv) * (HEAD_DIM ** -0.5)).T
           for j in range(ATT_HEADS // 2)]
    krow = _iota((2 * BLK, BLK), 0)
    qcol = _iota((2 * BLK, BLK), 1)
    kpos = (c - 1) * BLK + krow - FRONT
    ok = (krow >= qcol) & (krow <= qcol + BLK) & (kpos >= 0)
    bias1 = jnp.where(ok, 0.0, NEG_INF)
    bias = jnp.concatenate([bias1] * 4, axis=1)
    zero_half = jnp.zeros((HEAD_DIM, BLK), F32)
    yield

    o_rows = []
    for g in range(2):
        blocks = []
        for r in range(4):
            h = 4 * g + r
            qh = qts[h // 2][(h % 2) * HEAD_DIM:(h % 2 + 1) * HEAD_DIM, :]
            blocks.append(jnp.concatenate([qh, zero_half] if g == 0 else [zero_half, qh], axis=0))
        qg = jnp.concatenate(blocks, axis=1).astype(BF16)
        s = _dot(kk, qg) + bias
        sink = jnp.concatenate(
            [jnp.full((1, BLK), sinks_ref[4 * g + r], F32) for r in range(4)], axis=1)
        yield
        m = jnp.maximum(jnp.max(s, axis=0, keepdims=True), sink)
        p = jnp.exp(s - m)
        yield
        inv_den = 1.0 / (jnp.sum(p, axis=0, keepdims=True) + jnp.exp(sink - m))
        og = _dot(vvt, p.astype(BF16))
        for r in range(4):
            cols = slice(r * BLK, (r + 1) * BLK)
            o_rows.append(og[g * HEAD_DIM:(g + 1) * HEAD_DIM, cols] * inv_den[:, cols])
        yield
    o_att = jnp.concatenate(o_rows, axis=0).T
    mix_ref[:, 0:ATT_WIDTH] = _rmsnorm(o_att, attg_ref[...])


def _prompt_s5(
        c, sinks_ref, proj_ref, cos_ref, sin_ref, attg_ref,
        bre_ref, bim_ref, cre_ref, cim_ref, sc_ref, s5d_ref, gluw_ref, glub_ref, s5g_ref,
        convw_ref, convb_ref, dtb_ref, alog_ref, dssd_ref, ssdg_ref,
        mix_ref, kout_ref, vout_ref, s5re_ref, s5im_ref, convout_ref, ssdout_ref,
        kprev, vprev, hre, him, carry_re, carry_im, xbuf, hssd, rows_scr):
    for j in range(S5_WIDTH // 128):
        rows_scr[j] = proj_ref[:, U0 + j * 128:U0 + (j + 1) * 128]
    u = jnp.concatenate(
        [jnp.concatenate([rows_scr[j, pl.ds(t, 8, stride=SEG), :] for j in range(S5_WIDTH // 128)], axis=1)
         for t in range(SEG)], axis=0)
    ub = u.astype(BF16)
    hre[...] = _dot(ub, bre_ref[...])
    him[...] = _dot(ub, bim_ref[...])
    yield
    first_seg = _iota((8, 128), 0) == 0
    for lt in range(S5_LANES // 128):
        ls = slice(lt * 128, (lt + 1) * 128)
        a_r = sc_ref[SC_ABAR_RE:SC_ABAR_RE + 8, ls]
        a_i = sc_ref[SC_ABAR_IM:SC_ABAR_IM + 8, ls]
        er = hre[0:8, ls]
        ei = him[0:8, ls]
        for t in range(1, SEG):
            rs = slice(8 * t, 8 * t + 8)
            er, ei = hre[rs, ls] + a_r * er - a_i * ei, him[rs, ls] + a_r * ei + a_i * er
        yield
        for d, b0 in ((1, SC_SEG1), (2, SC_SEG2), (4, SC_SEG4)):
            s_r = sc_ref[b0:b0 + 8, ls]
            s_i = sc_ref[b0 + 8:b0 + 16, ls]
            pr = pltpu.roll(er, d, 0)
            pi = pltpu.roll(ei, d, 0)
            er, ei = er + s_r * pr - s_i * pi, ei + s_r * pi + s_i * pr
        cr = carry_re[:, ls]
        ci = carry_im[:, ls]
        q_r = sc_ref[SC_CARRY:SC_CARRY + 8, ls]
        q_i = sc_ref[SC_CARRY + 8:SC_CARRY + 16, ls]
        tr = er + q_r * cr - q_i * ci
        ti = ei + q_r * ci + q_i * cr
        hr = jnp.where(first_seg, cr, pltpu.roll(tr, 1, 0))
        hi = jnp.where(first_seg, ci, pltpu.roll(ti, 1, 0))
        carry_re[:, ls] = jnp.broadcast_to(tr[7:8], (8, 128))
        carry_im[:, ls] = jnp.broadcast_to(ti[7:8], (8, 128))
        for t in range(SEG):
            rs = slice(8 * t, 8 * t + 8)
            hr, hi = hre[rs, ls] + a_r * hr - a_i * hi, him[rs, ls] + a_r * hi + a_i * hr
            hre[rs, ls] = hr
            him[rs, ls] = hi
        yield
    s5re_ref[...] = carry_re[0:1, :]
    s5im_ref[...] = carry_im[0:1, :]
    y_perm = _dot(hre[...].astype(BF16), cre_ref[...]) - _dot(him[...].astype(BF16), cim_ref[...])
    yield
    o_perm = _s5_tail(y_perm, u, s5d_ref, gluw_ref, glub_ref, s5g_ref)
    for j in range(S5_WIDTH // 128):
        for t in range(SEG):
            rows_scr[j, pl.ds(t, 8, stride=SEG), :] = o_perm[8 * t:8 * t + 8, j * 128:(j + 1) * 128]
        mix_ref[:, ATT_WIDTH + j * 128:ATT_WIDTH + (j + 1) * 128] = rows_scr[j]


def _prompt_ssd(
        c, sinks_ref, proj_ref, cos_ref, sin_ref, attg_ref,
        bre_ref, bim_ref, cre_ref, cim_ref, sc_ref, s5d_ref, gluw_ref, glub_ref, s5g_ref,
        convw_ref, convb_ref, dtb_ref, alog_ref, dssd_ref, ssdg_ref,
        mix_ref, kout_ref, vout_ref, s5re_ref, s5im_ref, convout_ref, ssdout_ref,
        kprev, vprev, hre, him, carry_re, carry_im, xbuf, hssd, rows_scr):
    xbc = proj_ref[:, XBC0:XBC0 + SSD_CONV_DIM]
    xbuf[8:8 + BLK, :] = xbc
    conv = convb_ref[...]
    for j in range(SSD_CONV):
        conv = conv + xbuf[5 + j:5 + j + BLK, :] * convw_ref[j:j + 1, :]
    yield
    xc = _silu(conv)
    convout_ref[...] = xbc[BLK - (SSD_CONV - 1):BLK, :]
    xbuf[0:8, :] = xbc[BLK - 8:BLK, :]
    xs = xc[:, 0:SSD_WIDTH]
    bm = xc[:, SSD_WIDTH:SSD_WIDTH + 128]
    cm = xc[:, SSD_WIDTH + 128:SSD_WIDTH + 256]
    yield

    lane = _iota((BLK, 128), 1)
    row = _iota((BLK, 128), 0)
    head_r = _iota((8, BLK), 0)
    time_c = _iota((8, BLK), 1)
    live = (head_r < SSD_HEADS) & (time_c >= jnp.where(c == 0, FRONT, 0))
    raw_t = proj_ref[:, DT0:DT0 + 128].T[0:8, :]
    dt_t = jnp.where(live, _softplus(raw_t + dtb_ref[...]), 0.0)
    dta_t = dt_t * (-jnp.exp(alog_ref[...]))
    causal = lane <= row
    triu = jnp.where(row <= lane, 1.0, 0.0).astype(BF16)
    hi3, mid3, lo3 = _split3(dta_t)
    cs_t = _dot(hi3, triu) + _dot(mid3, triu) + _dot(lo3, triu)
    yield
    cs_last = cs_t[:, BLK - 1:BLK]
    packed = jnp.concatenate(
        [dt_t, dt_t * jnp.exp(cs_last - cs_t), jnp.exp(cs_t), cs_t, jnp.zeros((BLK - 32, BLK), F32)], axis=0)
    cols = packed.T
    heads64 = (SSD_HEAD_DIM,) * SSD_HEADS
    xd = xs * _lane_bcast(cols[:, 0:SSD_HEADS], heads64)
    dxd = xs * _lane_bcast(cols[:, 8:8 + SSD_HEADS], heads64)
    ecs = _lane_bcast(cols[:, 16:16 + SSD_HEADS], heads64)
    cs = cols[:, 24:24 + SSD_HEADS]
    yield

    bmb = bm.astype(BF16)
    left = lane < 64
    cb = (_dot_nt(jnp.where(left, cm, 0.0).astype(BF16), bmb),
          _dot_nt(jnp.where(left, 0.0, cm).astype(BF16), bmb))
    scores = []
    for h in range(SSD_HEADS):
        seg = cs[:, h:h + 1] - cs_t[h:h + 1, :]
        scores.append((cb[h // 2] * jnp.exp(jnp.where(causal, seg, NEG_INF))).astype(BF16))
        yield
    y_diag = jnp.concatenate(
        [_dot(jnp.concatenate([scores[2 * j], scores[2 * j + 1]], axis=1),
              _pair_block_diag(xd[:, j * 128:(j + 1) * 128]).astype(BF16)) for j in range(2)], axis=1)
    yield

    h_prev = hssd[...]
    y_off = _dot_nt(cm.astype(BF16), h_prev.astype(BF16)) * ecs
    yield
    states = _dot(dxd.T.astype(BF16), bmb)
    own = (_iota((SSD_WIDTH, 128), 0) >> 7) == (_iota((SSD_WIDTH, 128), 1) >> 6)
    cd = jnp.exp(cs_last)
    cdm = jnp.concatenate(
        [jnp.broadcast_to(cd[h:h + 1, :], (SSD_HEAD_DIM, 128)) for h in range(SSD_HEADS)], axis=0)
    h_new = cdm * h_prev + jnp.where(own, states, 0.0)
    hssd[...] = h_new
    for h in range(SSD_HEADS):
        g0 = (h // 2) * SSD_STATE
        ssdout_ref[h] = h_new[h * SSD_HEAD_DIM:(h + 1) * SSD_HEAD_DIM, g0:g0 + SSD_STATE]
    yield

    yssd = y_diag + y_off + dssd_ref[...] * xs
    yc = yssd * _silu(proj_ref[:, Z0:Z0 + SSD_WIDTH])
    mix_ref[:, ATT_WIDTH + S5_WIDTH:] = _rmsnorm(yc, ssdg_ref[...])


def _prompt_mixers(proj, pp, tabs, batch, nc, layer):
    def blk(b):
        return lambda c: (jnp.where(c == 0, b, 4 + b * (nc - 1) + c - 1), 0)

    const = lambda shape: pl.BlockSpec((None,) + shape, lambda c: (layer,) + (0,) * len(shape))
    whole = lambda shape: pl.BlockSpec(shape, lambda c: (0,) * len(shape))
    in_specs = [pl.BlockSpec(memory_space=pltpu.SMEM)]
    in_specs += [pl.BlockSpec((BLK, PROJ_W), blk(b)) for b in range(batch)]
    in_specs += [
        pl.BlockSpec((BLK, 128), lambda c: (c, 0)),
        pl.BlockSpec((BLK, 128), lambda c: (c, 0)),
        const((1, ATT_WIDTH)),
        const((S5_WIDTH, S5_LANES)), const((S5_WIDTH, S5_LANES)),
        const((S5_LANES, S5_WIDTH)), const((S5_LANES, S5_WIDTH)),
        const((SC_ROWS, S5_LANES)), const((1, S5_WIDTH)), const((S5_WIDTH, S5_WIDTH)),
        const((1, S5_WIDTH)), const((1, S5_WIDTH)),
        const((SSD_CONV, SSD_CONV_DIM)), const((1, SSD_CONV_DIM)),
        const((8, BLK)), const((8, BLK)), const((1, SSD_WIDTH)), const((1, SSD_WIDTH)),
    ]
    state_shapes =[(BLK, KV_WIDTH), (BLK, KV_WIDTH), (1, S5_LANES), (1, S5_LANES),
                    (SSD_CONV - 1, SSD_CONV_DIM), (SSD_HEADS, SSD_HEAD_DIM, SSD_STATE)]
    out_specs = [
        pl.BlockSpec((batch, BLK, D_MODEL), lambda c: (0, 0, 0)),
        pl.BlockSpec((batch, BLK, D_MODEL), lambda c: (0, jnp.maximum(c - 1, 0), 0)),
    ] + [whole((batch,) + s) for s in state_shapes]
    out_shape = [
        jax.ShapeDtypeStruct((SPECIAL // BLK, BLK, D_MODEL), F32),
        jax.ShapeDtypeStruct((batch, (nc - 1) * BLK, D_MODEL), F32),
    ] + [jax.ShapeDtypeStruct((batch,) + s, F32) for s in state_shapes]
    scratch = [pltpu.VMEM((batch,) + s, F32) for s in (
        (BLK, KV_WIDTH), (BLK, KV_WIDTH), (BLK, S5_LANES), (BLK, S5_LANES),
        (8, S5_LANES), (8, S5_LANES), (BLK + 8, SSD_CONV_DIM), (SSD_WIDTH, 128),
        (S5_WIDTH // 128, BLK, 128))]
    res = pl.pallas_call(
        functools.partial(_prompt_mixer_kernel, layer=layer, batch=batch),
        grid=(nc,),
        in_specs=in_specs, out_specs=out_specs, out_shape=out_shape, scratch_shapes=scratch,
        compiler_params=pltpu.CompilerParams(dimension_semantics=("arbitrary",),
                                             vmem_limit_bytes=48 * MIB),
        name="prompt_mixers",
    )(pp["sinks"], *([proj] * batch), tabs["cos_p"], tabs["sin_p"], pp["attn_out_g"],
      pp["bre"], pp["bim"], pp["cre"], pp["cim"], pp["sc"], pp["s5_d"], pp["glu_w"], pp["glu_b"],
      pp["s5_out_g"], pp["conv_w"], pp["conv_b"], pp["dt_bias_c"], pp["a_log_c"], pp["ssd_d"],
      pp["ssd_norm_g"])
    return [res[0].reshape(SPECIAL, D_MODEL)] + list(res[1:])


def _decode_mixer_kernel(*refs, layer, chained):
    n_in = 27 + (6 if chained else 0)
    (sinks_all, _, proj_ref, cos_ref, sin_ref, attg_ref,
     bret_ref, bimt_ref, cret_ref, cimt_ref, abar_ref, s5d_ref, gluw_ref, glub_ref, s5g_ref,
     convw_ref, convb_ref, dtb_ref, alog_ref, dssd_ref, ssdg_ref,
     kt_ref, vt_ref, s5re_in, s5im_in, conv_in, ssd_in) = refs[:27]
    (mix_ref, ktout_ref, vtout_ref, s5re_ref, s5im_ref, convout_ref, ssdout_ref,
     oatt_scr, xs_scr, xdt_scr, bt_scr, ct_scr, dcyt_scr, yt_scr) = refs[n_in:]
    sinks_ref = sinks_all.at[layer]
    i = pl.program_id(0)
    heads64 = (SSD_HEAD_DIM,) * SSD_HEADS

    @pl.when(i == 0)
    def _():
        mix_ref[DEC:, :] = jnp.zeros((mix_ref.shape[0] - DEC, D_MODEL), F32)

        u = proj_ref[:, U0:U0 + S5_WIDTH]
        ut = u.T.astype(BF16)
        a_re = abar_ref[0]
        a_im = abar_ref[1]
        h0r = s5re_in[...]
        h0i = s5im_in[...]
        hr = _dot(bret_ref[...], ut) + a_re * h0r - a_im * h0i
        hi = _dot(bimt_ref[...], ut) + a_re * h0i + a_im * h0r
        s5re_ref[...] = hr
        s5im_ref[...] = hi
        yt = _dot(cret_ref[...], hr.astype(BF16)) - _dot(cimt_ref[...], hi.astype(BF16))
        mix_ref[0:DEC, ATT_WIDTH:ATT_WIDTH + S5_WIDTH] = _s5_tail(
            yt.T, u, s5d_ref, gluw_ref, glub_ref, s5g_ref)

        xbc = proj_ref[:, XBC0:XBC0 + SSD_CONV_DIM]
        conv = convb_ref[...]
        for j in range(SSD_CONV - 1):
            conv = conv + conv_in[j] * convw_ref[j:j + 1, :]
        conv = conv + xbc * convw_ref[SSD_CONV - 1:SSD_CONV, :]
        convout_ref[0] = conv_in[1]
        convout_ref[1] = conv_in[2]
        convout_ref[2] = xbc
        xc = _silu(conv)
        xs = xc[:, 0:SSD_WIDTH]
        dt = _softplus(proj_ref[:, DT0:DT0 + 128] + dtb_ref[...])
        decay = jnp.exp(dt * (-jnp.exp(alog_ref[...])))
        xs_scr[...] = xs
        xdt_scr[...] = (xs * _lane_bcast(dt, heads64)).T
        bt_scr[...] = xc[:, SSD_WIDTH:SSD_WIDTH + 128].T
        ct_scr[...] = xc[:, SSD_WIDTH + 128:SSD_WIDTH + 256].T
        dcyt_scr[...] = decay.T[0:8, :]

    r0 = pl.multiple_of(i * DCH, DCH)
    cosv = cos_ref[...]
    sinv = sin_ref[...]
    k_rot = _rope(proj_ref[pl.ds(r0, DCH), K0:K0 + KV_WIDTH], cosv, sinv)
    v_new = proj_ref[pl.ds(r0, DCH), V0:V0 + KV_WIDTH]
    pad_rows = jnp.zeros((BLK - DCH, KV_WIDTH), F32)
    knew_t = jnp.concatenate([k_rot, pad_rows], axis=0).T
    vnew_t = jnp.concatenate([v_new, pad_rows], axis=0).T
    last = _iota((KV_WIDTH, BLK), 1) == BLK - 1
    for b in range(DCH):
        ktout_ref[b] = jnp.where(last, knew_t[:, b:b + 1], pltpu.roll(kt_ref[b], BLK - 1, 1))
        vtout_ref[b] = jnp.where(last, vnew_t[:, b:b + 1], pltpu.roll(vt_ref[b], BLK - 1, 1))
    left = _iota((DCH, 128), 1) < 64
    qs = []
    for h in range(ATT_HEADS):
        j, e, g = h // 2, h % 2, h // 4
        qt = _rope(proj_ref[pl.ds(r0, DCH), Q0 + j * 128:Q0 + (j + 1) * 128], cosv, sinv)
        qt = qt * (HEAD_DIM ** -0.5)
        if e != g:
            qt = pltpu.roll(qt, 64, 1)
        qs.append(jnp.where(left == (g == 0), qt, 0.0))
    qx = jnp.concatenate(qs, axis=0).astype(BF16)
    kt_cat = jnp.concatenate([kt_ref[b] for b in range(DCH)], axis=1).astype(BF16)
    vt_cat = jnp.concatenate([vt_ref[b] for b in range(DCH)], axis=1).astype(BF16)
    s_old = _dot(qx, kt_cat)
    s_new = _dot_nt(qx, k_rot.astype(BF16))
    rseq = _iota((ATT_HEADS * DCH, DCH * BLK), 0) & (DCH - 1)
    same = rseq == (_iota((ATT_HEADS * DCH, DCH * BLK), 1) >> 7)
    same_new = (_iota((ATT_HEADS * DCH, DCH), 0) & (DCH - 1)) == _iota((ATT_HEADS * DCH, DCH), 1)
    s_old = jnp.where(same, s_old, NEG_INF)
    s_new = jnp.where(same_new, s_new, NEG_INF)
    sink = jnp.concatenate(
        [jnp.full((DCH, 1), sinks_ref[h], F32) for h in range(ATT_HEADS)], axis=0)
    m = jnp.maximum(jnp.maximum(jnp.max(s_old, axis=-1, keepdims=True),
                                jnp.max(s_new, axis=-1, keepdims=True)), sink)
    p_old = jnp.exp(s_old - m)
    p_new = jnp.exp(s_new - m)
    den = (jnp.sum(p_old, axis=-1, keepdims=True) + jnp.sum(p_new, axis=-1, keepdims=True)
           + jnp.exp(sink - m))
    o = (_dot_nt((p_old / den).astype(BF16), vt_cat)
         + _dot((p_new / den).astype(BF16), v_new.astype(BF16)))
    o_tiles = []
    for j in range(ATT_HEADS // 2):
        g = j // 2
        a = o[(2 * j) * DCH:(2 * j + 1) * DCH]
        bb = o[(2 * j + 1) * DCH:(2 * j + 2) * DCH]
        if g == 1:
            a = pltpu.roll(a, 64, 1)
        else:
            bb = pltpu.roll(bb, 64, 1)
        o_tiles.append(jnp.where(left, a, bb))
    oatt_scr[pl.ds(r0, DCH), :] = jnp.concatenate(o_tiles, axis=1)

    rows_per_step = SSD_WIDTH // (DEC // DCH)
    head = i // (SSD_HEAD_DIM // rows_per_step)
    g0 = pl.multiple_of((head // 2) * SSD_STATE, SSD_STATE)
    dcy = dcyt_scr[pl.ds(head, 1), :]
    btg = bt_scr[pl.ds(g0, SSD_STATE), :]
    ctg = ct_scr[pl.ds(g0, SSD_STATE), :]
    for rr in range(rows_per_step):
        row = i * rows_per_step + rr
        h_new = dcy * ssd_in[rr] + xdt_scr[pl.ds(row, 1), :] * btg
        ssdout_ref[rr] = h_new
        yt_scr[pl.ds(row, 1), :] = jnp.sum(ctg * h_new, axis=0, keepdims=True)

    @pl.when(i == DEC // DCH - 1)
    def _():
        mix_ref[0:DEC, 0:ATT_WIDTH] = _rmsnorm(oatt_scr[...], attg_ref[...])
        xs = xs_scr[...]
        yssd = yt_scr[...].T + dssd_ref[...] * xs
        yc = yssd * _silu(proj_ref[:, Z0:Z0 + SSD_WIDTH])
        mix_ref[0:DEC, ATT_WIDTH + S5_WIDTH:] = _rmsnorm(yc, ssdg_ref[...])


def _decode_mixers(mix, proj, pp, tabs, states, prev, layer):
    rows = proj.shape[0]
    depth = states[0].shape[0]
    nsteps = DEC // DCH
    chained = prev is not None
    const = lambda shape: pl.BlockSpec((None,) + shape, lambda i: (layer,) + (0,) * len(shape))
    plain = lambda shape: pl.BlockSpec(shape, lambda i: (0,) * len(shape))
    rows_per_step = SSD_WIDTH // nsteps
    state_specs = [
        pl.BlockSpec((None, DCH, KV_WIDTH, BLK), lambda i: (layer, i, 0, 0)),
        pl.BlockSpec((None, DCH, KV_WIDTH, BLK), lambda i: (layer, i, 0, 0)),
        const((S5_LANES, DEC)), const((S5_LANES, DEC)),
        const((SSD_CONV - 1, DEC, SSD_CONV_DIM)),
        pl.BlockSpec((None, rows_per_step, SSD_STATE, DEC), lambda i: (layer, i, 0, 0)),
    ]
    in_specs = [
        pl.BlockSpec(memory_space=pltpu.SMEM),
        pl.BlockSpec(memory_space=pl.ANY),
        pl.BlockSpec((DEC, PROJ_W), lambda i: (DEC_ROW0 // DEC, 0)),
        plain((1, 128)), plain((1, 128)),
        const((1, ATT_WIDTH)),
        const((S5_LANES, S5_WIDTH)), const((S5_LANES, S5_WIDTH)),
        const((S5_WIDTH, S5_LANES)), const((S5_WIDTH, S5_LANES)),
        const((2, S5_LANES, DEC)), const((1, S5_WIDTH)), const((S5_WIDTH, S5_WIDTH)),
        const((1, S5_WIDTH)), const((1, S5_WIDTH)),
        const((SSD_CONV, SSD_CONV_DIM)), const((1, SSD_CONV_DIM)),
        const((1, 128)), const((1, 128)), const((1, SSD_WIDTH)), const((1, SSD_WIDTH)),
    ] + state_specs + ([pl.BlockSpec(memory_space=pl.ANY)] * 6 if chained else [])
    out_specs = [pl.BlockSpec((2 * DEC, D_MODEL), lambda i: (DEC_ROW0 // (2 * DEC), 0))] + state_specs
    out_shape = [jax.ShapeDtypeStruct(mix.shape, F32)] + [
        jax.ShapeDtypeStruct(s.shape, F32) for s in states]
    scratch = [
        pltpu.VMEM((DEC, ATT_WIDTH), F32), pltpu.VMEM((DEC, SSD_WIDTH), F32),
        pltpu.VMEM((SSD_WIDTH, DEC), F32), pltpu.VMEM((128, DEC), F32), pltpu.VMEM((128, DEC), F32),
        pltpu.VMEM((8, DEC), F32), pltpu.VMEM((SSD_WIDTH, DEC), F32),
    ]
    aliases = {1: 0}
    if chained:
        aliases.update({27 + k: 1 + k for k in range(6)})
    assert depth > layer
    return pl.pallas_call(
        functools.partial(_decode_mixer_kernel, layer=layer, chained=chained),
        grid=(nsteps,),
        in_specs=in_specs, out_specs=out_specs, out_shape=out_shape,
        scratch_shapes=scratch,
        input_output_aliases=aliases,
        compiler_params=pltpu.CompilerParams(dimension_semantics=("arbitrary",),
                                             vmem_limit_bytes=40 * MIB),
        name="decode_mixers",
    )(pp["sinks"], mix, proj, tabs["cos_d"], tabs["sin_d"], pp["attn_out_g"],
      pp["bre_t"], pp["bim_t"], pp["cre_t"], pp["cim_t"], pp["abar_t"], pp["s5_d"], pp["glu_w"],
      pp["glu_b"], pp["s5_out_g"], pp["conv_w"], pp["conv_b"], pp["dt_bias"], pp["a_log"],
      pp["ssd_d"], pp["ssd_norm_g"], *states, *(prev if chained else ()))


def _rope_tables(nc):
    half = HEAD_DIM // 2
    inv = ROPE_THETA ** (-np.arange(half, dtype=np.float64) / half)

    def tab(pos):
        ang = pos.astype(np.float64)[:, None] * inv[None, :]
        cos = np.cos(ang).astype(np.float32)
        sin = np.sin(ang).astype(np.float32)
        return np.tile(cos, (1, 4)), np.concatenate([-sin, sin, -sin, sin], axis=1)

    cos_p, sin_p = tab(np.arange(nc * BLK, dtype=np.int32) - FRONT)
    cos_d, sin_d = tab(np.full((1,), PAST_LEN, dtype=np.int32))
    return {"cos_p": cos_p, "sin_p": sin_p, "cos_d": cos_d, "sin_d": sin_d}


def kernel(x_prompt, x_sample, cache_k, cache_v, state_s5_re, state_s5_im, state_ssd_conv, state_ssd,
           meta_tokens, ln1_g, w_in, attn_sinks, attn_out_g, s5_a_re, s5_a_im, s5_log_dt,
           s5_b_re, s5_b_im, s5_c_re, s5_c_im, s5_d, s5_glu_w, s5_glu_b, s5_out_g,
           ssd_conv_w, ssd_conv_b, ssd_dt_bias, ssd_a_log, ssd_d, ssd_norm_g, w_out,
           ln2_g, w_gate, w_up, w_down, lnf_g):
    batch, seq, _ = x_prompt.shape
    depth = w_in.shape[0]
    assert batch == 2 and x_sample.shape[0] == DEC and x_sample.shape[1] == 1
    assert seq % TM == 0 and cache_k.shape[2] == BLK
    nc = seq // BLK + 1

    zeros_front = jnp.zeros((FRONT, D_MODEL), F32)
    x = jnp.concatenate([zeros_front, meta_tokens, zeros_front, meta_tokens,
                         x_sample.reshape(DEC, D_MODEL), jnp.zeros((BLK, D_MODEL), F32),
                         x_prompt.reshape(batch * seq, D_MODEL)], axis=0)

    tabs = _rope_tables(nc)
    sc, bb_re, bb_im = _s5_prepare(s5_a_re, s5_a_im, s5_log_dt, s5_b_re, s5_b_im)
    bre, bim = _s5_block_diag_in(bb_re), _s5_block_diag_in(bb_im)
    cre, cim = _s5_block_diag_out(s5_c_re), _s5_block_diag_out(s5_c_im)
    w_in_p = jnp.pad(w_in, ((0, 0), (0, 0), (0, PROJ_W - N_IN))).astype(BF16)
    w_out_b, w_gate_b, w_up_b, w_down_b = (t.astype(BF16) for t in (w_out, w_gate, w_up, w_down))
    pad_heads = lambda t: jnp.pad(t, ((0, 0), (0, 128 - SSD_HEADS)))[:, None, :]
    head_rows = lambda t: jnp.broadcast_to(
        jnp.pad(t, ((0, 0), (0, 8 - SSD_HEADS)))[:, :, None], (depth, 8, BLK))
    row = lambda t: t[:, None, :]
    abar_t = jnp.broadcast_to(jnp.stack([sc[:, SC_ABAR_RE], sc[:, SC_ABAR_IM]], axis=1)[..., None],
                              (depth, 2, S5_LANES, DEC))
    pp = {
        "sinks": attn_sinks, "attn_out_g": row(attn_out_g),
        "bre": bre, "bim": bim, "cre": cre, "cim": cim, "sc": sc,
        "bre_t": jnp.swapaxes(bre, 1, 2), "bim_t": jnp.swapaxes(bim, 1, 2),
        "cre_t": jnp.swapaxes(cre, 1, 2), "cim_t": jnp.swapaxes(cim, 1, 2), "abar_t": abar_t,
        "s5_d": row(s5_d), "glu_w": s5_glu_w.astype(BF16), "glu_b": row(s5_glu_b),
        "s5_out_g": row(s5_out_g),
        "conv_w": ssd_conv_w, "conv_b": row(ssd_conv_b),
        "dt_bias": pad_heads(ssd_dt_bias), "a_log": pad_heads(ssd_a_log),
        "dt_bias_c": head_rows(ssd_dt_bias), "a_log_c": head_rows(ssd_a_log),
        "ssd_d": row(jnp.repeat(ssd_d, SSD_HEAD_DIM, axis=-1)), "ssd_norm_g": row(ssd_norm_g),
    }
    ln1_r, ln2_r = row(ln1_g), row(ln2_g)

    states = (
        jnp.transpose(cache_k, (0, 1, 3, 4, 2)).reshape(depth, DEC, KV_WIDTH, BLK),
        jnp.transpose(cache_v, (0, 1, 3, 4, 2)).reshape(depth, DEC, KV_WIDTH, BLK),
        jnp.transpose(state_s5_re, (0, 2, 3, 1)).reshape(depth, S5_LANES, DEC),
        jnp.transpose(state_s5_im, (0, 2, 3, 1)).reshape(depth, S5_LANES, DEC),
        jnp.transpose(state_ssd_conv, (0, 2, 1, 3)),
        jnp.transpose(state_ssd, (0, 2, 3, 4, 1)).reshape(depth, SSD_WIDTH, SSD_STATE, DEC),
    )

    outs_p = [[] for _ in range(6)]
    outs_s = None
    y_special = y_main = None
    proj = _inproj(x, ln1_r, w_in_p, 0)
    for l in range(depth):
        res_p = _prompt_mixers(proj, pp, tabs, batch, nc, l)
        res_s = _decode_mixers(res_p[0], proj, pp, tabs, states, outs_s, l)
        outs_s = res_s[1:]
        for i in range(6):
            outs_p[i].append(res_p[i + 2])
        ffn_args = (res_s[0], res_p[1], x, w_out_b, ln2_r, w_gate_b, w_up_b, w_down_b, l)
        if l + 1 < depth:
            x, proj = _outffn(*ffn_args, g1=ln1_r, w_in=w_in_p)
        else:
            y_special, y_main = _outffn(*ffn_args, gf=lnf_g[None, :])

    y_prompt = y_main.reshape(batch, seq, D_MODEL)
    y_sample = y_special[DEC_ROW0:DEC_ROW0 + DEC].reshape(DEC, 1, D_MODEL)
    kv_p = lambda ts: jnp.transpose(
        jnp.stack(ts).reshape(depth, batch, 2, HEAD_DIM, BLK), (0, 1, 4, 2, 3))
    s5_p = lambda ts: jnp.stack(ts).reshape(depth, batch, S5_GROUPS, S5_STATE)
    kt_s, vt_s, s5re_s, s5im_s, conv_s, ssd_s = outs_s
    kv_s = lambda t: jnp.transpose(t.reshape(depth, DEC, 2, HEAD_DIM, BLK), (0, 1, 4, 2, 3))
    s5_s = lambda t: jnp.transpose(t.reshape(depth, S5_GROUPS, S5_STATE, DEC), (0, 3, 1, 2))
    return (y_prompt, y_sample,
            kv_p(outs_p[0]), kv_p(outs_p[1]), s5_p(outs_p[2]), s5_p(outs_p[3]),
            jnp.stack(outs_p[4]), jnp.stack(outs_p[5]),
            kv_s(kt_s), kv_s(vt_s), s5_s(s5re_s), s5_s(s5im_s),
            jnp.transpose(conv_s, (0, 2, 1, 3)),
            jnp.transpose(ssd_s.reshape(depth, SSD_HEADS, SSD_HEAD_DIM, SSD_STATE, DEC),
                          (0, 4, 1, 2, 3)))
```

```python
import functools

import jax
import jax.numpy as jnp
import numpy as np
from jax import lax
from jax.experimental import pallas as pl
from jax.experimental.pallas import tpu as pltpu

F32 = jnp.float32
BF16 = jnp.bfloat16

D_MODEL = 1024
N_META = 16
HEAD_DIM = 64
ATT_WIDTH = 512
ATT_HEADS = 8
KV_WIDTH = 128
S5_WIDTH = 256
S5_GROUPS = 16
S5_GROUP_CH = 16
S5_STATE = 64
S5_LANES = S5_GROUPS * S5_STATE
SSD_WIDTH = 256
SSD_HEADS = 4
SSD_HEAD_DIM = 64
SSD_STATE = 64
SSD_CONV = 4
SSD_CONV_DIM = 512
FFN_HIDDEN = 2816
NORM_EPS = 1e-6
ROPE_THETA = 10000.0
PAST_LEN = 8192
N_IN = 1796

BLK = 128
FRONT = BLK - N_META
TM = 512
SPECIAL = 4 * BLK
DEC = 128
DEC_ROW0 = 2 * BLK
DCH = 16
PROJ_W = 1920
Q0, K0, V0, U0, Z0, XBC0, DT0 = 0, 512, 640, 768, 1024, 1280, 1792
FFN_CHUNK = 256
SEG = BLK // 8
SC_ABAR_RE, SC_ABAR_IM, SC_SEG1, SC_SEG2, SC_SEG4, SC_CARRY, SC_ROWS = 0, 8, 16, 32, 48, 64, 80
NEG_INF = float("-inf")
MIB = 1024 * 1024


def _dot(a, b):
    return jnp.dot(a, b, preferred_element_type=F32)


def _dot_nt(a, b):
    return lax.dot_general(a, b, (((1,), (1,)), ((), ())), preferred_element_type=F32)


def _sigmoid(x):
    return 1.0 / (1.0 + jnp.exp(-x))


def _silu(x):
    return x * _sigmoid(x)


def _softplus(x):
    return jnp.maximum(x, 0.0) + jnp.log1p(jnp.exp(-jnp.abs(x)))


def _gelu_tanh(x):
    return x * _sigmoid((2.0 * 0.7978845608028654) * (x + 0.044715 * (x * x * x)))


def _rmsnorm(x, g):
    return x * lax.rsqrt(jnp.mean(x * x, axis=-1, keepdims=True) + NORM_EPS) * g


def _iota(shape, dim):
    return lax.broadcasted_iota(jnp.int32, shape, dim)


def _rope(x, cosv, sinv):
    first = (_iota(x.shape, 1) & 63) < 32
    partner = jnp.where(first, pltpu.roll(x, 96, 1), pltpu.roll(x, 32, 1))
    return x * cosv + partner * sinv


def _lane_bcast(x, widths):
    rows = x.shape[0]
    return jnp.concatenate(
        [jnp.broadcast_to(x[:, h:h + 1], (rows, w)) for h, w in enumerate(widths)], axis=1)


def _split3(x):
    hi = x.astype(BF16)
    r1 = x - hi.astype(F32)
    mid = r1.astype(BF16)
    lo = (r1 - mid.astype(F32)).astype(BF16)
    return hi, mid, lo


def _s5_prep_kernel(are_ref, aim_ref, ldt_ref, btre_ref, btim_ref, sc_ref, bbre_ref, bbim_ref):
    ar = are_ref[0]
    ai = aim_ref[0]
    dt = jnp.exp(ldt_ref[0])

    def power(k):
        kf = k.astype(F32)
        mag = jnp.exp((kf * dt) * ar)
        ang = (kf * dt) * ai
        return mag * jnp.cos(ang), mag * jnp.sin(ang)

    pre, pim = power(jnp.full((8, S5_LANES), 1, jnp.int32))
    sc_ref[0, SC_ABAR_RE:SC_ABAR_RE + 8, :] = pre
    sc_ref[0, SC_ABAR_IM:SC_ABAR_IM + 8, :] = pim
    row = _iota((8, S5_LANES), 0)
    for d, base in ((1, SC_SEG1), (2, SC_SEG2), (4, SC_SEG4)):
        sre, sim = power(jnp.full((8, S5_LANES), SEG * d, jnp.int32))
        sc_ref[0, base:base + 8, :] = jnp.where(row >= d, sre, 0.0)
        sc_ref[0, base + 8:base + 16, :] = jnp.where(row >= d, sim, 0.0)
    qre, qim = power(SEG * (row + 1))
    sc_ref[0, SC_CARRY:SC_CARRY + 8, :] = qre
    sc_ref[0, SC_CARRY + 8:SC_CARRY + 16, :] = qim
    abar_re = pre[0:1]
    abar_im = pim[0:1]
    den = ar * ar + ai * ai
    xr = abar_re - 1.0
    f_re = (xr * ar + abar_im * ai) / den
    f_im = (abar_im * ar - xr * ai) / den
    br = btre_ref[0]
    bi = btim_ref[0]
    bbre_ref[0] = f_re * br - f_im * bi
    bbim_ref[0] = f_re * bi + f_im * br


def _s5_prepare(a_re, a_im, log_dt, b_re, b_im):
    depth = a_re.shape[0]
    flat = lambda t: t.reshape(depth, 1, S5_LANES)
    ldt = jnp.repeat(log_dt, S5_STATE, axis=-1).reshape(depth, 1, S5_LANES)
    bt = lambda t: jnp.transpose(t, (0, 3, 1, 2)).reshape(depth, S5_GROUP_CH, S5_LANES)
    vec = pl.BlockSpec((1, 1, S5_LANES), lambda l: (l, 0, 0))
    mat = pl.BlockSpec((1, S5_GROUP_CH, S5_LANES), lambda l: (l, 0, 0))
    return pl.pallas_call(
        _s5_prep_kernel,
        grid=(depth,),
        in_specs=[vec, vec, vec, mat, mat],
        out_specs=[pl.BlockSpec((1, SC_ROWS, S5_LANES), lambda l: (l, 0, 0)), mat, mat],
        out_shape=[jax.ShapeDtypeStruct((depth, SC_ROWS, S5_LANES), F32),
                   jax.ShapeDtypeStruct((depth, S5_GROUP_CH, S5_LANES), F32),
                   jax.ShapeDtypeStruct((depth, S5_GROUP_CH, S5_LANES), F32)],
        name="s5_prep",
    )(flat(a_re), flat(a_im), ldt, bt(b_re), bt(b_im))


def _s5_block_diag_in(bb):
    depth = bb.shape[0]
    same = (jnp.arange(S5_GROUPS)[:, None, None] ==
            (jnp.arange(S5_LANES) // S5_STATE)[None, None, :])
    out = jnp.where(same[None], bb[:, None, :, :], 0.0)
    return out.reshape(depth, S5_WIDTH, S5_LANES).astype(BF16)


def _s5_block_diag_out(c):
    depth = c.shape[0]
    ct = jnp.transpose(c, (0, 1, 3, 2))
    same = jnp.eye(S5_GROUPS, dtype=bool)[None, :, None, :, None]
    out = jnp.where(same, ct[:, :, :, None, :], 0.0)
    return out.reshape(depth, S5_LANES, S5_WIDTH).astype(BF16)


def _inproj_tile(x, g_ref, w_ref):
    h = _rmsnorm(x, g_ref[...])
    row = _iota((TM, 1), 0)
    first_tile = pl.program_id(0) == 0
    front = jnp.where(first_tile, FRONT, 0)
    tail = jnp.where(first_tile, 3 * BLK, TM)
    pad = (row < front) | ((row >= BLK) & (row < BLK + front)) | (row >= tail)
    h = jnp.where(pad, 0.0, h)
    return _dot(h.astype(BF16), w_ref[...])


def _tile_rows(special_ref, main_ref):
    return jnp.where(pl.program_id(0) == 0, special_ref[...], main_ref[...])


def _inproj_kernel(xs_ref, xm_ref, g_ref, w_ref, o_ref):
    o_ref[...] = _inproj_tile(_tile_rows(xs_ref, xm_ref), g_ref, w_ref)


def _layer_resident(layer, shape):
    return pl.BlockSpec((None,) + shape, lambda i: (layer,) + (0,) * len(shape),
                        pipeline_mode=pl.Buffered(1))


def _special_spec():
    return pl.BlockSpec((TM, D_MODEL), lambda i: (0, 0), pipeline_mode=pl.Buffered(1))


def _main_spec():
    return pl.BlockSpec((TM, D_MODEL), lambda i: (jnp.maximum(i - 1, 0), 0))


def _inproj(x_special, x_main, g, w, layer):
    rows = SPECIAL + x_main.shape[0]
    resident = functools.partial(_layer_resident, layer)
    return pl.pallas_call(
        _inproj_kernel,
        grid=(rows // TM,),
        in_specs=[_special_spec(), _main_spec(),
                  resident((1, D_MODEL)),
                  resident((D_MODEL, PROJ_W))],
        out_specs=pl.BlockSpec((TM, PROJ_W), lambda i: (i, 0)),
        out_shape=jax.ShapeDtypeStruct((rows, PROJ_W), F32),
        compiler_params=pltpu.CompilerParams(dimension_semantics=("arbitrary",),
                                             vmem_limit_bytes=32 * MIB),
        name="inproj",
    )(x_special, x_main, g, w)


def _outffn_body(mixs_ref, mixm_ref, xs_ref, xm_ref, wo_ref, g2_ref, wg_ref, wu_ref, wd_ref):
    mix = _tile_rows(mixs_ref, mixm_ref)
    x1 = _tile_rows(xs_ref, xm_ref) + _dot(mix.astype(BF16), wo_ref[...])
    h = _rmsnorm(x1, g2_ref[...]).astype(BF16)
    acc = jnp.zeros((TM, D_MODEL), F32)
    for j in range(FFN_HIDDEN // FFN_CHUNK):
        sl = slice(j * FFN_CHUNK, (j + 1) * FFN_CHUNK)
        gate = _dot(h, wg_ref[:, sl])
        up = _dot(h, wu_ref[:, sl])
        acc = acc + _dot((_silu(gate) * up).astype(BF16), wd_ref[sl, :])
    return x1 + acc


def _outffn_kernel(*refs):
    g1_ref, win_ref = refs[9:11]
    f32_slabs = refs[11:14]
    xs_out_ref, xm_out_ref, proj_ref = refs[14:17]
    bf16_slabs = refs[17:20]
    x_new = _outffn_body(*refs[:9])
    _store_tile(x_new, xs_out_ref, xm_out_ref)
    proj_ref[...] = _inproj_tile(x_new, g1_ref, win_ref)
    for src, dst in zip(f32_slabs, bf16_slabs):
        dst[...] = src[...].astype(BF16)


def _store_tile(value, special_ref, main_ref):
    main_ref[...] = value

    @pl.when(pl.program_id(0) == 0)
    def _():
        special_ref[...] = value


def _outffn_final_kernel(*refs):
    gf_ref, ysp_ref, ymain_ref = refs[-3:]
    _store_tile(_rmsnorm(_outffn_body(*refs[:-3]), gf_ref[...]), ysp_ref, ymain_ref)


def _outffn(mix_special, mix_main, x_special, x_main, wo, g2, ffn_w, layer,
            g1=None, w_in=None, next_ffn_f32=None, gf=None):
    main_rows = x_main.shape[0]
    rows = SPECIAL + main_rows
    steps = rows // TM
    resident = functools.partial(_layer_resident, layer)
    whole = lambda a: pl.BlockSpec(a.shape, lambda i: (0, 0), pipeline_mode=pl.Buffered(1))
    mix_main = mix_main.reshape(main_rows, D_MODEL)
    in_specs = [_special_spec(), _main_spec(), _special_spec(), _main_spec(),
                resident((D_MODEL, D_MODEL)), resident((1, D_MODEL))] + [whole(w) for w in ffn_w]
    out_specs = [pl.BlockSpec((TM, D_MODEL), lambda i: (0, 0)), _main_spec()]
    out_shape = [jax.ShapeDtypeStruct((SPECIAL, D_MODEL), F32),
                 jax.ShapeDtypeStruct((main_rows, D_MODEL), F32)]
    args = (mix_special, mix_main, x_special, x_main, wo, g2, *ffn_w)
    if gf is None:
        nxt = functools.partial(_layer_resident, layer + 1)
        slab_in, slab_out = [], []
        for w in next_ffn_f32:
            w_rows, w_cols = w.shape[1:]
            n = max(d for d in range(1, steps + 1) if w_rows % (16 * d) == 0)
            clamp = lambda i, n=n: jnp.minimum(i, n - 1)
            slab_in.append(pl.BlockSpec((None, w_rows // n, w_cols),
                                        lambda i, clamp=clamp: (layer + 1, clamp(i), 0)))
            slab_out.append(pl.BlockSpec((w_rows // n, w_cols), lambda i, clamp=clamp: (clamp(i), 0)))
        res = pl.pallas_call(
            _outffn_kernel, grid=(steps,),
            in_specs=in_specs + [nxt((1, D_MODEL)), nxt((D_MODEL, PROJ_W))] + slab_in,
            out_specs=out_specs + [pl.BlockSpec((TM, PROJ_W), lambda i: (i, 0))] + slab_out,
            out_shape=out_shape + [jax.ShapeDtypeStruct((rows, PROJ_W), F32)] + [
                jax.ShapeDtypeStruct(w.shape[1:], BF16) for w in next_ffn_f32],
            compiler_params=pltpu.CompilerParams(dimension_semantics=("arbitrary",),
                                                 vmem_limit_bytes=60 * MIB),
            name="outffn",
        )(*args, g1, w_in, *next_ffn_f32)
        return res[0], res[1], res[2], tuple(res[3:])
    return pl.pallas_call(
        _outffn_final_kernel, grid=(rows // TM,),
        in_specs=in_specs + [pl.BlockSpec((1, D_MODEL), lambda i: (0, 0))],
        out_specs=out_specs, out_shape=out_shape,
        compiler_params=pltpu.CompilerParams(dimension_semantics=("arbitrary",),
                                             vmem_limit_bytes=52 * MIB),
        name="outffn_final",
    )(*args, gf)


def _s5_tail(y, u, s5d_ref, gluw_ref, glub_ref, s5g_ref):
    y5 = _gelu_tanh(y + s5d_ref[...] * u)
    gate = _dot(y5.astype(BF16), gluw_ref[...]) + glub_ref[...]
    return _rmsnorm(y5 * _sigmoid(gate), s5g_ref[...])


def _pair_block_diag(t):
    left = _iota(t.shape, 1) < 64
    return jnp.concatenate([jnp.where(left, t, 0.0), jnp.where(left, 0.0, t)], axis=0)


N_MIXER_PARAMS = 18


def _prompt_mixer_kernel(*refs, layer, batch):
    sinks_all = refs[0]
    proj_refs = refs[1:1 + batch]
    shared = refs[1 + batch:1 + batch + N_MIXER_PARAMS]
    n_in = 1 + batch + N_MIXER_PARAMS
    mixs_ref, mixm_ref = refs[n_in:n_in + 2]
    outs = refs[n_in + 2:n_in + 8]
    scratch = refs[n_in + 8:]
    c = pl.program_id(0)

    @pl.when(c == 0)
    def _():
        for s in scratch:
            s[...] = jnp.zeros_like(s)

    live = [mixer(c, sinks_all.at[layer], proj_refs[b], *shared, mixm_ref.at[b],
                  *(o.at[b] for o in outs), *(s.at[b] for s in scratch))
            for b in range(batch) for mixer in (_prompt_attention, _prompt_ssd, _prompt_s5)]
    while live:
        live = [gen for gen in live if next(gen, "done") != "done"]

    @pl.when(c == 0)
    def _():
        mixs_ref[...] = mixm_ref[...]


def _prompt_attention(
        c, sinks_ref, proj_ref, cos_ref, sin_ref, attg_ref,
        bre_ref, bim_ref, cre_ref, cim_ref, sc_ref, s5d_ref, gluw_ref, glub_ref, s5g_ref,
        convw_ref, convb_ref, dtb_ref, alog_ref, dssd_ref, ssdg_ref,
        mix_ref, kout_ref, vout_ref, s5re_ref, s5im_ref, convout_ref, ssdout_ref,
        kprev, vprev, hre, him, carry_re, carry_im, xbuf, hssd, rows_scr):
    cosv = cos_ref[...]
    sinv = sin_ref[...]
    k_rot = _rope(proj_ref[:, K0:K0 + KV_WIDTH], cosv, sinv)
    vt_new = proj_ref[:, V0:V0 + KV_WIDTH].T
    kout_ref[...] = k_rot.T
    vout_ref[...] = vt_new
    kk = jnp.concatenate([kprev[...], k_rot], axis=0).astype(BF16)
    vvt = jnp.concatenate([vprev[...], vt_new], axis=1).astype(BF16)
    kprev[...] = k_rot
    vprev[...] = vt_new

    qts = [(_rope(proj_ref[:, Q0 + j * 128:Q0 + (j + 1) * 128], cosv, sinv) * (HEAD_DIM ** -0.5)).T
           for j in range(ATT_HEADS // 2)]
    krow = _iota((2 * BLK, BLK), 0)
    qcol = _iota((2 * BLK, BLK), 1)
    kpos = (c - 1) * BLK + krow - FRONT
    ok = (krow >= qcol) & (krow <= qcol + BLK) & (kpos >= 0)
    bias1 = jnp.where(ok, 0.0, NEG_INF)
    bias = jnp.concatenate([bias1] * 4, axis=1)
    zero_half = jnp.zeros((HEAD_DIM, BLK), F32)
    yield

    o_rows = []
    for g in range(2):
        blocks = []
        for r in range(4):
            h = 4 * g + r
            qh = qts[h // 2][(h % 2) * HEAD_DIM:(h % 2 + 1) * HEAD_DIM, :]
            blocks.append(jnp.concatenate([qh, zero_half] if g == 0 else [zero_half, qh], axis=0))
        qg = jnp.concatenate(blocks, axis=1).astype(BF16)
        s = _dot(kk, qg) + bias
        sink = jnp.concatenate(
            [jnp.full((1, BLK), sinks_ref[4 * g + r], F32) for r in range(4)], axis=1)
        yield
        m = jnp.maximum(jnp.max(s, axis=0, keepdims=True), sink)
        p = jnp.exp(s - m)
        yield
        inv_den = 1.0 / (jnp.sum(p, axis=0, keepdims=True) + jnp.exp(sink - m))
        og = _dot(vvt, p.astype(BF16))
        for r in range(4):
            cols = slice(r * BLK, (r + 1) * BLK)
            o_rows.append(og[g * HEAD_DIM:(g + 1) * HEAD_DIM, cols] * inv_den[:, cols])
        yield
    o_att = jnp.concatenate(o_rows, axis=0).T
    mix_ref[:, 0:ATT_WIDTH] = _rmsnorm(o_att, attg_ref[...])


def _prompt_s5(
        c, sinks_ref, proj_ref, cos_ref, sin_ref, attg_ref,
        bre_ref, bim_ref, cre_ref, cim_ref, sc_ref, s5d_ref, gluw_ref, glub_ref, s5g_ref,
        convw_ref, convb_ref, dtb_ref, alog_ref, dssd_ref, ssdg_ref,
        mix_ref, kout_ref, vout_ref, s5re_ref, s5im_ref, convout_ref, ssdout_ref,
        kprev, vprev, hre, him, carry_re, carry_im, xbuf, hssd, rows_scr):
    for j in range(S5_WIDTH // 128):
        rows_scr[j] = proj_ref[:, U0 + j * 128:U0 + (j + 1) * 128]
    u = jnp.concatenate(
        [jnp.concatenate([rows_scr[j, pl.ds(t, 8, stride=SEG), :] for j in range(S5_WIDTH // 128)], axis=1)
         for t in range(SEG)], axis=0)
    ub = u.astype(BF16)
    hre[...] = _dot(ub, bre_ref[...])
    him[...] = _dot(ub, bim_ref[...])
    yield
    first_seg = _iota((8, 128), 0) == 0
    for lt in range(S5_LANES // 128):
        ls = slice(lt * 128, (lt + 1) * 128)
        a_r = sc_ref[SC_ABAR_RE:SC_ABAR_RE + 8, ls]
        a_i = sc_ref[SC_ABAR_IM:SC_ABAR_IM + 8, ls]
        er = hre[0:8, ls]
        ei = him[0:8, ls]
        for t in range(1, SEG):
            rs = slice(8 * t, 8 * t + 8)
            er, ei = hre[rs, ls] + a_r * er - a_i * ei, him[rs, ls] + a_r * ei + a_i * er
        yield
        for d, b0 in ((1, SC_SEG1), (2, SC_SEG2), (4, SC_SEG4)):
            s_r = sc_ref[b0:b0 + 8, ls]
            s_i = sc_ref[b0 + 8:b0 + 16, ls]
            pr = pltpu.roll(er, d, 0)
            pi = pltpu.roll(ei, d, 0)
            er, ei = er + s_r * pr - s_i * pi, ei + s_r * pi + s_i * pr
        cr = carry_re[:, ls]
        ci = carry_im[:, ls]
        q_r = sc_ref[SC_CARRY:SC_CARRY + 8, ls]
        q_i = sc_ref[SC_CARRY + 8:SC_CARRY + 16, ls]
        tr = er + q_r * cr - q_i * ci
        ti = ei + q_r * ci + q_i * cr
        hr = jnp.where(first_seg, cr, pltpu.roll(tr, 1, 0))
        hi = jnp.where(first_seg, ci, pltpu.roll(ti, 1, 0))
        carry_re[:, ls] = jnp.broadcast_to(tr[7:8], (8, 128))
        carry_im[:, ls] = jnp.broadcast_to(ti[7:8], (8, 128))
        for t in range(SEG):
            rs = slice(8 * t, 8 * t + 8)
            hr, hi = hre[rs, ls] + a_r * hr - a_i * hi, him[rs, ls] + a_r * hi + a_i * hr
            hre[rs, ls] = hr
            him[rs, ls] = hi
        yield
    s5re_ref[...] = carry_re[0:1, :]
    s5im_ref[...] = carry_im[0:1, :]
    y_perm = _dot(hre[...].astype(BF16), cre_ref[...]) - _dot(him[...].astype(BF16), cim_ref[...])
    yield
    o_perm = _s5_tail(y_perm, u, s5d_ref, gluw_ref, glub_ref, s5g_ref)
    for j in range(S5_WIDTH // 128):
        for t in range(SEG):
            rows_scr[j, pl.ds(t, 8, stride=SEG), :] = o_perm[8 * t:8 * t + 8, j * 128:(j + 1) * 128]
        mix_ref[:, ATT_WIDTH + j * 128:ATT_WIDTH + (j + 1) * 128] = rows_scr[j]


def _prompt_ssd(
        c, sinks_ref, proj_ref, cos_ref, sin_ref, attg_ref,
        bre_ref, bim_ref, cre_ref, cim_ref, sc_ref, s5d_ref, gluw_ref, glub_ref, s5g_ref,
        convw_ref, convb_ref, dtb_ref, alog_ref, dssd_ref, ssdg_ref,
        mix_ref, kout_ref, vout_ref, s5re_ref, s5im_ref, convout_ref, ssdout_ref,
        kprev, vprev, hre, him, carry_re, carry_im, xbuf, hssd, rows_scr):
    xbc = proj_ref[:, XBC0:XBC0 + SSD_CONV_DIM]
    xbuf[8:8 + BLK, :] = xbc
    conv = convb_ref[...]
    for j in range(SSD_CONV):
        conv = conv + xbuf[5 + j:5 + j + BLK, :] * convw_ref[j:j + 1, :]
    yield
    xc = _silu(conv)
    convout_ref[...] = xbc[BLK - (SSD_CONV - 1):BLK, :]
    xbuf[0:8, :] = xbc[BLK - 8:BLK, :]
    xs = xc[:, 0:SSD_WIDTH]
    bm = xc[:, SSD_WIDTH:SSD_WIDTH + 128]
    cm = xc[:, SSD_WIDTH + 128:SSD_WIDTH + 256]
    yield

    lane = _iota((BLK, 128), 1)
    row = _iota((BLK, 128), 0)
    head_r = _iota((8, BLK), 0)
    time_c = _iota((8, BLK), 1)
    live = (head_r < SSD_HEADS) & (time_c >= jnp.where(c == 0, FRONT, 0))
    raw_t = proj_ref[:, DT0:DT0 + 128].T[0:8, :]
    dt_t = jnp.where(live, _softplus(raw_t + dtb_ref[...]), 0.0)
    dta_t = dt_t * (-jnp.exp(alog_ref[...]))
    causal = lane <= row
    triu = jnp.where(row <= lane, 1.0, 0.0).astype(BF16)
    hi3, mid3, lo3 = _split3(dta_t)
    cs_t = _dot(hi3, triu) + _dot(mid3, triu) + _dot(lo3, triu)
    yield
    cs_last = cs_t[:, BLK - 1:BLK]
    packed = jnp.concatenate(
        [dt_t, dt_t * jnp.exp(cs_last - cs_t), jnp.exp(cs_t), cs_t, jnp.zeros((BLK - 32, BLK), F32)], axis=0)
    cols = packed.T
    heads64 = (SSD_HEAD_DIM,) * SSD_HEADS
    xd = xs * _lane_bcast(cols[:, 0:SSD_HEADS], heads64)
    dxd = xs * _lane_bcast(cols[:, 8:8 + SSD_HEADS], heads64)
    ecs = _lane_bcast(cols[:, 16:16 + SSD_HEADS], heads64)
    cs = cols[:, 24:24 + SSD_HEADS]
    yield

    bmb = bm.astype(BF16)
    left = lane < 64
    cb = (_dot_nt(jnp.where(left, cm, 0.0).astype(BF16), bmb),
          _dot_nt(jnp.where(left, 0.0, cm).astype(BF16), bmb))
    scores = []
    for h in range(SSD_HEADS):
        seg = cs[:, h:h + 1] - cs_t[h:h + 1, :]
        scores.append((cb[h // 2] * jnp.exp(jnp.where(causal, seg, NEG_INF))).astype(BF16))
        yield
    y_diag = jnp.concatenate(
        [_dot(jnp.concatenate([scores[2 * j], scores[2 * j + 1]], axis=1),
              _pair_block_diag(xd[:, j * 128:(j + 1) * 128]).astype(BF16)) for j in range(2)], axis=1)
    yield

    h_prev = hssd[...]
    y_off = _dot_nt(cm.astype(BF16), h_prev.astype(BF16)) * ecs
    yield
    states = _dot(dxd.T.astype(BF16), bmb)
    own = (_iota((SSD_WIDTH, 128), 0) >> 7) == (_iota((SSD_WIDTH, 128), 1) >> 6)
    cd = jnp.exp(cs_last)
    cdm = jnp.concatenate(
        [jnp.broadcast_to(cd[h:h + 1, :], (SSD_HEAD_DIM, 128)) for h in range(SSD_HEADS)], axis=0)
    h_new = cdm * h_prev + jnp.where(own, states, 0.0)
    hssd[...] = h_new
    for h in range(SSD_HEADS):
        g0 = (h // 2) * SSD_STATE
        ssdout_ref[h] = h_new[h * SSD_HEAD_DIM:(h + 1) * SSD_HEAD_DIM, g0:g0 + SSD_STATE]
    yield

    yssd = y_diag + y_off + dssd_ref[...] * xs
    yc = yssd * _silu(proj_ref[:, Z0:Z0 + SSD_WIDTH])
    mix_ref[:, ATT_WIDTH + S5_WIDTH:] = _rmsnorm(yc, ssdg_ref[...])


def _prompt_mixers(proj, pp, tabs, batch, nc, layer):
    def blk(b):
        return lambda c: (jnp.where(c == 0, b, 4 + b * (nc - 1) + c - 1), 0)

    const = lambda shape: pl.BlockSpec((None,) + shape, lambda c: (layer,) + (0,) * len(shape))
    whole = lambda shape: pl.BlockSpec(shape, lambda c: (0,) * len(shape))
    in_specs = [pl.BlockSpec(memory_space=pltpu.SMEM)]
    in_specs += [pl.BlockSpec((BLK, PROJ_W), blk(b)) for b in range(batch)]
    in_specs += [
        pl.BlockSpec((BLK, 128), lambda c: (c, 0)),
        pl.BlockSpec((BLK, 128), lambda c: (c, 0)),
        const((1, ATT_WIDTH)),
        const((S5_WIDTH, S5_LANES)), const((S5_WIDTH, S5_LANES)),
        const((S5_LANES, S5_WIDTH)), const((S5_LANES, S5_WIDTH)),
        const((SC_ROWS, S5_LANES)), const((1, S5_WIDTH)), const((S5_WIDTH, S5_WIDTH)),
        const((1, S5_WIDTH)), const((1, S5_WIDTH)),
        const((SSD_CONV, SSD_CONV_DIM)), const((1, SSD_CONV_DIM)),
        const((8, BLK)), const((8, BLK)), const((1, SSD_WIDTH)), const((1, SSD_WIDTH)),
    ]
    state_shapes =[(BLK, KV_WIDTH), (BLK, KV_WIDTH), (1, S5_LANES), (1, S5_LANES),
                    (SSD_CONV - 1, SSD_CONV_DIM), (SSD_HEADS, SSD_HEAD_DIM, SSD_STATE)]
    out_specs = [
        pl.BlockSpec((batch, BLK, D_MODEL), lambda c: (0, 0, 0)),
        pl.BlockSpec((batch, BLK, D_MODEL), lambda c: (0, jnp.maximum(c - 1, 0), 0)),
    ] + [whole((batch,) + s) for s in state_shapes]
    out_shape = [
        jax.ShapeDtypeStruct((SPECIAL // BLK, BLK, D_MODEL), F32),
        jax.ShapeDtypeStruct((batch, (nc - 1) * BLK, D_MODEL), F32),
    ] + [jax.ShapeDtypeStruct((batch,) + s, F32) for s in state_shapes]
    scratch = [pltpu.VMEM((batch,) + s, F32) for s in (
        (BLK, KV_WIDTH), (BLK, KV_WIDTH), (BLK, S5_LANES), (BLK, S5_LANES),
        (8, S5_LANES), (8, S5_LANES), (BLK + 8, SSD_CONV_DIM), (SSD_WIDTH, 128),
        (S5_WIDTH // 128, BLK, 128))]
    res = pl.pallas_call(
        functools.partial(_prompt_mixer_kernel, layer=layer, batch=batch),
        grid=(nc,),
        in_specs=in_specs, out_specs=out_specs, out_shape=out_shape, scratch_shapes=scratch,
        compiler_params=pltpu.CompilerParams(dimension_semantics=("arbitrary",),
                                             vmem_limit_bytes=48 * MIB),
        name="prompt_mixers",
    )(pp["sinks"], *([proj] * batch), tabs["cos_p"], tabs["sin_p"], pp["attn_out_g"],
      pp["bre"], pp["bim"], pp["cre"], pp["cim"], pp["sc"], pp["s5_d"], pp["glu_w"], pp["glu_b"],
      pp["s5_out_g"], pp["conv_w"], pp["conv_b"], pp["dt_bias_c"], pp["a_log_c"], pp["ssd_d"],
      pp["ssd_norm_g"])
    return [res[0].reshape(SPECIAL, D_MODEL)] + list(res[1:])


def _decode_mixer_kernel(*refs, layer, chained):
    n_in = 27 + (6 if chained else 0)
    (sinks_all, _, proj_ref, cos_ref, sin_ref, attg_ref,
     bret_ref, bimt_ref, cret_ref, cimt_ref, abar_ref, s5d_ref, gluw_ref, glub_ref, s5g_ref,
     convw_ref, convb_ref, dtb_ref, alog_ref, dssd_ref, ssdg_ref,
     kt_ref, vt_ref, s5re_in, s5im_in, conv_in, ssd_in) = refs[:27]
    (mix_ref, ktout_ref, vtout_ref, s5re_ref, s5im_ref, convout_ref, ssdout_ref,
     oatt_scr, xs_scr, xdt_scr, bt_scr, ct_scr, dcyt_scr, yt_scr) = refs[n_in:]
    sinks_ref = sinks_all.at[layer]
    i = pl.program_id(0)
    heads64 = (SSD_HEAD_DIM,) * SSD_HEADS

    @pl.when(i == 0)
    def _():
        mix_ref[DEC:, :] = jnp.zeros((mix_ref.shape[0] - DEC, D_MODEL), F32)

        u = proj_ref[:, U0:U0 + S5_WIDTH]
        ut = u.T.astype(BF16)
        a_re = abar_ref[0]
        a_im = abar_ref[1]
        h0r = s5re_in[...]
        h0i = s5im_in[...]
        hr = _dot(bret_ref[...], ut) + a_re * h0r - a_im * h0i
        hi = _dot(bimt_ref[...], ut) + a_re * h0i + a_im * h0r
        s5re_ref[...] = hr
        s5im_ref[...] = hi
        yt = _dot(cret_ref[...], hr.astype(BF16)) - _dot(cimt_ref[...], hi.astype(BF16))
        mix_ref[0:DEC, ATT_WIDTH:ATT_WIDTH + S5_WIDTH] = _s5_tail(
            yt.T, u, s5d_ref, gluw_ref, glub_ref, s5g_ref)

        xbc = proj_ref[:, XBC0:XBC0 + SSD_CONV_DIM]
        conv = convb_ref[...]
        for j in range(SSD_CONV - 1):
            conv = conv + conv_in[j] * convw_ref[j:j + 1, :]
        conv = conv + xbc * convw_ref[SSD_CONV - 1:SSD_CONV, :]
        convout_ref[0] = conv_in[1]
        convout_ref[1] = conv_in[2]
        convout_ref[2] = xbc
        xc = _silu(conv)
        xs = xc[:, 0:SSD_WIDTH]
        dt = _softplus(proj_ref[:, DT0:DT0 + 128] + dtb_ref[...])
        decay = jnp.exp(dt * (-jnp.exp(alog_ref[...])))
        xs_scr[...] = xs
        xdt_scr[...] = (xs * _lane_bcast(dt, heads64)).T
        bt_scr[...] = xc[:, SSD_WIDTH:SSD_WIDTH + 128].T
        ct_scr[...] = xc[:, SSD_WIDTH + 128:SSD_WIDTH + 256].T
        dcyt_scr[...] = decay.T[0:8, :]

    r0 = pl.multiple_of(i * DCH, DCH)
    cosv = cos_ref[...]
    sinv = sin_ref[...]
    k_rot = _rope(proj_ref[pl.ds(r0, DCH), K0:K0 + KV_WIDTH], cosv, sinv)
    v_new = proj_ref[pl.ds(r0, DCH), V0:V0 + KV_WIDTH]
    pad_rows = jnp.zeros((BLK - DCH, KV_WIDTH), F32)
    knew_t = jnp.concatenate([k_rot, pad_rows], axis=0).T
    vnew_t = jnp.concatenate([v_new, pad_rows], axis=0).T
    last = _iota((KV_WIDTH, BLK), 1) == BLK - 1
    for b in range(DCH):
        ktout_ref[b] = jnp.where(last, knew_t[:, b:b + 1], pltpu.roll(kt_ref[b], BLK - 1, 1))
        vtout_ref[b] = jnp.where(last, vnew_t[:, b:b + 1], pltpu.roll(vt_ref[b], BLK - 1, 1))
    left = _iota((DCH, 128), 1) < 64
    qs = []
    for h in range(ATT_HEADS):
        j, e, g = h // 2, h % 2, h // 4
        qt = _rope(proj_ref[pl.ds(r0, DCH), Q0 + j * 128:Q0 + (j + 1) * 128], cosv, sinv)
        qt = qt * (HEAD_DIM ** -0.5)
        if e != g:
            qt = pltpu.roll(qt, 64, 1)
        qs.append(jnp.where(left == (g == 0), qt, 0.0))
    qx = jnp.concatenate(qs, axis=0).astype(BF16)
    kt_cat = jnp.concatenate([kt_ref[b] for b in range(DCH)], axis=1).astype(BF16)
    vt_cat = jnp.concatenate([vt_ref[b] for b in range(DCH)], axis=1).astype(BF16)
    s_old = _dot(qx, kt_cat)
    s_new = _dot_nt(qx, k_rot.astype(BF16))
    rseq = _iota((ATT_HEADS * DCH, DCH * BLK), 0) & (DCH - 1)
    same = rseq == (_iota((ATT_HEADS * DCH, DCH * BLK), 1) >> 7)
    same_new = (_iota((ATT_HEADS * DCH, DCH), 0) & (DCH - 1)) == _iota((ATT_HEADS * DCH, DCH), 1)
    s_old = jnp.where(same, s_old, NEG_INF)
    s_new = jnp.where(same_new, s_new, NEG_INF)
    sink = jnp.concatenate(
        [jnp.full((DCH, 1), sinks_ref[h], F32) for h in range(ATT_HEADS)], axis=0)
    m = jnp.maximum(jnp.maximum(jnp.max(s_old, axis=-1, keepdims=True),
                                jnp.max(s_new, axis=-1, keepdims=True)), sink)
    p_old = jnp.exp(s_old - m)
    p_new = jnp.exp(s_new - m)
    den = (jnp.sum(p_old, axis=-1, keepdims=True) + jnp.sum(p_new, axis=-1, keepdims=True)
           + jnp.exp(sink - m))
    o = (_dot_nt((p_old / den).astype(BF16), vt_cat)
         + _dot((p_new / den).astype(BF16), v_new.astype(BF16)))
    o_tiles = []
    for j in range(ATT_HEADS // 2):
        g = j // 2
        a = o[(2 * j) * DCH:(2 * j + 1) * DCH]
        bb = o[(2 * j + 1) * DCH:(2 * j + 2) * DCH]
        if g == 1:
            a = pltpu.roll(a, 64, 1)
        else:
            bb = pltpu.roll(bb, 64, 1)
        o_tiles.append(jnp.where(left, a, bb))
    oatt_scr[pl.ds(r0, DCH), :] = jnp.concatenate(o_tiles, axis=1)

    rows_per_step = SSD_WIDTH // (DEC // DCH)
    head = i // (SSD_HEAD_DIM // rows_per_step)
    g0 = pl.multiple_of((head // 2) * SSD_STATE, SSD_STATE)
    dcy = dcyt_scr[pl.ds(head, 1), :]
    btg = bt_scr[pl.ds(g0, SSD_STATE), :]
    ctg = ct_scr[pl.ds(g0, SSD_STATE), :]
    for rr in range(rows_per_step):
        row = i * rows_per_step + rr
        h_new = dcy * ssd_in[rr] + xdt_scr[pl.ds(row, 1), :] * btg
        ssdout_ref[rr] = h_new
        yt_scr[pl.ds(row, 1), :] = jnp.sum(ctg * h_new, axis=0, keepdims=True)

    @pl.when(i == DEC // DCH - 1)
    def _():
        mix_ref[0:DEC, 0:ATT_WIDTH] = _rmsnorm(oatt_scr[...], attg_ref[...])
        xs = xs_scr[...]
        yssd = yt_scr[...].T + dssd_ref[...] * xs
        yc = yssd * _silu(proj_ref[:, Z0:Z0 + SSD_WIDTH])
        mix_ref[0:DEC, ATT_WIDTH + S5_WIDTH:] = _rmsnorm(yc, ssdg_ref[...])


def _decode_mixers(mix, proj, pp, tabs, states, prev, layer):
    rows = proj.shape[0]
    depth = states[0].shape[0]
    nsteps = DEC // DCH
    chained = prev is not None
    const = lambda shape: pl.BlockSpec((None,) + shape, lambda i: (layer,) + (0,) * len(shape))
    plain = lambda shape: pl.BlockSpec(shape, lambda i: (0,) * len(shape))
    rows_per_step = SSD_WIDTH // nsteps
    state_specs = [
        pl.BlockSpec((None, DCH, KV_WIDTH, BLK), lambda i: (layer, i, 0, 0)),
        pl.BlockSpec((None, DCH, KV_WIDTH, BLK), lambda i: (layer, i, 0, 0)),
        const((S5_LANES, DEC)), const((S5_LANES, DEC)),
        const((SSD_CONV - 1, DEC, SSD_CONV_DIM)),
        pl.BlockSpec((None, rows_per_step, SSD_STATE, DEC), lambda i: (layer, i, 0, 0)),
    ]
    in_specs = [
        pl.BlockSpec(memory_space=pltpu.SMEM),
        pl.BlockSpec(memory_space=pl.ANY),
        pl.BlockSpec((DEC, PROJ_W), lambda i: (DEC_ROW0 // DEC, 0)),
        plain((1, 128)), plain((1, 128)),
        const((1, ATT_WIDTH)),
        const((S5_LANES, S5_WIDTH)), const((S5_LANES, S5_WIDTH)),
        const((S5_WIDTH, S5_LANES)), const((S5_WIDTH, S5_LANES)),
        const((2, S5_LANES, DEC)), const((1, S5_WIDTH)), const((S5_WIDTH, S5_WIDTH)),
        const((1, S5_WIDTH)), const((1, S5_WIDTH)),
        const((SSD_CONV, SSD_CONV_DIM)), const((1, SSD_CONV_DIM)),
        const((1, 128)), const((1, 128)), const((1, SSD_WIDTH)), const((1, SSD_WIDTH)),
    ] + state_specs + ([pl.BlockSpec(memory_space=pl.ANY)] * 6 if chained else [])
    out_specs = [pl.BlockSpec((2 * DEC, D_MODEL), lambda i: (DEC_ROW0 // (2 * DEC), 0))] + state_specs
    out_shape = [jax.ShapeDtypeStruct(mix.shape, F32)] + [
        jax.ShapeDtypeStruct(s.shape, F32) for s in states]
    scratch = [
        pltpu.VMEM((DEC, ATT_WIDTH), F32), pltpu.VMEM((DEC, SSD_WIDTH), F32),
        pltpu.VMEM((SSD_WIDTH, DEC), F32), pltpu.VMEM((128, DEC), F32), pltpu.VMEM((128, DEC), F32),
        pltpu.VMEM((8, DEC), F32), pltpu.VMEM((SSD_WIDTH, DEC), F32),
    ]
    aliases = {1: 0}
    if chained:
        aliases.update({27 + k: 1 + k for k in range(6)})
    assert depth > layer
    return pl.pallas_call(
        functools.partial(_decode_mixer_kernel, layer=layer, chained=chained),
        grid=(nsteps,),
        in_specs=in_specs, out_specs=out_specs, out_shape=out_shape,
        scratch_shapes=scratch,
        input_output_aliases=aliases,
        compiler_params=pltpu.CompilerParams(dimension_semantics=("arbitrary",),
                                             vmem_limit_bytes=40 * MIB),
        name="decode_mixers",
    )(pp["sinks"], mix, proj, tabs["cos_d"], tabs["sin_d"], pp["attn_out_g"],
      pp["bre_t"], pp["bim_t"], pp["cre_t"], pp["cim_t"], pp["abar_t"], pp["s5_d"], pp["glu_w"],
      pp["glu_b"], pp["s5_out_g"], pp["conv_w"], pp["conv_b"], pp["dt_bias"], pp["a_log"],
      pp["ssd_d"], pp["ssd_norm_g"], *states, *(prev if chained else ()))


def _rope_tables(nc):
    half = HEAD_DIM // 2
    inv = ROPE_THETA ** (-np.arange(half, dtype=np.float64) / half)

    def tab(pos):
        ang = pos.astype(np.float64)[:, None] * inv[None, :]
        cos = np.cos(ang).astype(np.float32)
        sin = np.sin(ang).astype(np.float32)
        return np.tile(cos, (1, 4)), np.concatenate([-sin, sin, -sin, sin], axis=1)

    cos_p, sin_p = tab(np.arange(nc * BLK, dtype=np.int32) - FRONT)
    cos_d, sin_d = tab(np.full((1,), PAST_LEN, dtype=np.int32))
    return {"cos_p": cos_p, "sin_p": sin_p, "cos_d": cos_d, "sin_d": sin_d}


def kernel(x_prompt, x_sample, cache_k, cache_v, state_s5_re, state_s5_im, state_ssd_conv, state_ssd,
           meta_tokens, ln1_g, w_in, attn_sinks, attn_out_g, s5_a_re, s5_a_im, s5_log_dt,
           s5_b_re, s5_b_im, s5_c_re, s5_c_im, s5_d, s5_glu_w, s5_glu_b, s5_out_g,
           ssd_conv_w, ssd_conv_b, ssd_dt_bias, ssd_a_log, ssd_d, ssd_norm_g, w_out,
           ln2_g, w_gate, w_up, w_down, lnf_g):
    batch, seq, _ = x_prompt.shape
    depth = w_in.shape[0]
    assert batch == 2 and x_sample.shape[0] == DEC and x_sample.shape[1] == 1
    assert seq % TM == 0 and cache_k.shape[2] == BLK
    nc = seq // BLK + 1

    zeros_front = jnp.zeros((FRONT, D_MODEL), F32)
    x_special = jnp.concatenate([zeros_front, meta_tokens, zeros_front, meta_tokens,
                                 x_sample.reshape(DEC, D_MODEL), jnp.zeros((BLK, D_MODEL), F32)], axis=0)
    x_main = x_prompt.reshape(batch * seq, D_MODEL)

    tabs = _rope_tables(nc)
    sc, bb_re, bb_im = _s5_prepare(s5_a_re, s5_a_im, s5_log_dt, s5_b_re, s5_b_im)
    bre, bim = _s5_block_diag_in(bb_re), _s5_block_diag_in(bb_im)
    cre, cim = _s5_block_diag_out(s5_c_re), _s5_block_diag_out(s5_c_im)
    w_in_p = jnp.pad(w_in, ((0, 0), (0, 0), (0, PROJ_W - N_IN))).astype(BF16)
    w_out_b = w_out.astype(BF16)
    ffn_f32 = (w_gate, w_up, w_down)
    ffn_w = tuple(t[0].astype(BF16) for t in ffn_f32)
    pad_heads = lambda t: jnp.pad(t, ((0, 0), (0, 128 - SSD_HEADS)))[:, None, :]
    head_rows = lambda t: jnp.broadcast_to(
        jnp.pad(t, ((0, 0), (0, 8 - SSD_HEADS)))[:, :, None], (depth, 8, BLK))
    row = lambda t: t[:, None, :]
    abar_t = jnp.broadcast_to(jnp.stack([sc[:, SC_ABAR_RE], sc[:, SC_ABAR_IM]], axis=1)[..., None],
                              (depth, 2, S5_LANES, DEC))
    pp = {
        "sinks": attn_sinks, "attn_out_g": row(attn_out_g),
        "bre": bre, "bim": bim, "cre": cre, "cim": cim, "sc": sc,
        "bre_t": jnp.swapaxes(bre, 1, 2), "bim_t": jnp.swapaxes(bim, 1, 2),
        "cre_t": jnp.swapaxes(cre, 1, 2), "cim_t": jnp.swapaxes(cim, 1, 2), "abar_t": abar_t,
        "s5_d": row(s5_d), "glu_w": s5_glu_w.astype(BF16), "glu_b": row(s5_glu_b),
        "s5_out_g": row(s5_out_g),
        "conv_w": ssd_conv_w, "conv_b": row(ssd_conv_b),
        "dt_bias": pad_heads(ssd_dt_bias), "a_log": pad_heads(ssd_a_log),
        "dt_bias_c": head_rows(ssd_dt_bias), "a_log_c": head_rows(ssd_a_log),
        "ssd_d": row(jnp.repeat(ssd_d, SSD_HEAD_DIM, axis=-1)), "ssd_norm_g": row(ssd_norm_g),
    }
    ln1_r, ln2_r = row(ln1_g), row(ln2_g)

    states = (
        jnp.transpose(cache_k, (0, 1, 3, 4, 2)).reshape(depth, DEC, KV_WIDTH, BLK),
        jnp.transpose(cache_v, (0, 1, 3, 4, 2)).reshape(depth, DEC, KV_WIDTH, BLK),
        jnp.transpose(state_s5_re, (0, 2, 3, 1)).reshape(depth, S5_LANES, DEC),
        jnp.transpose(state_s5_im, (0, 2, 3, 1)).reshape(depth, S5_LANES, DEC),
        jnp.transpose(state_ssd_conv, (0, 2, 1, 3)),
        jnp.transpose(state_ssd, (0, 2, 3, 4, 1)).reshape(depth, SSD_WIDTH, SSD_STATE, DEC),
    )

    outs_p = [[] for _ in range(6)]
    outs_s = None
    y_special = y_main = None
    proj = _inproj(x_special, x_main, ln1_r, w_in_p, 0)
    for l in range(depth):
        res_p = _prompt_mixers(proj, pp, tabs, batch, nc, l)
        res_s = _decode_mixers(res_p[0], proj, pp, tabs, states, outs_s, l)
        outs_s = res_s[1:]
        for i in range(6):
            outs_p[i].append(res_p[i + 2])
        ffn_args = (res_s[0], res_p[1], x_special, x_main, w_out_b, ln2_r, ffn_w, l)
        if l + 1 < depth:
            x_special, x_main, proj, ffn_w = _outffn(*ffn_args, g1=ln1_r, w_in=w_in_p,
                                                     next_ffn_f32=ffn_f32)
        else:
            y_special, y_main = _outffn(*ffn_args, gf=lnf_g[None, :])

    y_prompt = y_main.reshape(batch, seq, D_MODEL)
    y_sample = y_special[DEC_ROW0:DEC_ROW0 + DEC].reshape(DEC, 1, D_MODEL)
    kv_p = lambda ts: jnp.transpose(
        jnp.stack(ts).reshape(depth, batch, 2, HEAD_DIM, BLK), (0, 1, 4, 2, 3))
    s5_p = lambda ts: jnp.stack(ts).reshape(depth, batch, S5_GROUPS, S5_STATE)
    kt_s, vt_s, s5re_s, s5im_s, conv_s, ssd_s = outs_s
    kv_s = lambda t: jnp.transpose(t.reshape(depth, DEC, 2, HEAD_DIM, BLK), (0, 1, 4, 2, 3))
    s5_s = lambda t: jnp.transpose(t.reshape(depth, S5_GROUPS, S5_STATE, DEC), (0, 3, 1, 2))
    return (y_prompt, y_sample,
            kv_p(outs_p[0]), kv_p(outs_p[1]), s5_p(outs_p[2]), s5_p(outs_p[3]),
            jnp.stack(outs_p[4]), jnp.stack(outs_p[5]),
            kv_s(kt_s), kv_s(vt_s), s5_s(s5re_s), s5_s(s5im_s),
            jnp.transpose(conv_s, (0, 2, 1, 3)),
            jnp.transpose(ssd_s.reshape(depth, SSD_HEADS, SSD_HEAD_DIM, SSD_STATE, DEC),
                          (0, 4, 1, 2, 3)))
```

```python
import functools

import jax
import jax.numpy as jnp
import numpy as np
from jax import lax
from jax.experimental import pallas as pl
from jax.experimental.pallas import tpu as pltpu

F32 = jnp.float32
BF16 = jnp.bfloat16

D_MODEL = 1024
N_META = 16
HEAD_DIM = 64
ATT_WIDTH = 512
ATT_HEADS = 8
KV_WIDTH = 128
S5_WIDTH = 256
S5_GROUPS = 16
S5_GROUP_CH = 16
S5_STATE = 64
S5_LANES = S5_GROUPS * S5_STATE
SSD_WIDTH = 256
SSD_HEADS = 4
SSD_HEAD_DIM = 64
SSD_STATE = 64
SSD_CONV = 4
SSD_CONV_DIM = 512
FFN_HIDDEN = 2816
NORM_EPS = 1e-6
ROPE_THETA = 10000.0
PAST_LEN = 8192
N_IN = 1796

BLK = 128
FRONT = BLK - N_META
TM = 512
SPECIAL = 4 * BLK
DEC = 128
DEC_ROW0 = 2 * BLK
DCH = 16
PROJ_W = 1920
Q0, K0, V0, U0, Z0, XBC0, DT0 = 0, 512, 640, 768, 1024, 1280, 1792
FFN_CHUNK = 256
SEG = BLK // 8
SC_ABAR_RE, SC_ABAR_IM, SC_SEG1, SC_SEG2, SC_SEG4, SC_CARRY, SC_ROWS = 0, 8, 16, 32, 48, 64, 80
NEG_INF = float("-inf")
MIB = 1024 * 1024


def _dot(a, b):
    return jnp.dot(a, b, preferred_element_type=F32)


def _dot_nt(a, b):
    return lax.dot_general(a, b, (((1,), (1,)), ((), ())), preferred_element_type=F32)


def _sigmoid(x):
    return 1.0 / (1.0 + jnp.exp(-x))


def _silu(x):
    return x * _sigmoid(x)


def _softplus(x):
    return jnp.maximum(x, 0.0) + jnp.log1p(jnp.exp(-jnp.abs(x)))


def _gelu_tanh(x):
    return x * _sigmoid((2.0 * 0.7978845608028654) * (x + 0.044715 * (x * x * x)))


def _rmsnorm(x, g):
    return x * lax.rsqrt(jnp.mean(x * x, axis=-1, keepdims=True) + NORM_EPS) * g


def _iota(shape, dim):
    return lax.broadcasted_iota(jnp.int32, shape, dim)


def _rope(x, cosv, sinv):
    first = (_iota(x.shape, 1) & 63) < 32
    partner = jnp.where(first, pltpu.roll(x, 96, 1), pltpu.roll(x, 32, 1))
    return x * cosv + partner * sinv


def _lane_bcast(x, widths):
    rows = x.shape[0]
    return jnp.concatenate(
        [jnp.broadcast_to(x[:, h:h + 1], (rows, w)) for h, w in enumerate(widths)], axis=1)


def _split3(x):
    hi = x.astype(BF16)
    r1 = x - hi.astype(F32)
    mid = r1.astype(BF16)
    lo = (r1 - mid.astype(F32)).astype(BF16)
    return hi, mid, lo


def _s5_prep_kernel(are_ref, aim_ref, ldt_ref, btre_ref, btim_ref, sc_ref, bbre_ref, bbim_ref):
    ar = are_ref[0]
    ai = aim_ref[0]
    dt = jnp.exp(ldt_ref[0])

    def power(k):
        kf = k.astype(F32)
        mag = jnp.exp((kf * dt) * ar)
        ang = (kf * dt) * ai
        return mag * jnp.cos(ang), mag * jnp.sin(ang)

    pre, pim = power(jnp.full((8, S5_LANES), 1, jnp.int32))
    sc_ref[0, SC_ABAR_RE:SC_ABAR_RE + 8, :] = pre
    sc_ref[0, SC_ABAR_IM:SC_ABAR_IM + 8, :] = pim
    row = _iota((8, S5_LANES), 0)
    for d, base in ((1, SC_SEG1), (2, SC_SEG2), (4, SC_SEG4)):
        sre, sim = power(jnp.full((8, S5_LANES), SEG * d, jnp.int32))
        sc_ref[0, base:base + 8, :] = jnp.where(row >= d, sre, 0.0)
        sc_ref[0, base + 8:base + 16, :] = jnp.where(row >= d, sim, 0.0)
    qre, qim = power(SEG * (row + 1))
    sc_ref[0, SC_CARRY:SC_CARRY + 8, :] = qre
    sc_ref[0, SC_CARRY + 8:SC_CARRY + 16, :] = qim
    abar_re = pre[0:1]
    abar_im = pim[0:1]
    den = ar * ar + ai * ai
    xr = abar_re - 1.0
    f_re = (xr * ar + abar_im * ai) / den
    f_im = (abar_im * ar - xr * ai) / den
    br = btre_ref[0]
    bi = btim_ref[0]
    bbre_ref[0] = f_re * br - f_im * bi
    bbim_ref[0] = f_re * bi + f_im * br


def _s5_prepare(a_re, a_im, log_dt, b_re, b_im):
    depth = a_re.shape[0]
    flat = lambda t: t.reshape(depth, 1, S5_LANES)
    ldt = jnp.repeat(log_dt, S5_STATE, axis=-1).reshape(depth, 1, S5_LANES)
    bt = lambda t: jnp.transpose(t, (0, 3, 1, 2)).reshape(depth, S5_GROUP_CH, S5_LANES)
    vec = pl.BlockSpec((1, 1, S5_LANES), lambda l: (l, 0, 0))
    mat = pl.BlockSpec((1, S5_GROUP_CH, S5_LANES), lambda l: (l, 0, 0))
    return pl.pallas_call(
        _s5_prep_kernel,
        grid=(depth,),
        in_specs=[vec, vec, vec, mat, mat],
        out_specs=[pl.BlockSpec((1, SC_ROWS, S5_LANES), lambda l: (l, 0, 0)), mat, mat],
        out_shape=[jax.ShapeDtypeStruct((depth, SC_ROWS, S5_LANES), F32),
                   jax.ShapeDtypeStruct((depth, S5_GROUP_CH, S5_LANES), F32),
                   jax.ShapeDtypeStruct((depth, S5_GROUP_CH, S5_LANES), F32)],
        name="s5_prep",
    )(flat(a_re), flat(a_im), ldt, bt(b_re), bt(b_im))


def _s5_block_diag_in(bb):
    depth = bb.shape[0]
    same = (jnp.arange(S5_GROUPS)[:, None, None] ==
            (jnp.arange(S5_LANES) // S5_STATE)[None, None, :])
    out = jnp.where(same[None], bb[:, None, :, :], 0.0)
    return out.reshape(depth, S5_WIDTH, S5_LANES).astype(BF16)


def _s5_block_diag_out(c):
    depth = c.shape[0]
    ct = jnp.transpose(c, (0, 1, 3, 2))
    same = jnp.eye(S5_GROUPS, dtype=bool)[None, :, None, :, None]
    out = jnp.where(same, ct[:, :, :, None, :], 0.0)
    return out.reshape(depth, S5_LANES, S5_WIDTH).astype(BF16)


def _normed_rows(x, g_ref, special_tile):
    row = _iota((TM, 1), 0)
    front = jnp.where(special_tile, FRONT, 0)
    tail = jnp.where(special_tile, 3 * BLK, TM)
    pad = (row < front) | ((row >= BLK) & (row < BLK + front)) | (row >= tail)
    return jnp.where(pad, 0.0, _rmsnorm(x, g_ref[...])).astype(BF16)


def _inproj_tile(x, g_ref, w_ref):
    return _dot(_normed_rows(x, g_ref, pl.program_id(0) == 0), w_ref[...])


def _run_interleaved(*gens):
    live = list(gens)
    while live:
        live = [g for g in live if next(g, "done") != "done"]


def _tile_rows(special_ref, main_ref):
    return jnp.where(pl.program_id(0) == 0, special_ref[...], main_ref[...])


def _inproj_kernel(xs_ref, xm_ref, g_ref, w_ref, o_ref):
    o_ref[...] = _inproj_tile(_tile_rows(xs_ref, xm_ref), g_ref, w_ref)


def _layer_resident(layer, shape):
    return pl.BlockSpec((None,) + shape, lambda i: (layer,) + (0,) * len(shape),
                        pipeline_mode=pl.Buffered(1))


def _special_spec():
    return pl.BlockSpec((TM, D_MODEL), lambda i: (0, 0), pipeline_mode=pl.Buffered(1))


def _main_spec():
    return pl.BlockSpec((TM, D_MODEL), lambda i: (jnp.maximum(i - 1, 0), 0))


def _inproj(x_special, x_main, g, w, layer):
    rows = SPECIAL + x_main.shape[0]
    resident = functools.partial(_layer_resident, layer)
    return pl.pallas_call(
        _inproj_kernel,
        grid=(rows // TM,),
        in_specs=[_special_spec(), _main_spec(),
                  resident((1, D_MODEL)),
                  resident((D_MODEL, PROJ_W))],
        out_specs=pl.BlockSpec((TM, PROJ_W), lambda i: (i, 0)),
        out_shape=jax.ShapeDtypeStruct((rows, PROJ_W), F32),
        compiler_params=pltpu.CompilerParams(dimension_semantics=("arbitrary",),
                                             vmem_limit_bytes=32 * MIB),
        name="inproj",
    )(x_special, x_main, g, w)


def _outffn_kernel(*refs, final):
    mixs_ref, mixm_ref, xs_ref, xm_ref, wo_ref, g2_ref, wg_ref, wu_ref, wd_ref = refs[:9]
    x1 = _tile_rows(xs_ref, xm_ref) + _dot(_tile_rows(mixs_ref, mixm_ref), wo_ref[...])
    h = _rmsnorm(x1, g2_ref[...]).astype(BF16)
    acc = jnp.zeros((TM, D_MODEL), F32)
    for j in range(FFN_HIDDEN // FFN_CHUNK):
        sl = slice(j * FFN_CHUNK, (j + 1) * FFN_CHUNK)
        gate = _dot(h, wg_ref[:, sl])
        up = _dot(h, wu_ref[:, sl])
        acc = acc + _dot((_silu(gate) * up).astype(BF16), wd_ref[sl, :])
    out = x1 + acc
    special_tile = pl.program_id(0) == 0
    if final:
        gf_ref, outs_ref, outm_ref = refs[9:]
        out = _rmsnorm(out, gf_ref[...])
    else:
        g1_ref, win_ref = refs[9:11]
        f32_slabs = refs[11:14]
        outs_ref, outm_ref, proj_ref = refs[14:17]
        bf16_slabs = refs[17:20]
        proj_ref[...] = _dot(_normed_rows(out, g1_ref, special_tile), win_ref[...])
        for src, dst in zip(f32_slabs, bf16_slabs):
            dst[...] = src[...].astype(BF16)
    outm_ref[...] = out

    @pl.when(special_tile)
    def _():
        outs_ref[...] = out


def _outffn(mix_special, mix_main, x_special, x_main, wo, g2, ffn_w, layer,
            g1=None, w_in=None, next_ffn_f32=None, gf=None):
    main_rows = x_main.shape[0]
    rows = SPECIAL + main_rows
    tiles = rows // TM
    resident = functools.partial(_layer_resident, layer)
    whole = lambda a: pl.BlockSpec(a.shape, lambda i: (0, 0), pipeline_mode=pl.Buffered(1))
    mix_main = mix_main.reshape(main_rows, D_MODEL)

    in_specs = [_special_spec(), _main_spec(), _special_spec(), _main_spec(),
                resident((D_MODEL, D_MODEL)), resident((1, D_MODEL))] + [whole(w) for w in ffn_w]
    out_specs = [pl.BlockSpec((TM, D_MODEL), lambda i: (0, 0)), _main_spec()]
    out_shape = [jax.ShapeDtypeStruct((SPECIAL, D_MODEL), F32),
                 jax.ShapeDtypeStruct((main_rows, D_MODEL), F32)]
    args = (mix_special, mix_main, x_special, x_main, wo, g2, *ffn_w)
    if gf is None:
        steps = tiles
        nxt = functools.partial(_layer_resident, layer + 1)
        slab_in, slab_out = [], []
        for w in next_ffn_f32:
            w_rows, w_cols = w.shape[1:]
            n = max(d for d in range(1, steps + 1) if w_rows % (16 * d) == 0)
            clamp = lambda i, n=n: jnp.minimum(i, n - 1)
            slab_in.append(pl.BlockSpec((None, w_rows // n, w_cols),
                                        lambda i, clamp=clamp: (layer + 1, clamp(i), 0)))
            slab_out.append(pl.BlockSpec((w_rows // n, w_cols), lambda i, clamp=clamp: (clamp(i), 0)))
        proj_spec = pl.BlockSpec((TM, PROJ_W), lambda i: (i, 0))
        res = pl.pallas_call(
            functools.partial(_outffn_kernel, final=False), grid=(steps,),
            in_specs=in_specs + [nxt((1, D_MODEL)), nxt((D_MODEL, PROJ_W))] + slab_in,
            out_specs=out_specs + [proj_spec] + slab_out,
            out_shape=out_shape + [jax.ShapeDtypeStruct((rows, PROJ_W), F32)] + [
                jax.ShapeDtypeStruct(w.shape[1:], BF16) for w in next_ffn_f32],
            compiler_params=pltpu.CompilerParams(dimension_semantics=("arbitrary",),
                                                 vmem_limit_bytes=60 * MIB),
            name="outffn",
        )(*args, g1, w_in, *next_ffn_f32)
        return res[0], res[1], res[2], tuple(res[3:])
    return pl.pallas_call(
        functools.partial(_outffn_kernel, final=True), grid=(tiles,),
        in_specs=in_specs + [pl.BlockSpec((1, D_MODEL), lambda i: (0, 0))],
        out_specs=out_specs, out_shape=out_shape,
        compiler_params=pltpu.CompilerParams(dimension_semantics=("arbitrary",),
                                             vmem_limit_bytes=52 * MIB),
        name="outffn_final",
    )(*args, gf)


def _s5_tail(y, u, s5d_ref, gluw_ref, glub_ref, s5g_ref):
    y5 = _gelu_tanh(y + s5d_ref[...] * u)
    gate = _dot(y5.astype(BF16), gluw_ref[...]) + glub_ref[...]
    return _rmsnorm(y5 * _sigmoid(gate), s5g_ref[...])


def _pair_block_diag(t):
    left = _iota(t.shape, 1) < 64
    return jnp.concatenate([jnp.where(left, t, 0.0), jnp.where(left, 0.0, t)], axis=0)


N_MIXER_PARAMS = 18


def _prompt_mixer_kernel(*refs, layer, batch):
    sinks_all = refs[0]
    proj_refs = refs[1:1 + batch]
    shared = refs[1 + batch:1 + batch + N_MIXER_PARAMS]
    n_in = 1 + batch + N_MIXER_PARAMS
    mixs_ref, mixm_ref = refs[n_in:n_in + 2]
    outs = refs[n_in + 2:n_in + 8]
    scratch = refs[n_in + 8:]
    c = pl.program_id(0)

    @pl.when(c == 0)
    def _():
        for s in scratch:
            s[...] = jnp.zeros_like(s)

    _run_interleaved(*(
        mixer(c, sinks_all.at[layer], proj_refs[b], *shared, mixm_ref.at[b],
              *(o.at[b] for o in outs), *(s.at[b] for s in scratch))
        for b in range(batch) for mixer in (_prompt_attention, _prompt_ssd, _prompt_s5)))

    @pl.when(c == 0)
    def _():
        mixs_ref[0:batch] = mixm_ref[...]
        mixs_ref[batch:] = jnp.zeros((mixs_ref.shape[0] - batch, BLK, D_MODEL), mixs_ref.dtype)


def _prompt_attention(
        c, sinks_ref, proj_ref, cos_ref, sin_ref, attg_ref,
        bre_ref, bim_ref, cre_ref, cim_ref, sc_ref, s5d_ref, gluw_ref, glub_ref, s5g_ref,
        convw_ref, convb_ref, dtb_ref, alog_ref, dssd_ref, ssdg_ref,
        mix_ref, kout_ref, vout_ref, s5re_ref, s5im_ref, convout_ref, ssdout_ref,
        kprev, vprev, hre, him, carry_re, carry_im, xbuf, hssd, rows_scr):
    cosv = cos_ref[...]
    sinv = sin_ref[...]
    k_rot = _rope(proj_ref[:, K0:K0 + KV_WIDTH], cosv, sinv)
    vt_new = proj_ref[:, V0:V0 + KV_WIDTH].T
    kout_ref[...] = k_rot.T
    vout_ref[...] = vt_new
    kk = jnp.concatenate([kprev[...], k_rot], axis=0).astype(BF16)
    vvt = jnp.concatenate([vprev[...], vt_new], axis=1).astype(BF16)
    kprev[...] = k_rot
    vprev[...] = vt_new

    qts = [(_rope(proj_ref[:, Q0 + j * 128:Q0 + (j + 1) * 128], cosv, sinv) * (HEAD_DIM ** -0.5)).T
           for j in range(ATT_HEADS // 2)]
    krow = _iota((2 * BLK, BLK), 0)
    qcol = _iota((2 * BLK, BLK), 1)
    kpos = (c - 1) * BLK + krow - FRONT
    ok = (krow >= qcol) & (krow <= qcol + BLK) & (kpos >= 0)
    bias1 = jnp.where(ok, 0.0, NEG_INF)
    bias = jnp.concatenate([bias1] * 4, axis=1)
    zero_half = jnp.zeros((HEAD_DIM, BLK), F32)
    yield

    o_rows = []
    for g in range(2):
        blocks = []
        for r in range(4):
            h = 4 * g + r
            qh = qts[h // 2][(h % 2) * HEAD_DIM:(h % 2 + 1) * HEAD_DIM, :]
            blocks.append(jnp.concatenate([qh, zero_half] if g == 0 else [zero_half, qh], axis=0))
        qg = jnp.concatenate(blocks, axis=1).astype(BF16)
        s = _dot(kk, qg) + bias
        sink = jnp.concatenate(
            [jnp.full((1, BLK), sinks_ref[4 * g + r], F32) for r in range(4)], axis=1)
        yield
        m = jnp.maximum(jnp.max(s, axis=0, keepdims=True), sink)
        p = jnp.exp(s - m)
        yield
        inv_den = 1.0 / (jnp.sum(p, axis=0, keepdims=True) + jnp.exp(sink - m))
        og = _dot(vvt, p.astype(BF16))
        for r in range(4):
            cols = slice(r * BLK, (r + 1) * BLK)
            o_rows.append(og[g * HEAD_DIM:(g + 1) * HEAD_DIM, cols] * inv_den[:, cols])
        yield
    o_att = jnp.concatenate(o_rows, axis=0).T
    mix_ref[:, 0:ATT_WIDTH] = _rmsnorm(o_att, attg_ref[...]).astype(mix_ref.dtype)


def _prompt_s5(
        c, sinks_ref, proj_ref, cos_ref, sin_ref, attg_ref,
        bre_ref, bim_ref, cre_ref, cim_ref, sc_ref, s5d_ref, gluw_ref, glub_ref, s5g_ref,
        convw_ref, convb_ref, dtb_ref, alog_ref, dssd_ref, ssdg_ref,
        mix_ref, kout_ref, vout_ref, s5re_ref, s5im_ref, convout_ref, ssdout_ref,
        kprev, vprev, hre, him, carry_re, carry_im, xbuf, hssd, rows_scr):
    for j in range(S5_WIDTH // 128):
        rows_scr[j] = proj_ref[:, U0 + j * 128:U0 + (j + 1) * 128]
    u = jnp.concatenate(
        [jnp.concatenate([rows_scr[j, pl.ds(t, 8, stride=SEG), :] for j in range(S5_WIDTH // 128)], axis=1)
         for t in range(SEG)], axis=0)
    ub = u.astype(BF16)
    hre[...] = _dot(ub, bre_ref[...])
    him[...] = _dot(ub, bim_ref[...])
    yield
    first_seg = _iota((8, 128), 0) == 0
    for lt in range(S5_LANES // 128):
        ls = slice(lt * 128, (lt + 1) * 128)
        a_r = sc_ref[SC_ABAR_RE:SC_ABAR_RE + 8, ls]
        a_i = sc_ref[SC_ABAR_IM:SC_ABAR_IM + 8, ls]
        er = hre[0:8, ls]
        ei = him[0:8, ls]
        for t in range(1, SEG):
            rs = slice(8 * t, 8 * t + 8)
            er, ei = hre[rs, ls] + a_r * er - a_i * ei, him[rs, ls] + a_r * ei + a_i * er
        yield
        for d, b0 in ((1, SC_SEG1), (2, SC_SEG2), (4, SC_SEG4)):
            s_r = sc_ref[b0:b0 + 8, ls]
            s_i = sc_ref[b0 + 8:b0 + 16, ls]
            pr = pltpu.roll(er, d, 0)
            pi = pltpu.roll(ei, d, 0)
            er, ei = er + s_r * pr - s_i * pi, ei + s_r * pi + s_i * pr
        cr = carry_re[:, ls]
        ci = carry_im[:, ls]
        q_r = sc_ref[SC_CARRY:SC_CARRY + 8, ls]
        q_i = sc_ref[SC_CARRY + 8:SC_CARRY + 16, ls]
        tr = er + q_r * cr - q_i * ci
        ti = ei + q_r * ci + q_i * cr
        hr = jnp.where(first_seg, cr, pltpu.roll(tr, 1, 0))
        hi = jnp.where(first_seg, ci, pltpu.roll(ti, 1, 0))
        carry_re[:, ls] = jnp.broadcast_to(tr[7:8], (8, 128))
        carry_im[:, ls] = jnp.broadcast_to(ti[7:8], (8, 128))
        for t in range(SEG):
            rs = slice(8 * t, 8 * t + 8)
            hr, hi = hre[rs, ls] + a_r * hr - a_i * hi, him[rs, ls] + a_r * hi + a_i * hr
            hre[rs, ls] = hr
            him[rs, ls] = hi
        yield
    s5re_ref[...] = carry_re[0:1, :]
    s5im_ref[...] = carry_im[0:1, :]
    y_perm = _dot(hre[...].astype(BF16), cre_ref[...]) - _dot(him[...].astype(BF16), cim_ref[...])
    yield
    o_perm = _s5_tail(y_perm, u, s5d_ref, gluw_ref, glub_ref, s5g_ref)
    for j in range(S5_WIDTH // 128):
        for t in range(SEG):
            rows_scr[j, pl.ds(t, 8, stride=SEG), :] = o_perm[8 * t:8 * t + 8, j * 128:(j + 1) * 128]
        mix_ref[:, ATT_WIDTH + j * 128:ATT_WIDTH + (j + 1) * 128] = rows_scr[j].astype(mix_ref.dtype)


def _prompt_ssd(
        c, sinks_ref, proj_ref, cos_ref, sin_ref, attg_ref,
        bre_ref, bim_ref, cre_ref, cim_ref, sc_ref, s5d_ref, gluw_ref, glub_ref, s5g_ref,
        convw_ref, convb_ref, dtb_ref, alog_ref, dssd_ref, ssdg_ref,
        mix_ref, kout_ref, vout_ref, s5re_ref, s5im_ref, convout_ref, ssdout_ref,
        kprev, vprev, hre, him, carry_re, carry_im, xbuf, hssd, rows_scr):
    xbc = proj_ref[:, XBC0:XBC0 + SSD_CONV_DIM]
    xbuf[8:8 + BLK, :] = xbc
    conv = convb_ref[...]
    for j in range(SSD_CONV):
        conv = conv + xbuf[5 + j:5 + j + BLK, :] * convw_ref[j:j + 1, :]
    yield
    xc = _silu(conv)
    convout_ref[...] = xbc[BLK - (SSD_CONV - 1):BLK, :]
    xbuf[0:8, :] = xbc[BLK - 8:BLK, :]
    xs = xc[:, 0:SSD_WIDTH]
    bm = xc[:, SSD_WIDTH:SSD_WIDTH + 128]
    cm = xc[:, SSD_WIDTH + 128:SSD_WIDTH + 256]
    yield

    lane = _iota((BLK, 128), 1)
    row = _iota((BLK, 128), 0)
    head_r = _iota((8, BLK), 0)
    time_c = _iota((8, BLK), 1)
    live = (head_r < SSD_HEADS) & (time_c >= jnp.where(c == 0, FRONT, 0))
    raw_t = proj_ref[:, DT0:DT0 + 128].T[0:8, :]
    dt_t = jnp.where(live, _softplus(raw_t + dtb_ref[...]), 0.0)
    dta_t = dt_t * (-jnp.exp(alog_ref[...]))
    causal = lane <= row
    triu = jnp.where(row <= lane, 1.0, 0.0).astype(BF16)
    hi3, mid3, lo3 = _split3(dta_t)
    cs_t = _dot(hi3, triu) + _dot(mid3, triu) + _dot(lo3, triu)
    yield
    cs_last = cs_t[:, BLK - 1:BLK]
    packed = jnp.concatenate(
        [dt_t, dt_t * jnp.exp(cs_last - cs_t), jnp.exp(cs_t), cs_t, jnp.zeros((BLK - 32, BLK), F32)], axis=0)
    cols = packed.T
    heads64 = (SSD_HEAD_DIM,) * SSD_HEADS
    xd = xs * _lane_bcast(cols[:, 0:SSD_HEADS], heads64)
    dxd = xs * _lane_bcast(cols[:, 8:8 + SSD_HEADS], heads64)
    ecs = _lane_bcast(cols[:, 16:16 + SSD_HEADS], heads64)
    cs = cols[:, 24:24 + SSD_HEADS]
    yield

    bmb = bm.astype(BF16)
    left = lane < 64
    cb = (_dot_nt(jnp.where(left, cm, 0.0).astype(BF16), bmb),
          _dot_nt(jnp.where(left, 0.0, cm).astype(BF16), bmb))
    scores = []
    for h in range(SSD_HEADS):
        seg = cs[:, h:h + 1] - cs_t[h:h + 1, :]
        scores.append((cb[h // 2] * jnp.exp(jnp.where(causal, seg, NEG_INF))).astype(BF16))
        yield
    y_diag = jnp.concatenate(
        [_dot(jnp.concatenate([scores[2 * j], scores[2 * j + 1]], axis=1),
              _pair_block_diag(xd[:, j * 128:(j + 1) * 128]).astype(BF16)) for j in range(2)], axis=1)
    yield

    h_prev = hssd[...]
    y_off = _dot_nt(cm.astype(BF16), h_prev.astype(BF16)) * ecs
    yield
    states = _dot(dxd.T.astype(BF16), bmb)
    own = (_iota((SSD_WIDTH, 128), 0) >> 7) == (_iota((SSD_WIDTH, 128), 1) >> 6)
    cd = jnp.exp(cs_last)
    cdm = jnp.concatenate(
        [jnp.broadcast_to(cd[h:h + 1, :], (SSD_HEAD_DIM, 128)) for h in range(SSD_HEADS)], axis=0)
    h_new = cdm * h_prev + jnp.where(own, states, 0.0)
    hssd[...] = h_new
    for h in range(SSD_HEADS):
        g0 = (h // 2) * SSD_STATE
        ssdout_ref[h] = h_new[h * SSD_HEAD_DIM:(h + 1) * SSD_HEAD_DIM, g0:g0 + SSD_STATE]
    yield

    yssd = y_diag + y_off + dssd_ref[...] * xs
    yc = yssd * _silu(proj_ref[:, Z0:Z0 + SSD_WIDTH])
    mix_ref[:, ATT_WIDTH + S5_WIDTH:] = _rmsnorm(yc, ssdg_ref[...]).astype(mix_ref.dtype)


def _prompt_mixers(proj, pp, tabs, batch, nc, layer):
    def blk(b):
        return lambda c: (jnp.where(c == 0, b, 4 + b * (nc - 1) + c - 1), 0)

    const = lambda shape: pl.BlockSpec((None,) + shape, lambda c: (layer,) + (0,) * len(shape))
    whole = lambda shape: pl.BlockSpec(shape, lambda c: (0,) * len(shape))
    in_specs = [pl.BlockSpec(memory_space=pltpu.SMEM)]
    in_specs += [pl.BlockSpec((BLK, PROJ_W), blk(b)) for b in range(batch)]
    in_specs += [
        pl.BlockSpec((BLK, 128), lambda c: (c, 0)),
        pl.BlockSpec((BLK, 128), lambda c: (c, 0)),
        const((1, ATT_WIDTH)),
        const((S5_WIDTH, S5_LANES)), const((S5_WIDTH, S5_LANES)),
        const((S5_LANES, S5_WIDTH)), const((S5_LANES, S5_WIDTH)),
        const((SC_ROWS, S5_LANES)), const((1, S5_WIDTH)), const((S5_WIDTH, S5_WIDTH)),
        const((1, S5_WIDTH)), const((1, S5_WIDTH)),
        const((SSD_CONV, SSD_CONV_DIM)), const((1, SSD_CONV_DIM)),
        const((8, BLK)), const((8, BLK)), const((1, SSD_WIDTH)), const((1, SSD_WIDTH)),
    ]
    state_shapes =[(BLK, KV_WIDTH), (BLK, KV_WIDTH), (1, S5_LANES), (1, S5_LANES),
                    (SSD_CONV - 1, SSD_CONV_DIM), (SSD_HEADS, SSD_HEAD_DIM, SSD_STATE)]
    out_specs = [
        pl.BlockSpec((SPECIAL // BLK, BLK, D_MODEL), lambda c: (0, 0, 0)),
        pl.BlockSpec((batch, BLK, D_MODEL), lambda c: (0, jnp.maximum(c - 1, 0), 0)),
    ] + [whole((batch,) + s) for s in state_shapes]
    out_shape = [
        jax.ShapeDtypeStruct((SPECIAL // BLK, BLK, D_MODEL), BF16),
        jax.ShapeDtypeStruct((batch, (nc - 1) * BLK, D_MODEL), BF16),
    ] + [jax.ShapeDtypeStruct((batch,) + s, F32) for s in state_shapes]
    scratch = [pltpu.VMEM((batch,) + s, F32) for s in (
        (BLK, KV_WIDTH), (BLK, KV_WIDTH), (BLK, S5_LANES), (BLK, S5_LANES),
        (8, S5_LANES), (8, S5_LANES), (BLK + 8, SSD_CONV_DIM), (SSD_WIDTH, 128),
        (S5_WIDTH // 128, BLK, 128))]
    res = pl.pallas_call(
        functools.partial(_prompt_mixer_kernel, layer=layer, batch=batch),
        grid=(nc,),
        in_specs=in_specs, out_specs=out_specs, out_shape=out_shape, scratch_shapes=scratch,
        compiler_params=pltpu.CompilerParams(dimension_semantics=("arbitrary",),
                                             vmem_limit_bytes=48 * MIB),
        name="prompt_mixers",
    )(pp["sinks"], *([proj] * batch), tabs["cos_p"], tabs["sin_p"], pp["attn_out_g"],
      pp["bre"], pp["bim"], pp["cre"], pp["cim"], pp["sc"], pp["s5_d"], pp["glu_w"], pp["glu_b"],
      pp["s5_out_g"], pp["conv_w"], pp["conv_b"], pp["dt_bias_c"], pp["a_log_c"], pp["ssd_d"],
      pp["ssd_norm_g"])
    return [res[0].reshape(SPECIAL, D_MODEL)] + list(res[1:])


def _decode_mixer_kernel(*refs, layer, chained):
    n_in = 27 + (6 if chained else 0)
    (sinks_all, _, proj_ref, cos_ref, sin_ref, attg_ref,
     bret_ref, bimt_ref, cret_ref, cimt_ref, abar_ref, s5d_ref, gluw_ref, glub_ref, s5g_ref,
     convw_ref, convb_ref, dtb_ref, alog_ref, dssd_ref, ssdg_ref,
     kt_ref, vt_ref, s5re_in, s5im_in, conv_in, ssd_in) = refs[:27]
    (mix_ref, ktout_ref, vtout_ref, s5re_ref, s5im_ref, convout_ref, ssdout_ref,
     oatt_scr, xs_scr, xdt_scr, bt_scr, ct_scr, dcyt_scr, yt_scr) = refs[n_in:]
    sinks_ref = sinks_all.at[layer]
    i = pl.program_id(0)
    heads64 = (SSD_HEAD_DIM,) * SSD_HEADS

    @pl.when(i == 0)
    def _():
        mix_ref[DEC:, :] = jnp.zeros((mix_ref.shape[0] - DEC, D_MODEL), mix_ref.dtype)

        u = proj_ref[:, U0:U0 + S5_WIDTH]
        ut = u.T.astype(BF16)
        a_re = abar_ref[0]
        a_im = abar_ref[1]
        h0r = s5re_in[...]
        h0i = s5im_in[...]
        hr = _dot(bret_ref[...], ut) + a_re * h0r - a_im * h0i
        hi = _dot(bimt_ref[...], ut) + a_re * h0i + a_im * h0r
        s5re_ref[...] = hr
        s5im_ref[...] = hi
        yt = _dot(cret_ref[...], hr.astype(BF16)) - _dot(cimt_ref[...], hi.astype(BF16))
        mix_ref[0:DEC, ATT_WIDTH:ATT_WIDTH + S5_WIDTH] = _s5_tail(
            yt.T, u, s5d_ref, gluw_ref, glub_ref, s5g_ref).astype(mix_ref.dtype)

        xbc = proj_ref[:, XBC0:XBC0 + SSD_CONV_DIM]
        conv = convb_ref[...]
        for j in range(SSD_CONV - 1):
            conv = conv + conv_in[j] * convw_ref[j:j + 1, :]
        conv = conv + xbc * convw_ref[SSD_CONV - 1:SSD_CONV, :]
        convout_ref[0] = conv_in[1]
        convout_ref[1] = conv_in[2]
        convout_ref[2] = xbc
        xc = _silu(conv)
        xs = xc[:, 0:SSD_WIDTH]
        dt = _softplus(proj_ref[:, DT0:DT0 + 128] + dtb_ref[...])
        decay = jnp.exp(dt * (-jnp.exp(alog_ref[...])))
        xs_scr[...] = xs
        xdt_scr[...] = (xs * _lane_bcast(dt, heads64)).T
        bt_scr[...] = xc[:, SSD_WIDTH:SSD_WIDTH + 128].T
        ct_scr[...] = xc[:, SSD_WIDTH + 128:SSD_WIDTH + 256].T
        dcyt_scr[...] = decay.T[0:8, :]

    r0 = pl.multiple_of(i * DCH, DCH)
    cosv = cos_ref[...]
    sinv = sin_ref[...]
    k_rot = _rope(proj_ref[pl.ds(r0, DCH), K0:K0 + KV_WIDTH], cosv, sinv)
    v_new = proj_ref[pl.ds(r0, DCH), V0:V0 + KV_WIDTH]
    pad_rows = jnp.zeros((BLK - DCH, KV_WIDTH), F32)
    knew_t = jnp.concatenate([k_rot, pad_rows], axis=0).T
    vnew_t = jnp.concatenate([v_new, pad_rows], axis=0).T
    last = _iota((KV_WIDTH, BLK), 1) == BLK - 1
    for b in range(DCH):
        ktout_ref[b] = jnp.where(last, knew_t[:, b:b + 1], pltpu.roll(kt_ref[b], BLK - 1, 1))
        vtout_ref[b] = jnp.where(last, vnew_t[:, b:b + 1], pltpu.roll(vt_ref[b], BLK - 1, 1))
    left = _iota((DCH, 128), 1) < 64
    qs = []
    for h in range(ATT_HEADS):
        j, e, g = h // 2, h % 2, h // 4
        qt = _rope(proj_ref[pl.ds(r0, DCH), Q0 + j * 128:Q0 + (j + 1) * 128], cosv, sinv)
        qt = qt * (HEAD_DIM ** -0.5)
        if e != g:
            qt = pltpu.roll(qt, 64, 1)
        qs.append(jnp.where(left == (g == 0), qt, 0.0))
    qx = jnp.concatenate(qs, axis=0).astype(BF16)
    kt_cat = jnp.concatenate([kt_ref[b] for b in range(DCH)], axis=1).astype(BF16)
    vt_cat = jnp.concatenate([vt_ref[b] for b in range(DCH)], axis=1).astype(BF16)
    s_old = _dot(qx, kt_cat)
    s_new = _dot_nt(qx, k_rot.astype(BF16))
    rseq = _iota((ATT_HEADS * DCH, DCH * BLK), 0) & (DCH - 1)
    same = rseq == (_iota((ATT_HEADS * DCH, DCH * BLK), 1) >> 7)
    same_new = (_iota((ATT_HEADS * DCH, DCH), 0) & (DCH - 1)) == _iota((ATT_HEADS * DCH, DCH), 1)
    s_old = jnp.where(same, s_old, NEG_INF)
    s_new = jnp.where(same_new, s_new, NEG_INF)
    sink = jnp.concatenate(
        [jnp.full((DCH, 1), sinks_ref[h], F32) for h in range(ATT_HEADS)], axis=0)
    m = jnp.maximum(jnp.maximum(jnp.max(s_old, axis=-1, keepdims=True),
                                jnp.max(s_new, axis=-1, keepdims=True)), sink)
    p_old = jnp.exp(s_old - m)
    p_new = jnp.exp(s_new - m)
    den = (jnp.sum(p_old, axis=-1, keepdims=True) + jnp.sum(p_new, axis=-1, keepdims=True)
           + jnp.exp(sink - m))
    o = (_dot_nt((p_old / den).astype(BF16), vt_cat)
         + _dot((p_new / den).astype(BF16), v_new.astype(BF16)))
    o_tiles = []
    for j in range(ATT_HEADS // 2):
        g = j // 2
        a = o[(2 * j) * DCH:(2 * j + 1) * DCH]
        bb = o[(2 * j + 1) * DCH:(2 * j + 2) * DCH]
        if g == 1:
            a = pltpu.roll(a, 64, 1)
        else:
            bb = pltpu.roll(bb, 64, 1)
        o_tiles.append(jnp.where(left, a, bb))
    oatt_scr[pl.ds(r0, DCH), :] = jnp.concatenate(o_tiles, axis=1)

    rows_per_step = SSD_WIDTH // (DEC // DCH)
    head = i // (SSD_HEAD_DIM // rows_per_step)
    g0 = pl.multiple_of((head // 2) * SSD_STATE, SSD_STATE)
    dcy = dcyt_scr[pl.ds(head, 1), :]
    btg = bt_scr[pl.ds(g0, SSD_STATE), :]
    ctg = ct_scr[pl.ds(g0, SSD_STATE), :]
    for rr in range(rows_per_step):
        row = i * rows_per_step + rr
        h_new = dcy * ssd_in[rr] + xdt_scr[pl.ds(row, 1), :] * btg
        ssdout_ref[rr] = h_new
        yt_scr[pl.ds(row, 1), :] = jnp.sum(ctg * h_new, axis=0, keepdims=True)

    @pl.when(i == DEC // DCH - 1)
    def _():
        mix_ref[0:DEC, 0:ATT_WIDTH] = _rmsnorm(oatt_scr[...], attg_ref[...]).astype(mix_ref.dtype)
        xs = xs_scr[...]
        yssd = yt_scr[...].T + dssd_ref[...] * xs
        yc = yssd * _silu(proj_ref[:, Z0:Z0 + SSD_WIDTH])
        mix_ref[0:DEC, ATT_WIDTH + S5_WIDTH:] = _rmsnorm(yc, ssdg_ref[...]).astype(mix_ref.dtype)


def _decode_mixers(mix, proj, pp, tabs, states, prev, layer):
    rows = proj.shape[0]
    depth = states[0].shape[0]
    nsteps = DEC // DCH
    chained = prev is not None
    const = lambda shape: pl.BlockSpec((None,) + shape, lambda i: (layer,) + (0,) * len(shape))
    plain = lambda shape: pl.BlockSpec(shape, lambda i: (0,) * len(shape))
    rows_per_step = SSD_WIDTH // nsteps
    state_specs = [
        pl.BlockSpec((None, DCH, KV_WIDTH, BLK), lambda i: (layer, i, 0, 0)),
        pl.BlockSpec((None, DCH, KV_WIDTH, BLK), lambda i: (layer, i, 0, 0)),
        const((S5_LANES, DEC)), const((S5_LANES, DEC)),
        const((SSD_CONV - 1, DEC, SSD_CONV_DIM)),
        pl.BlockSpec((None, rows_per_step, SSD_STATE, DEC), lambda i: (layer, i, 0, 0)),
    ]
    in_specs = [
        pl.BlockSpec(memory_space=pltpu.SMEM),
        pl.BlockSpec(memory_space=pl.ANY),
        pl.BlockSpec((DEC, PROJ_W), lambda i: (DEC_ROW0 // DEC, 0)),
        plain((1, 128)), plain((1, 128)),
        const((1, ATT_WIDTH)),
        const((S5_LANES, S5_WIDTH)), const((S5_LANES, S5_WIDTH)),
        const((S5_WIDTH, S5_LANES)), const((S5_WIDTH, S5_LANES)),
        const((2, S5_LANES, DEC)), const((1, S5_WIDTH)), const((S5_WIDTH, S5_WIDTH)),
        const((1, S5_WIDTH)), const((1, S5_WIDTH)),
        const((SSD_CONV, SSD_CONV_DIM)), const((1, SSD_CONV_DIM)),
        const((1, 128)), const((1, 128)), const((1, SSD_WIDTH)), const((1, SSD_WIDTH)),
    ] + state_specs + ([pl.BlockSpec(memory_space=pl.ANY)] * 6 if chained else [])
    out_specs = [pl.BlockSpec((2 * DEC, D_MODEL), lambda i: (DEC_ROW0 // (2 * DEC), 0))] + state_specs
    out_shape = [jax.ShapeDtypeStruct(mix.shape, mix.dtype)] + [
        jax.ShapeDtypeStruct(s.shape, F32) for s in states]
    scratch = [
        pltpu.VMEM((DEC, ATT_WIDTH), F32), pltpu.VMEM((DEC, SSD_WIDTH), F32),
        pltpu.VMEM((SSD_WIDTH, DEC), F32), pltpu.VMEM((128, DEC), F32), pltpu.VMEM((128, DEC), F32),
        pltpu.VMEM((8, DEC), F32), pltpu.VMEM((SSD_WIDTH, DEC), F32),
    ]
    aliases = {1: 0}
    if chained:
        aliases.update({27 + k: 1 + k for k in range(6)})
    assert depth > layer
    return pl.pallas_call(
        functools.partial(_decode_mixer_kernel, layer=layer, chained=chained),
        grid=(nsteps,),
        in_specs=in_specs, out_specs=out_specs, out_shape=out_shape,
        scratch_shapes=scratch,
        input_output_aliases=aliases,
        compiler_params=pltpu.CompilerParams(dimension_semantics=("arbitrary",),
                                             vmem_limit_bytes=40 * MIB),
        name="decode_mixers",
    )(pp["sinks"], mix, proj, tabs["cos_d"], tabs["sin_d"], pp["attn_out_g"],
      pp["bre_t"], pp["bim_t"], pp["cre_t"], pp["cim_t"], pp["abar_t"], pp["s5_d"], pp["glu_w"],
      pp["glu_b"], pp["s5_out_g"], pp["conv_w"], pp["conv_b"], pp["dt_bias"], pp["a_log"],
      pp["ssd_d"], pp["ssd_norm_g"], *states, *(prev if chained else ()))


def _rope_tables(nc):
    half = HEAD_DIM // 2
    inv = ROPE_THETA ** (-np.arange(half, dtype=np.float64) / half)

    def tab(pos):
        ang = pos.astype(np.float64)[:, None] * inv[None, :]
        cos = np.cos(ang).astype(np.float32)
        sin = np.sin(ang).astype(np.float32)
        return np.tile(cos, (1, 4)), np.concatenate([-sin, sin, -sin, sin], axis=1)

    cos_p, sin_p = tab(np.arange(nc * BLK, dtype=np.int32) - FRONT)
    cos_d, sin_d = tab(np.full((1,), PAST_LEN, dtype=np.int32))
    return {"cos_p": cos_p, "sin_p": sin_p, "cos_d": cos_d, "sin_d": sin_d}


def kernel(x_prompt, x_sample, cache_k, cache_v, state_s5_re, state_s5_im, state_ssd_conv, state_ssd,
           meta_tokens, ln1_g, w_in, attn_sinks, attn_out_g, s5_a_re, s5_a_im, s5_log_dt,
           s5_b_re, s5_b_im, s5_c_re, s5_c_im, s5_d, s5_glu_w, s5_glu_b, s5_out_g,
           ssd_conv_w, ssd_conv_b, ssd_dt_bias, ssd_a_log, ssd_d, ssd_norm_g, w_out,
           ln2_g, w_gate, w_up, w_down, lnf_g):
    batch, seq, _ = x_prompt.shape
    depth = w_in.shape[0]
    assert batch == 2 and x_sample.shape[0] == DEC and x_sample.shape[1] == 1
    assert seq % TM == 0 and cache_k.shape[2] == BLK
    nc = seq // BLK + 1

    zeros_front = jnp.zeros((FRONT, D_MODEL), F32)
    x_special = jnp.concatenate([zeros_front, meta_tokens, zeros_front, meta_tokens,
                                 x_sample.reshape(DEC, D_MODEL), jnp.zeros((BLK, D_MODEL), F32)], axis=0)
    x_main = x_prompt.reshape(batch * seq, D_MODEL)

    tabs = _rope_tables(nc)
    sc, bb_re, bb_im = _s5_prepare(s5_a_re, s5_a_im, s5_log_dt, s5_b_re, s5_b_im)
    bre, bim = _s5_block_diag_in(bb_re), _s5_block_diag_in(bb_im)
    cre, cim = _s5_block_diag_out(s5_c_re), _s5_block_diag_out(s5_c_im)
    w_in_p = jnp.pad(w_in, ((0, 0), (0, 0), (0, PROJ_W - N_IN))).astype(BF16)
    w_out_b = w_out.astype(BF16)
    ffn_f32 = (w_gate, w_up, w_down)
    ffn_w = tuple(t[0].astype(BF16) for t in ffn_f32)
    pad_heads = lambda t: jnp.pad(t, ((0, 0), (0, 128 - SSD_HEADS)))[:, None, :]
    head_rows = lambda t: jnp.broadcast_to(
        jnp.pad(t, ((0, 0), (0, 8 - SSD_HEADS)))[:, :, None], (depth, 8, BLK))
    row = lambda t: t[:, None, :]
    abar_t = jnp.broadcast_to(jnp.stack([sc[:, SC_ABAR_RE], sc[:, SC_ABAR_IM]], axis=1)[..., None],
                              (depth, 2, S5_LANES, DEC))
    pp = {
        "sinks": attn_sinks, "attn_out_g": row(attn_out_g),
        "bre": bre, "bim": bim, "cre": cre, "cim": cim, "sc": sc,
        "bre_t": jnp.swapaxes(bre, 1, 2), "bim_t": jnp.swapaxes(bim, 1, 2),
        "cre_t": jnp.swapaxes(cre, 1, 2), "cim_t": jnp.swapaxes(cim, 1, 2), "abar_t": abar_t,
        "s5_d": row(s5_d), "glu_w": s5_glu_w.astype(BF16), "glu_b": row(s5_glu_b),
        "s5_out_g": row(s5_out_g),
        "conv_w": ssd_conv_w, "conv_b": row(ssd_conv_b),
        "dt_bias": pad_heads(ssd_dt_bias), "a_log": pad_heads(ssd_a_log),
        "dt_bias_c": head_rows(ssd_dt_bias), "a_log_c": head_rows(ssd_a_log),
        "ssd_d": row(jnp.repeat(ssd_d, SSD_HEAD_DIM, axis=-1)), "ssd_norm_g": row(ssd_norm_g),
    }
    ln1_r, ln2_r = row(ln1_g), row(ln2_g)

    states = (
        jnp.transpose(cache_k, (0, 1, 3, 4, 2)).reshape(depth, DEC, KV_WIDTH, BLK),
        jnp.transpose(cache_v, (0, 1, 3, 4, 2)).reshape(depth, DEC, KV_WIDTH, BLK),
        jnp.transpose(state_s5_re, (0, 2, 3, 1)).reshape(depth, S5_LANES, DEC),
        jnp.transpose(state_s5_im, (0, 2, 3, 1)).reshape(depth, S5_LANES, DEC),
        jnp.transpose(state_ssd_conv, (0, 2, 1, 3)),
        jnp.transpose(state_ssd, (0, 2, 3, 4, 1)).reshape(depth, SSD_WIDTH, SSD_STATE, DEC),
    )

    outs_p = [[] for _ in range(6)]
    outs_s = None
    y_special = y_main = None
    proj = _inproj(x_special, x_main, ln1_r, w_in_p, 0)
    for l in range(depth):
        res_p = _prompt_mixers(proj, pp, tabs, batch, nc, l)
        res_s = _decode_mixers(res_p[0], proj, pp, tabs, states, outs_s, l)
        outs_s = res_s[1:]
        for i in range(6):
            outs_p[i].append(res_p[i + 2])
        ffn_args = (res_s[0], res_p[1], x_special, x_main, w_out_b, ln2_r, ffn_w, l)
        if l + 1 < depth:
            x_special, x_main, proj, ffn_w = _outffn(*ffn_args, g1=ln1_r, w_in=w_in_p,
                                                     next_ffn_f32=ffn_f32)
        else:
            y_special, y_main = _outffn(*ffn_args, gf=lnf_g[None, :])

    y_prompt = y_main.reshape(batch, seq, D_MODEL)
    y_sample = y_special[DEC_ROW0:DEC_ROW0 + DEC].reshape(DEC, 1, D_MODEL)
    kv_p = lambda ts: jnp.transpose(
        jnp.stack(ts).reshape(depth, batch, 2, HEAD_DIM, BLK), (0, 1, 4, 2, 3))
    s5_p = lambda ts: jnp.stack(ts).reshape(depth, batch, S5_GROUPS, S5_STATE)
    kt_s, vt_s, s5re_s, s5im_s, conv_s, ssd_s = outs_s
    kv_s = lambda t: jnp.transpose(t.reshape(depth, DEC, 2, HEAD_DIM, BLK), (0, 1, 4, 2, 3))
    s5_s = lambda t: jnp.transpose(t.reshape(depth, S5_GROUPS, S5_STATE, DEC), (0, 3, 1, 2))
    return (y_prompt, y_sample,
            kv_p(outs_p[0]), kv_p(outs_p[1]), s5_p(outs_p[2]), s5_p(outs_p[3]),
            jnp.stack(outs_p[4]), jnp.stack(outs_p[5]),
            kv_s(kt_s), kv_s(vt_s), s5_s(s5re_s), s5_s(s5im_s),
            jnp.transpose(conv_s, (0, 2, 1, 3)),
            jnp.transpose(ssd_s.reshape(depth, SSD_HEADS, SSD_HEAD_DIM, SSD_STATE, DEC),
                          (0, 4, 1, 2, 3)))
```

```python
import functools

import jax
import jax.numpy as jnp
import numpy as np
from jax import lax
from jax.experimental import pallas as pl
from jax.experimental.pallas import tpu as pltpu

F32 = jnp.float32
BF16 = jnp.bfloat16

D_MODEL = 1024
N_META = 16
HEAD_DIM = 64
ATT_WIDTH = 512
ATT_HEADS = 8
KV_WIDTH = 128
S5_WIDTH = 256
S5_GROUPS = 16
S5_GROUP_CH = 16
S5_STATE = 64
S5_LANES = S5_GROUPS * S5_STATE
SSD_WIDTH = 256
SSD_HEADS = 4
SSD_HEAD_DIM = 64
SSD_STATE = 64
SSD_CONV = 4
SSD_CONV_DIM = 512
FFN_HIDDEN = 2816
NORM_EPS = 1e-6
ROPE_THETA = 10000.0
PAST_LEN = 8192
N_IN = 1796

BLK = 128
FRONT = BLK - N_META
TM = 512
SPECIAL = 4 * BLK
DEC = 128
DEC_ROW0 = 2 * BLK
DCH = 16
PROJ_W = 1920
Q0, K0, V0, U0, Z0, XBC0, DT0 = 0, 512, 640, 768, 1024, 1280, 1792
FFN_CHUNK = 256
SEG = BLK // 8
SC_ABAR_RE, SC_ABAR_IM, SC_SEG1, SC_SEG2, SC_SEG4, SC_CARRY, SC_ROWS = 0, 8, 16, 32, 48, 64, 80
NEG_INF = float("-inf")
MIB = 1024 * 1024


def _dot(a, b):
    return jnp.dot(a, b, preferred_element_type=F32)


def _dot_nt(a, b):
    return lax.dot_general(a, b, (((1,), (1,)), ((), ())), preferred_element_type=F32)


def _sigmoid(x):
    return 1.0 / (1.0 + jnp.exp(-x))


def _silu(x):
    return x * _sigmoid(x)


def _softplus(x):
    return jnp.maximum(x, 0.0) + jnp.log1p(jnp.exp(-jnp.abs(x)))


def _gelu_tanh(x):
    return x * _sigmoid((2.0 * 0.7978845608028654) * (x + 0.044715 * (x * x * x)))


def _rmsnorm(x, g):
    return x * lax.rsqrt(jnp.mean(x * x, axis=-1, keepdims=True) + NORM_EPS) * g


def _iota(shape, dim):
    return lax.broadcasted_iota(jnp.int32, shape, dim)


def _rope(x, cosv, sinv):
    first = (_iota(x.shape, 1) & 63) < 32
    partner = jnp.where(first, pltpu.roll(x, 96, 1), pltpu.roll(x, 32, 1))
    return x * cosv + partner * sinv


def _lane_bcast(x, widths):
    rows = x.shape[0]
    return jnp.concatenate(
        [jnp.broadcast_to(x[:, h:h + 1], (rows, w)) for h, w in enumerate(widths)], axis=1)


def _split3(x):
    hi = x.astype(BF16)
    r1 = x - hi.astype(F32)
    mid = r1.astype(BF16)
    lo = (r1 - mid.astype(F32)).astype(BF16)
    return hi, mid, lo


def _s5_prep_kernel(are_ref, aim_ref, ldt_ref, btre_ref, btim_ref, sc_ref, bbre_ref, bbim_ref):
    ar = are_ref[0]
    ai = aim_ref[0]
    dt = jnp.exp(ldt_ref[0])

    def power(k):
        kf = k.astype(F32)
        mag = jnp.exp((kf * dt) * ar)
        ang = (kf * dt) * ai
        return mag * jnp.cos(ang), mag * jnp.sin(ang)

    pre, pim = power(jnp.full((8, S5_LANES), 1, jnp.int32))
    sc_ref[0, SC_ABAR_RE:SC_ABAR_RE + 8, :] = pre
    sc_ref[0, SC_ABAR_IM:SC_ABAR_IM + 8, :] = pim
    row = _iota((8, S5_LANES), 0)
    for d, base in ((1, SC_SEG1), (2, SC_SEG2), (4, SC_SEG4)):
        sre, sim = power(jnp.full((8, S5_LANES), SEG * d, jnp.int32))
        sc_ref[0, base:base + 8, :] = jnp.where(row >= d, sre, 0.0)
        sc_ref[0, base + 8:base + 16, :] = jnp.where(row >= d, sim, 0.0)
    qre, qim = power(SEG * (row + 1))
    sc_ref[0, SC_CARRY:SC_CARRY + 8, :] = qre
    sc_ref[0, SC_CARRY + 8:SC_CARRY + 16, :] = qim
    abar_re = pre[0:1]
    abar_im = pim[0:1]
    den = ar * ar + ai * ai
    xr = abar_re - 1.0
    f_re = (xr * ar + abar_im * ai) / den
    f_im = (abar_im * ar - xr * ai) / den
    br = btre_ref[0]
    bi = btim_ref[0]
    bbre_ref[0] = f_re * br - f_im * bi
    bbim_ref[0] = f_re * bi + f_im * br


def _s5_prepare(a_re, a_im, log_dt, b_re, b_im):
    depth = a_re.shape[0]
    flat = lambda t: t.reshape(depth, 1, S5_LANES)
    ldt = jnp.repeat(log_dt, S5_STATE, axis=-1).reshape(depth, 1, S5_LANES)
    bt = lambda t: jnp.transpose(t, (0, 3, 1, 2)).reshape(depth, S5_GROUP_CH, S5_LANES)
    vec = pl.BlockSpec((1, 1, S5_LANES), lambda l: (l, 0, 0))
    mat = pl.BlockSpec((1, S5_GROUP_CH, S5_LANES), lambda l: (l, 0, 0))
    return pl.pallas_call(
        _s5_prep_kernel,
        grid=(depth,),
        in_specs=[vec, vec, vec, mat, mat],
        out_specs=[pl.BlockSpec((1, SC_ROWS, S5_LANES), lambda l: (l, 0, 0)), mat, mat],
        out_shape=[jax.ShapeDtypeStruct((depth, SC_ROWS, S5_LANES), F32),
                   jax.ShapeDtypeStruct((depth, S5_GROUP_CH, S5_LANES), F32),
                   jax.ShapeDtypeStruct((depth, S5_GROUP_CH, S5_LANES), F32)],
        name="s5_prep",
    )(flat(a_re), flat(a_im), ldt, bt(b_re), bt(b_im))


def _s5_block_diag_in(bb):
    depth = bb.shape[0]
    same = (jnp.arange(S5_GROUPS)[:, None, None] ==
            (jnp.arange(S5_LANES) // S5_STATE)[None, None, :])
    out = jnp.where(same[None], bb[:, None, :, :], 0.0)
    return out.reshape(depth, S5_WIDTH, S5_LANES).astype(BF16)


def _s5_block_diag_out(c):
    depth = c.shape[0]
    ct = jnp.transpose(c, (0, 1, 3, 2))
    same = jnp.eye(S5_GROUPS, dtype=bool)[None, :, None, :, None]
    out = jnp.where(same, ct[:, :, :, None, :], 0.0)
    return out.reshape(depth, S5_LANES, S5_WIDTH).astype(BF16)


def _run_interleaved(*gens):
    live = list(gens)
    while live:
        live = [g for g in live if next(g, "done") != "done"]


def _tile_rows(special_ref, main_ref):
    return jnp.where(pl.program_id(0) == 0, special_ref[...], main_ref[...])


def _layer_resident(layer, shape):
    return pl.BlockSpec((None,) + shape, lambda i: (layer,) + (0,) * len(shape),
                        pipeline_mode=pl.Buffered(1))


def _special_spec():
    return pl.BlockSpec((TM, D_MODEL), lambda i: (0, 0), pipeline_mode=pl.Buffered(1))


def _main_spec():
    return pl.BlockSpec((TM, D_MODEL), lambda i: (jnp.maximum(i - 1, 0), 0))


def _outffn_kernel(*refs, final):
    mixs_ref, mixm_ref, xs_ref, xm_ref, wo_ref, g2_ref, wg_ref, wu_ref, wd_ref = refs[:9]
    x1 = _tile_rows(xs_ref, xm_ref) + _dot(_tile_rows(mixs_ref, mixm_ref), wo_ref[...])
    h = _rmsnorm(x1, g2_ref[...]).astype(BF16)
    acc = jnp.zeros((TM, D_MODEL), F32)
    for j in range(FFN_HIDDEN // FFN_CHUNK):
        sl = slice(j * FFN_CHUNK, (j + 1) * FFN_CHUNK)
        gate = _dot(h, wg_ref[:, sl])
        up = _dot(h, wu_ref[:, sl])
        acc = acc + _dot((_silu(gate) * up).astype(BF16), wd_ref[sl, :])
    out = x1 + acc
    special_tile = pl.program_id(0) == 0
    if final:
        gf_ref, outs_ref, outm_ref = refs[9:]
        out = _rmsnorm(out, gf_ref[...])
    else:
        f32_slabs = refs[9:12]
        outs_ref, outm_ref = refs[12:14]
        bf16_slabs = refs[14:17]
        for src, dst in zip(f32_slabs, bf16_slabs):
            dst[...] = src[...].astype(BF16)
    outm_ref[...] = out

    @pl.when(special_tile)
    def _():
        outs_ref[...] = out


def _outffn(mix_special, mix_main, x_special, x_main, wo, g2, ffn_w, layer, next_ffn_f32=None, gf=None):
    main_rows = x_main.shape[0]
    rows = SPECIAL + main_rows
    tiles = rows // TM
    resident = functools.partial(_layer_resident, layer)
    whole = lambda a: pl.BlockSpec(a.shape, lambda i: (0, 0), pipeline_mode=pl.Buffered(1))
    mix_main = mix_main.reshape(main_rows, D_MODEL)

    in_specs = [_special_spec(), _main_spec(), _special_spec(), _main_spec(),
                resident((D_MODEL, D_MODEL)), resident((1, D_MODEL))] + [whole(w) for w in ffn_w]
    out_specs = [pl.BlockSpec((TM, D_MODEL), lambda i: (0, 0)), _main_spec()]
    out_shape = [jax.ShapeDtypeStruct((SPECIAL, D_MODEL), F32),
                 jax.ShapeDtypeStruct((main_rows, D_MODEL), F32)]
    args = (mix_special, mix_main, x_special, x_main, wo, g2, *ffn_w)
    if gf is None:
        steps = tiles
        slab_in, slab_out = [], []
        for w in next_ffn_f32:
            w_rows, w_cols = w.shape[1:]
            n = max(d for d in range(1, steps + 1) if w_rows % (16 * d) == 0)
            clamp = lambda i, n=n: jnp.minimum(i, n - 1)
            slab_in.append(pl.BlockSpec((None, w_rows // n, w_cols),
                                        lambda i, clamp=clamp: (layer + 1, clamp(i), 0)))
            slab_out.append(pl.BlockSpec((w_rows // n, w_cols), lambda i, clamp=clamp: (clamp(i), 0)))
        res = pl.pallas_call(
            functools.partial(_outffn_kernel, final=False), grid=(steps,),
            in_specs=in_specs + slab_in,
            out_specs=out_specs + slab_out,
            out_shape=out_shape + [jax.ShapeDtypeStruct(w.shape[1:], BF16) for w in next_ffn_f32],
            compiler_params=pltpu.CompilerParams(dimension_semantics=("arbitrary",),
                                                 vmem_limit_bytes=52 * MIB),
            name="outffn",
        )(*args, *next_ffn_f32)
        return res[0], res[1], tuple(res[2:])
    return pl.pallas_call(
        functools.partial(_outffn_kernel, final=True), grid=(tiles,),
        in_specs=in_specs + [pl.BlockSpec((1, D_MODEL), lambda i: (0, 0))],
        out_specs=out_specs, out_shape=out_shape,
        compiler_params=pltpu.CompilerParams(dimension_semantics=("arbitrary",),
                                             vmem_limit_bytes=52 * MIB),
        name="outffn_final",
    )(*args, gf)


def _s5_tail(y, u, s5d_ref, gluw_ref, glub_ref, s5g_ref):
    y5 = _gelu_tanh(y + s5d_ref[...] * u)
    gate = _dot(y5.astype(BF16), gluw_ref[...]) + glub_ref[...]
    return _rmsnorm(y5 * _sigmoid(gate), s5g_ref[...])


def _pair_block_diag(t):
    left = _iota(t.shape, 1) < 64
    return jnp.concatenate([jnp.where(left, t, 0.0), jnp.where(left, 0.0, t)], axis=0)


N_MIXER_PARAMS = 18
PROJ_SPLIT = 1


def _prompt_mixer_kernel(*refs, layer, batch):
    sinks_all, xs_ref, xm_ref, g1_ref, win_ref = refs[:5]
    shared = refs[5:5 + N_MIXER_PARAMS]
    n_in = 5 + N_MIXER_PARAMS
    mixs_ref, mixm_ref = refs[n_in:n_in + 2]
    outs = refs[n_in + 2:n_in + 8]
    scratch = refs[n_in + 8:-1]
    proj_scr = refs[-1]
    c = pl.program_id(0)
    rows = batch * BLK

    @pl.when(c == 0)
    def _():
        for s in scratch:
            s[...] = jnp.zeros_like(s)
        zero_rows = (_iota((rows, 1), 0) & (BLK - 1)) < FRONT
        h0 = jnp.where(zero_rows, 0.0, _rmsnorm(xs_ref[...].reshape(rows, D_MODEL), g1_ref[...]))
        proj_scr[...] = _dot(h0.astype(BF16), win_ref[...]).reshape(batch, BLK, PROJ_W)

    handover = {}

    def next_projection():
        hn = _rmsnorm(xm_ref[...].reshape(rows, D_MODEL), g1_ref[...]).astype(BF16)
        yield
        for k in range(PROJ_SPLIT):
            cols = slice(k * (PROJ_W // PROJ_SPLIT), (k + 1) * (PROJ_W // PROJ_SPLIT))
            handover[k] = _dot(hn, win_ref[:, cols])
            yield
            yield

    _run_interleaved(next_projection(), *(
        mixer(c, sinks_all.at[layer], proj_scr.at[b], *shared, mixm_ref.at[b],
              *(o.at[b] for o in outs), *(s.at[b] for s in scratch))
        for b in range(batch) for mixer in (_prompt_attention, _prompt_ssd, _prompt_s5)))
    for k in range(PROJ_SPLIT):
        cols = slice(k * (PROJ_W // PROJ_SPLIT), (k + 1) * (PROJ_W // PROJ_SPLIT))
        proj_scr[:, :, cols] = handover[k].reshape(batch, BLK, PROJ_W // PROJ_SPLIT)

    @pl.when(c == 0)
    def _():
        mixs_ref[0:batch] = mixm_ref[...]
        mixs_ref[batch:] = jnp.zeros((mixs_ref.shape[0] - batch, BLK, D_MODEL), mixs_ref.dtype)


def _prompt_attention(
        c, sinks_ref, proj_ref, cos_ref, sin_ref, attg_ref,
        bre_ref, bim_ref, cre_ref, cim_ref, sc_ref, s5d_ref, gluw_ref, glub_ref, s5g_ref,
        convw_ref, convb_ref, dtb_ref, alog_ref, dssd_ref, ssdg_ref,
        mix_ref, kout_ref, vout_ref, s5re_ref, s5im_ref, convout_ref, ssdout_ref,
        kprev, vprev, hre, him, carry_re, carry_im, xbuf, hssd, rows_scr):
    cosv = cos_ref[...]
    sinv = sin_ref[...]
    k_rot = _rope(proj_ref[:, K0:K0 + KV_WIDTH], cosv, sinv)
    vt_new = proj_ref[:, V0:V0 + KV_WIDTH].T
    kout_ref[...] = k_rot.T
    vout_ref[...] = vt_new
    kk = jnp.concatenate([kprev[...], k_rot], axis=0).astype(BF16)
    vvt = jnp.concatenate([vprev[...], vt_new], axis=1).astype(BF16)
    kprev[...] = k_rot
    vprev[...] = vt_new

    qts = [(_rope(proj_ref[:, Q0 + j * 128:Q0 + (j + 1) * 128], cosv, sinv) * (HEAD_DIM ** -0.5)).T
           for j in range(ATT_HEADS // 2)]
    krow = _iota((2 * BLK, BLK), 0)
    qcol = _iota((2 * BLK, BLK), 1)
    kpos = (c - 1) * BLK + krow - FRONT
    ok = (krow >= qcol) & (krow <= qcol + BLK) & (kpos >= 0)
    bias1 = jnp.where(ok, 0.0, NEG_INF)
    bias = jnp.concatenate([bias1] * 4, axis=1)
    zero_half = jnp.zeros((HEAD_DIM, BLK), F32)
    yield

    o_rows = []
    for g in range(2):
        blocks = []
        for r in range(4):
            h = 4 * g + r
            qh = qts[h // 2][(h % 2) * HEAD_DIM:(h % 2 + 1) * HEAD_DIM, :]
            blocks.append(jnp.concatenate([qh, zero_half] if g == 0 else [zero_half, qh], axis=0))
        qg = jnp.concatenate(blocks, axis=1).astype(BF16)
        s = _dot(kk, qg) + bias
        sink = jnp.concatenate(
            [jnp.full((1, BLK), sinks_ref[4 * g + r], F32) for r in range(4)], axis=1)
        yield
        m = jnp.maximum(jnp.max(s, axis=0, keepdims=True), sink)
        p = jnp.exp(s - m)
        yield
        inv_den = 1.0 / (jnp.sum(p, axis=0, keepdims=True) + jnp.exp(sink - m))
        og = _dot(vvt, p.astype(BF16))
        for r in range(4):
            cols = slice(r * BLK, (r + 1) * BLK)
            o_rows.append(og[g * HEAD_DIM:(g + 1) * HEAD_DIM, cols] * inv_den[:, cols])
        yield
    o_att = jnp.concatenate(o_rows, axis=0).T
    mix_ref[:, 0:ATT_WIDTH] = _rmsnorm(o_att, attg_ref[...]).astype(mix_ref.dtype)


def _prompt_s5(
        c, sinks_ref, proj_ref, cos_ref, sin_ref, attg_ref,
        bre_ref, bim_ref, cre_ref, cim_ref, sc_ref, s5d_ref, gluw_ref, glub_ref, s5g_ref,
        convw_ref, convb_ref, dtb_ref, alog_ref, dssd_ref, ssdg_ref,
        mix_ref, kout_ref, vout_ref, s5re_ref, s5im_ref, convout_ref, ssdout_ref,
        kprev, vprev, hre, him, carry_re, carry_im, xbuf, hssd, rows_scr):
    for j in range(S5_WIDTH // 128):
        rows_scr[j] = proj_ref[:, U0 + j * 128:U0 + (j + 1) * 128]
    u = jnp.concatenate(
        [jnp.concatenate([rows_scr[j, pl.ds(t, 8, stride=SEG), :] for j in range(S5_WIDTH // 128)], axis=1)
         for t in range(SEG)], axis=0)
    ub = u.astype(BF16)
    hre[...] = _dot(ub, bre_ref[...])
    him[...] = _dot(ub, bim_ref[...])
    yield
    first_seg = _iota((8, 128), 0) == 0
    for lt in range(S5_LANES // 128):
        ls = slice(lt * 128, (lt + 1) * 128)
        a_r = sc_ref[SC_ABAR_RE:SC_ABAR_RE + 8, ls]
        a_i = sc_ref[SC_ABAR_IM:SC_ABAR_IM + 8, ls]
        er = hre[0:8, ls]
        ei = him[0:8, ls]
        for t in range(1, SEG):
            rs = slice(8 * t, 8 * t + 8)
            er, ei = hre[rs, ls] + a_r * er - a_i * ei, him[rs, ls] + a_r * ei + a_i * er
        yield
        for d, b0 in ((1, SC_SEG1), (2, SC_SEG2), (4, SC_SEG4)):
            s_r = sc_ref[b0:b0 + 8, ls]
            s_i = sc_ref[b0 + 8:b0 + 16, ls]
            pr = pltpu.roll(er, d, 0)
            pi = pltpu.roll(ei, d, 0)
            er, ei = er + s_r * pr - s_i * pi, ei + s_r * pi + s_i * pr
        cr = carry_re[:, ls]
        ci = carry_im[:, ls]
        q_r = sc_ref[SC_CARRY:SC_CARRY + 8, ls]
        q_i = sc_ref[SC_CARRY + 8:SC_CARRY + 16, ls]
        tr = er + q_r * cr - q_i * ci
        ti = ei + q_r * ci + q_i * cr
        hr = jnp.where(first_seg, cr, pltpu.roll(tr, 1, 0))
        hi = jnp.where(first_seg, ci, pltpu.roll(ti, 1, 0))
        carry_re[:, ls] = jnp.broadcast_to(tr[7:8], (8, 128))
        carry_im[:, ls] = jnp.broadcast_to(ti[7:8], (8, 128))
        for t in range(SEG):
            rs = slice(8 * t, 8 * t + 8)
            hr, hi = hre[rs, ls] + a_r * hr - a_i * hi, him[rs, ls] + a_r * hi + a_i * hr
            hre[rs, ls] = hr
            him[rs, ls] = hi
        yield
    s5re_ref[...] = carry_re[0:1, :]
    s5im_ref[...] = carry_im[0:1, :]
    y_perm = _dot(hre[...].astype(BF16), cre_ref[...]) - _dot(him[...].astype(BF16), cim_ref[...])
    yield
    o_perm = _s5_tail(y_perm, u, s5d_ref, gluw_ref, glub_ref, s5g_ref)
    for j in range(S5_WIDTH // 128):
        for t in range(SEG):
            rows_scr[j, pl.ds(t, 8, stride=SEG), :] = o_perm[8 * t:8 * t + 8, j * 128:(j + 1) * 128]
        mix_ref[:, ATT_WIDTH + j * 128:ATT_WIDTH + (j + 1) * 128] = rows_scr[j].astype(mix_ref.dtype)


def _prompt_ssd(
        c, sinks_ref, proj_ref, cos_ref, sin_ref, attg_ref,
        bre_ref, bim_ref, cre_ref, cim_ref, sc_ref, s5d_ref, gluw_ref, glub_ref, s5g_ref,
        convw_ref, convb_ref, dtb_ref, alog_ref, dssd_ref, ssdg_ref,
        mix_ref, kout_ref, vout_ref, s5re_ref, s5im_ref, convout_ref, ssdout_ref,
        kprev, vprev, hre, him, carry_re, carry_im, xbuf, hssd, rows_scr):
    xbc = proj_ref[:, XBC0:XBC0 + SSD_CONV_DIM]
    xbuf[8:8 + BLK, :] = xbc
    conv = convb_ref[...]
    for j in range(SSD_CONV):
        conv = conv + xbuf[5 + j:5 + j + BLK, :] * convw_ref[j:j + 1, :]
    yield
    xc = _silu(conv)
    convout_ref[...] = xbc[BLK - (SSD_CONV - 1):BLK, :]
    xbuf[0:8, :] = xbc[BLK - 8:BLK, :]
    xs = xc[:, 0:SSD_WIDTH]
    bm = xc[:, SSD_WIDTH:SSD_WIDTH + 128]
    cm = xc[:, SSD_WIDTH + 128:SSD_WIDTH + 256]
    yield

    lane = _iota((BLK, 128), 1)
    row = _iota((BLK, 128), 0)
    head_r = _iota((8, BLK), 0)
    time_c = _iota((8, BLK), 1)
    live = (head_r < SSD_HEADS) & (time_c >= jnp.where(c == 0, FRONT, 0))
    raw_t = proj_ref[:, DT0:DT0 + 128].T[0:8, :]
    dt_t = jnp.where(live, _softplus(raw_t + dtb_ref[...]), 0.0)
    dta_t = dt_t * (-jnp.exp(alog_ref[...]))
    causal = lane <= row
    triu = jnp.where(row <= lane, 1.0, 0.0).astype(BF16)
    hi3, mid3, lo3 = _split3(dta_t)
    cs_t = _dot(hi3, triu) + _dot(mid3, triu) + _dot(lo3, triu)
    yield
    cs_last = cs_t[:, BLK - 1:BLK]
    packed = jnp.concatenate(
        [dt_t, dt_t * jnp.exp(cs_last - cs_t), jnp.exp(cs_t), cs_t, jnp.zeros((BLK - 32, BLK), F32)], axis=0)
    cols = packed.T
    heads64 = (SSD_HEAD_DIM,) * SSD_HEADS
    xd = xs * _lane_bcast(cols[:, 0:SSD_HEADS], heads64)
    dxd = xs * _lane_bcast(cols[:, 8:8 + SSD_HEADS], heads64)
    ecs = _lane_bcast(cols[:, 16:16 + SSD_HEADS], heads64)
    cs = cols[:, 24:24 + SSD_HEADS]
    yield

    bmb = bm.astype(BF16)
    left = lane < 64
    cb = (_dot_nt(jnp.where(left, cm, 0.0).astype(BF16), bmb),
          _dot_nt(jnp.where(left, 0.0, cm).astype(BF16), bmb))
    scores = []
    for h in range(SSD_HEADS):
        seg = cs[:, h:h + 1] - cs_t[h:h + 1, :]
        scores.append((cb[h // 2] * jnp.exp(jnp.where(causal, seg, NEG_INF))).astype(BF16))
        yield
    y_diag = jnp.concatenate(
        [_dot(jnp.concatenate([scores[2 * j], scores[2 * j + 1]], axis=1),
              _pair_block_diag(xd[:, j * 128:(j + 1) * 128]).astype(BF16)) for j in range(2)], axis=1)
    yield

    h_prev = hssd[...]
    y_off = _dot_nt(cm.astype(BF16), h_prev.astype(BF16)) * ecs
    yield
    states = _dot(dxd.T.astype(BF16), bmb)
    own = (_iota((SSD_WIDTH, 128), 0) >> 7) == (_iota((SSD_WIDTH, 128), 1) >> 6)
    cd = jnp.exp(cs_last)
    cdm = jnp.concatenate(
        [jnp.broadcast_to(cd[h:h + 1, :], (SSD_HEAD_DIM, 128)) for h in range(SSD_HEADS)], axis=0)
    h_new = cdm * h_prev + jnp.where(own, states, 0.0)
    hssd[...] = h_new
    for h in range(SSD_HEADS):
        g0 = (h // 2) * SSD_STATE
        ssdout_ref[h] = h_new[h * SSD_HEAD_DIM:(h + 1) * SSD_HEAD_DIM, g0:g0 + SSD_STATE]
    yield

    yssd = y_diag + y_off + dssd_ref[...] * xs
    yc = yssd * _silu(proj_ref[:, Z0:Z0 + SSD_WIDTH])
    mix_ref[:, ATT_WIDTH + S5_WIDTH:] = _rmsnorm(yc, ssdg_ref[...]).astype(mix_ref.dtype)


def _prompt_mixers(x_special, x_main, pp, tabs, batch, nc, layer):
    const = lambda shape: pl.BlockSpec((None,) + shape, lambda c: (layer,) + (0,) * len(shape))
    whole = lambda shape: pl.BlockSpec(shape, lambda c: (0,) * len(shape))
    in_specs = [
        pl.BlockSpec(memory_space=pltpu.SMEM),
        pl.BlockSpec((batch, BLK, D_MODEL), lambda c: (0, 0, 0)),
        pl.BlockSpec((batch, BLK, D_MODEL), lambda c: (0, jnp.minimum(c, nc - 2), 0)),
        const((1, D_MODEL)),
        pl.BlockSpec((None, D_MODEL, PROJ_W), lambda c: (layer, 0, 0), pipeline_mode=pl.Buffered(1)),
        pl.BlockSpec((BLK, 128), lambda c: (c, 0)),
        pl.BlockSpec((BLK, 128), lambda c: (c, 0)),
        const((1, ATT_WIDTH)),
        const((S5_WIDTH, S5_LANES)), const((S5_WIDTH, S5_LANES)),
        const((S5_LANES, S5_WIDTH)), const((S5_LANES, S5_WIDTH)),
        const((SC_ROWS, S5_LANES)), const((1, S5_WIDTH)), const((S5_WIDTH, S5_WIDTH)),
        const((1, S5_WIDTH)), const((1, S5_WIDTH)),
        const((SSD_CONV, SSD_CONV_DIM)), const((1, SSD_CONV_DIM)),
        const((8, BLK)), const((8, BLK)), const((1, SSD_WIDTH)), const((1, SSD_WIDTH)),
    ]
    state_shapes =[(BLK, KV_WIDTH), (BLK, KV_WIDTH), (1, S5_LANES), (1, S5_LANES),
                    (SSD_CONV - 1, SSD_CONV_DIM), (SSD_HEADS, SSD_HEAD_DIM, SSD_STATE)]
    out_specs = [
        pl.BlockSpec((SPECIAL // BLK, BLK, D_MODEL), lambda c: (0, 0, 0)),
        pl.BlockSpec((batch, BLK, D_MODEL), lambda c: (0, jnp.maximum(c - 1, 0), 0)),
    ] + [whole((batch,) + s) for s in state_shapes]
    out_shape = [
        jax.ShapeDtypeStruct((SPECIAL // BLK, BLK, D_MODEL), BF16),
        jax.ShapeDtypeStruct((batch, (nc - 1) * BLK, D_MODEL), BF16),
    ] + [jax.ShapeDtypeStruct((batch,) + s, F32) for s in state_shapes]
    scratch = [pltpu.VMEM((batch,) + s, F32) for s in (
        (BLK, KV_WIDTH), (BLK, KV_WIDTH), (BLK, S5_LANES), (BLK, S5_LANES),
        (8, S5_LANES), (8, S5_LANES), (BLK + 8, SSD_CONV_DIM), (SSD_WIDTH, 128),
        (S5_WIDTH // 128, BLK, 128))] + [pltpu.VMEM((batch, BLK, PROJ_W), F32)]
    res = pl.pallas_call(
        functools.partial(_prompt_mixer_kernel, layer=layer, batch=batch),
        grid=(nc,),
        in_specs=in_specs, out_specs=out_specs, out_shape=out_shape, scratch_shapes=scratch,
        compiler_params=pltpu.CompilerParams(dimension_semantics=("arbitrary",),
                                             vmem_limit_bytes=48 * MIB),
        name="prompt_mixers",
    )(pp["sinks"], x_special.reshape(SPECIAL // BLK, BLK, D_MODEL),
      x_main.reshape(batch, (nc - 1) * BLK, D_MODEL), pp["ln1"], pp["w_in"],
      tabs["cos_p"], tabs["sin_p"], pp["attn_out_g"],
      pp["bre"], pp["bim"], pp["cre"], pp["cim"], pp["sc"], pp["s5_d"], pp["glu_w"], pp["glu_b"],
      pp["s5_out_g"], pp["conv_w"], pp["conv_b"], pp["dt_bias_c"], pp["a_log_c"], pp["ssd_d"],
      pp["ssd_norm_g"])
    return [res[0].reshape(SPECIAL, D_MODEL)] + list(res[1:])


def _decode_mixer_kernel(*refs, layer, chained):
    n_in = 29 + (6 if chained else 0)
    (sinks_all, _, x_ref, cos_ref, sin_ref, attg_ref,
     bret_ref, bimt_ref, cret_ref, cimt_ref, abar_ref, s5d_ref, gluw_ref, glub_ref, s5g_ref,
     convw_ref, convb_ref, dtb_ref, alog_ref, dssd_ref, ssdg_ref, g1_ref, win_ref,
     kt_ref, vt_ref, s5re_in, s5im_in, conv_in, ssd_in) = refs[:29]
    (mix_ref, ktout_ref, vtout_ref, s5re_ref, s5im_ref, convout_ref, ssdout_ref,
     oatt_scr, xs_scr, xdt_scr, bt_scr, ct_scr, dcyt_scr, yt_scr, proj_ref) = refs[n_in:]
    sinks_ref = sinks_all.at[layer]
    i = pl.program_id(0)
    heads64 = (SSD_HEAD_DIM,) * SSD_HEADS

    @pl.when(i == 0)
    def _():
        mix_ref[DEC:, :] = jnp.zeros((mix_ref.shape[0] - DEC, D_MODEL), mix_ref.dtype)
        proj_ref[...] = _dot(_rmsnorm(x_ref[...], g1_ref[...]).astype(BF16), win_ref[...])

        u = proj_ref[:, U0:U0 + S5_WIDTH]
        ut = u.T.astype(BF16)
        a_re = abar_ref[0]
        a_im = abar_ref[1]
        h0r = s5re_in[...]
        h0i = s5im_in[...]
        hr = _dot(bret_ref[...], ut) + a_re * h0r - a_im * h0i
        hi = _dot(bimt_ref[...], ut) + a_re * h0i + a_im * h0r
        s5re_ref[...] = hr
        s5im_ref[...] = hi
        yt = _dot(cret_ref[...], hr.astype(BF16)) - _dot(cimt_ref[...], hi.astype(BF16))
        mix_ref[0:DEC, ATT_WIDTH:ATT_WIDTH + S5_WIDTH] = _s5_tail(
            yt.T, u, s5d_ref, gluw_ref, glub_ref, s5g_ref).astype(mix_ref.dtype)

        xbc = proj_ref[:, XBC0:XBC0 + SSD_CONV_DIM]
        conv = convb_ref[...]
        for j in range(SSD_CONV - 1):
            conv = conv + conv_in[j] * convw_ref[j:j + 1, :]
        conv = conv + xbc * convw_ref[SSD_CONV - 1:SSD_CONV, :]
        convout_ref[0] = conv_in[1]
        convout_ref[1] = conv_in[2]
        convout_ref[2] = xbc
        xc = _silu(conv)
        xs = xc[:, 0:SSD_WIDTH]
        dt = _softplus(proj_ref[:, DT0:DT0 + 128] + dtb_ref[...])
        decay = jnp.exp(dt * (-jnp.exp(alog_ref[...])))
        xs_scr[...] = xs
        xdt_scr[...] = (xs * _lane_bcast(dt, heads64)).T
        bt_scr[...] = xc[:, SSD_WIDTH:SSD_WIDTH + 128].T
        ct_scr[...] = xc[:, SSD_WIDTH + 128:SSD_WIDTH + 256].T
        dcyt_scr[...] = decay.T[0:8, :]

    r0 = pl.multiple_of(i * DCH, DCH)
    cosv = cos_ref[...]
    sinv = sin_ref[...]
    k_rot = _rope(proj_ref[pl.ds(r0, DCH), K0:K0 + KV_WIDTH], cosv, sinv)
    v_new = proj_ref[pl.ds(r0, DCH), V0:V0 + KV_WIDTH]
    pad_rows = jnp.zeros((BLK - DCH, KV_WIDTH), F32)
    knew_t = jnp.concatenate([k_rot, pad_rows], axis=0).T
    vnew_t = jnp.concatenate([v_new, pad_rows], axis=0).T
    last = _iota((KV_WIDTH, BLK), 1) == BLK - 1
    for b in range(DCH):
        ktout_ref[b] = jnp.where(last, knew_t[:, b:b + 1], pltpu.roll(kt_ref[b], BLK - 1, 1))
        vtout_ref[b] = jnp.where(last, vnew_t[:, b:b + 1], pltpu.roll(vt_ref[b], BLK - 1, 1))
    left = _iota((DCH, 128), 1) < 64
    qs = []
    for h in range(ATT_HEADS):
        j, e, g = h // 2, h % 2, h // 4
        qt = _rope(proj_ref[pl.ds(r0, DCH), Q0 + j * 128:Q0 + (j + 1) * 128], cosv, sinv)
        qt = qt * (HEAD_DIM ** -0.5)
        if e != g:
            qt = pltpu.roll(qt, 64, 1)
        qs.append(jnp.where(left == (g == 0), qt, 0.0))
    qx = jnp.concatenate(qs, axis=0).astype(BF16)
    kt_cat = jnp.concatenate([kt_ref[b] for b in range(DCH)], axis=1).astype(BF16)
    vt_cat = jnp.concatenate([vt_ref[b] for b in range(DCH)], axis=1).astype(BF16)
    s_old = _dot(qx, kt_cat)
    s_new = _dot_nt(qx, k_rot.astype(BF16))
    rseq = _iota((ATT_HEADS * DCH, DCH * BLK), 0) & (DCH - 1)
    same = rseq == (_iota((ATT_HEADS * DCH, DCH * BLK), 1) >> 7)
    same_new = (_iota((ATT_HEADS * DCH, DCH), 0) & (DCH - 1)) == _iota((ATT_HEADS * DCH, DCH), 1)
    s_old = jnp.where(same, s_old, NEG_INF)
    s_new = jnp.where(same_new, s_new, NEG_INF)
    sink = jnp.concatenate(
        [jnp.full((DCH, 1), sinks_ref[h], F32) for h in range(ATT_HEADS)], axis=0)
    m = jnp.maximum(jnp.maximum(jnp.max(s_old, axis=-1, keepdims=True),
                                jnp.max(s_new, axis=-1, keepdims=True)), sink)
    p_old = jnp.exp(s_old - m)
    p_new = jnp.exp(s_new - m)
    den = (jnp.sum(p_old, axis=-1, keepdims=True) + jnp.sum(p_new, axis=-1, keepdims=True)
           + jnp.exp(sink - m))
    o = (_dot_nt((p_old / den).astype(BF16), vt_cat)
         + _dot((p_new / den).astype(BF16), v_new.astype(BF16)))
    o_tiles = []
    for j in range(ATT_HEADS // 2):
        g = j // 2
        a = o[(2 * j) * DCH:(2 * j + 1) * DCH]
        bb = o[(2 * j + 1) * DCH:(2 * j + 2) * DCH]
        if g == 1:
            a = pltpu.roll(a, 64, 1)
        else:
            bb = pltpu.roll(bb, 64, 1)
        o_tiles.append(jnp.where(left, a, bb))
    oatt_scr[pl.ds(r0, DCH), :] = jnp.concatenate(o_tiles, axis=1)

    rows_per_step = SSD_WIDTH // (DEC // DCH)
    head = i // (SSD_HEAD_DIM // rows_per_step)
    g0 = pl.multiple_of((head // 2) * SSD_STATE, SSD_STATE)
    dcy = dcyt_scr[pl.ds(head, 1), :]
    btg = bt_scr[pl.ds(g0, SSD_STATE), :]
    ctg = ct_scr[pl.ds(g0, SSD_STATE), :]
    for rr in range(rows_per_step):
        row = i * rows_per_step + rr
        h_new = dcy * ssd_in[rr] + xdt_scr[pl.ds(row, 1), :] * btg
        ssdout_ref[rr] = h_new
        yt_scr[pl.ds(row, 1), :] = jnp.sum(ctg * h_new, axis=0, keepdims=True)

    @pl.when(i == DEC // DCH - 1)
    def _():
        mix_ref[0:DEC, 0:ATT_WIDTH] = _rmsnorm(oatt_scr[...], attg_ref[...]).astype(mix_ref.dtype)
        xs = xs_scr[...]
        yssd = yt_scr[...].T + dssd_ref[...] * xs
        yc = yssd * _silu(proj_ref[:, Z0:Z0 + SSD_WIDTH])
        mix_ref[0:DEC, ATT_WIDTH + S5_WIDTH:] = _rmsnorm(yc, ssdg_ref[...]).astype(mix_ref.dtype)


def _decode_mixers(mix, x_special, pp, tabs, states, prev, layer):
    depth = states[0].shape[0]
    nsteps = DEC // DCH
    chained = prev is not None
    const = lambda shape: pl.BlockSpec((None,) + shape, lambda i: (layer,) + (0,) * len(shape))
    plain = lambda shape: pl.BlockSpec(shape, lambda i: (0,) * len(shape))
    rows_per_step = SSD_WIDTH // nsteps
    state_specs = [
        pl.BlockSpec((None, DCH, KV_WIDTH, BLK), lambda i: (layer, i, 0, 0)),
        pl.BlockSpec((None, DCH, KV_WIDTH, BLK), lambda i: (layer, i, 0, 0)),
        const((S5_LANES, DEC)), const((S5_LANES, DEC)),
        const((SSD_CONV - 1, DEC, SSD_CONV_DIM)),
        pl.BlockSpec((None, rows_per_step, SSD_STATE, DEC), lambda i: (layer, i, 0, 0)),
    ]
    in_specs = [
        pl.BlockSpec(memory_space=pltpu.SMEM),
        pl.BlockSpec(memory_space=pl.ANY),
        pl.BlockSpec((DEC, D_MODEL), lambda i: (DEC_ROW0 // DEC, 0)),
        plain((1, 128)), plain((1, 128)),
        const((1, ATT_WIDTH)),
        const((S5_LANES, S5_WIDTH)), const((S5_LANES, S5_WIDTH)),
        const((S5_WIDTH, S5_LANES)), const((S5_WIDTH, S5_LANES)),
        const((2, S5_LANES, DEC)), const((1, S5_WIDTH)), const((S5_WIDTH, S5_WIDTH)),
        const((1, S5_WIDTH)), const((1, S5_WIDTH)),
        const((SSD_CONV, SSD_CONV_DIM)), const((1, SSD_CONV_DIM)),
        const((1, 128)), const((1, 128)), const((1, SSD_WIDTH)), const((1, SSD_WIDTH)),
        const((1, D_MODEL)), const((D_MODEL, PROJ_W)),
    ] + state_specs + ([pl.BlockSpec(memory_space=pl.ANY)] * 6 if chained else [])
    out_specs = [pl.BlockSpec((2 * DEC, D_MODEL), lambda i: (DEC_ROW0 // (2 * DEC), 0))] + state_specs
    out_shape = [jax.ShapeDtypeStruct(mix.shape, mix.dtype)] + [
        jax.ShapeDtypeStruct(s.shape, F32) for s in states]
    scratch = [
        pltpu.VMEM((DEC, ATT_WIDTH), F32), pltpu.VMEM((DEC, SSD_WIDTH), F32),
        pltpu.VMEM((SSD_WIDTH, DEC), F32), pltpu.VMEM((128, DEC), F32), pltpu.VMEM((128, DEC), F32),
        pltpu.VMEM((8, DEC), F32), pltpu.VMEM((SSD_WIDTH, DEC), F32),
        pltpu.VMEM((DEC, PROJ_W), F32),
    ]
    aliases = {1: 0}
    if chained:
        aliases.update({29 + k: 1 + k for k in range(6)})
    assert depth > layer
    return pl.pallas_call(
        functools.partial(_decode_mixer_kernel, layer=layer, chained=chained),
        grid=(nsteps,),
        in_specs=in_specs, out_specs=out_specs, out_shape=out_shape,
        scratch_shapes=scratch,
        input_output_aliases=aliases,
        compiler_params=pltpu.CompilerParams(dimension_semantics=("arbitrary",),
                                             vmem_limit_bytes=40 * MIB),
        name="decode_mixers",
    )(pp["sinks"], mix, x_special, tabs["cos_d"], tabs["sin_d"], pp["attn_out_g"],
      pp["bre_t"], pp["bim_t"], pp["cre_t"], pp["cim_t"], pp["abar_t"], pp["s5_d"], pp["glu_w"],
      pp["glu_b"], pp["s5_out_g"], pp["conv_w"], pp["conv_b"], pp["dt_bias"], pp["a_log"],
      pp["ssd_d"], pp["ssd_norm_g"], pp["ln1"], pp["w_in"], *states, *(prev if chained else ()))


def _rope_tables(nc):
    half = HEAD_DIM // 2
    inv = ROPE_THETA ** (-np.arange(half, dtype=np.float64) / half)

    def tab(pos):
        ang = pos.astype(np.float64)[:, None] * inv[None, :]
        cos = np.cos(ang).astype(np.float32)
        sin = np.sin(ang).astype(np.float32)
        return np.tile(cos, (1, 4)), np.concatenate([-sin, sin, -sin, sin], axis=1)

    cos_p, sin_p = tab(np.arange(nc * BLK, dtype=np.int32) - FRONT)
    cos_d, sin_d = tab(np.full((1,), PAST_LEN, dtype=np.int32))
    return {"cos_p": cos_p, "sin_p": sin_p, "cos_d": cos_d, "sin_d": sin_d}


def kernel(x_prompt, x_sample, cache_k, cache_v, state_s5_re, state_s5_im, state_ssd_conv, state_ssd,
           meta_tokens, ln1_g, w_in, attn_sinks, attn_out_g, s5_a_re, s5_a_im, s5_log_dt,
           s5_b_re, s5_b_im, s5_c_re, s5_c_im, s5_d, s5_glu_w, s5_glu_b, s5_out_g,
           ssd_conv_w, ssd_conv_b, ssd_dt_bias, ssd_a_log, ssd_d, ssd_norm_g, w_out,
           ln2_g, w_gate, w_up, w_down, lnf_g):
    batch, seq, _ = x_prompt.shape
    depth = w_in.shape[0]
    assert batch == 2 and x_sample.shape[0] == DEC and x_sample.shape[1] == 1
    assert seq % TM == 0 and cache_k.shape[2] == BLK
    nc = seq // BLK + 1

    zeros_front = jnp.zeros((FRONT, D_MODEL), F32)
    x_special = jnp.concatenate([zeros_front, meta_tokens, zeros_front, meta_tokens,
                                 x_sample.reshape(DEC, D_MODEL), jnp.zeros((BLK, D_MODEL), F32)], axis=0)
    x_main = x_prompt.reshape(batch * seq, D_MODEL)

    tabs = _rope_tables(nc)
    sc, bb_re, bb_im = _s5_prepare(s5_a_re, s5_a_im, s5_log_dt, s5_b_re, s5_b_im)
    bre, bim = _s5_block_diag_in(bb_re), _s5_block_diag_in(bb_im)
    cre, cim = _s5_block_diag_out(s5_c_re), _s5_block_diag_out(s5_c_im)
    w_in_p = jnp.pad(w_in, ((0, 0), (0, 0), (0, PROJ_W - N_IN))).astype(BF16)
    w_out_b = w_out.astype(BF16)
    ffn_f32 = (w_gate, w_up, w_down)
    ffn_w = tuple(t[0].astype(BF16) for t in ffn_f32)
    pad_heads = lambda t: jnp.pad(t, ((0, 0), (0, 128 - SSD_HEADS)))[:, None, :]
    head_rows = lambda t: jnp.broadcast_to(
        jnp.pad(t, ((0, 0), (0, 8 - SSD_HEADS)))[:, :, None], (depth, 8, BLK))
    row = lambda t: t[:, None, :]
    abar_t = jnp.broadcast_to(jnp.stack([sc[:, SC_ABAR_RE], sc[:, SC_ABAR_IM]], axis=1)[..., None],
                              (depth, 2, S5_LANES, DEC))
    pp = {
        "sinks": attn_sinks, "attn_out_g": row(attn_out_g), "ln1": row(ln1_g), "w_in": w_in_p,
        "bre": bre, "bim": bim, "cre": cre, "cim": cim, "sc": sc,
        "bre_t": jnp.swapaxes(bre, 1, 2), "bim_t": jnp.swapaxes(bim, 1, 2),
        "cre_t": jnp.swapaxes(cre, 1, 2), "cim_t": jnp.swapaxes(cim, 1, 2), "abar_t": abar_t,
        "s5_d": row(s5_d), "glu_w": s5_glu_w.astype(BF16), "glu_b": row(s5_glu_b),
        "s5_out_g": row(s5_out_g),
        "conv_w": ssd_conv_w, "conv_b": row(ssd_conv_b),
        "dt_bias": pad_heads(ssd_dt_bias), "a_log": pad_heads(ssd_a_log),
        "dt_bias_c": head_rows(ssd_dt_bias), "a_log_c": head_rows(ssd_a_log),
        "ssd_d": row(jnp.repeat(ssd_d, SSD_HEAD_DIM, axis=-1)), "ssd_norm_g": row(ssd_norm_g),
    }
    ln2_r = row(ln2_g)

    states = (
        jnp.transpose(cache_k, (0, 1, 3, 4, 2)).reshape(depth, DEC, KV_WIDTH, BLK),
        jnp.transpose(cache_v, (0, 1, 3, 4, 2)).reshape(depth, DEC, KV_WIDTH, BLK),
        jnp.transpose(state_s5_re, (0, 2, 3, 1)).reshape(depth, S5_LANES, DEC),
        jnp.transpose(state_s5_im, (0, 2, 3, 1)).reshape(depth, S5_LANES, DEC),
        jnp.transpose(state_ssd_conv, (0, 2, 1, 3)),
        jnp.transpose(state_ssd, (0, 2, 3, 4, 1)).reshape(depth, SSD_WIDTH, SSD_STATE, DEC),
    )

    outs_p = [[] for _ in range(6)]
    outs_s = None
    y_special = y_main = None
    for l in range(depth):
        res_p = _prompt_mixers(x_special, x_main, pp, tabs, batch, nc, l)
        res_s = _decode_mixers(res_p[0], x_special, pp, tabs, states, outs_s, l)
        outs_s = res_s[1:]
        for i in range(6):
            outs_p[i].append(res_p[i + 2])
        ffn_args = (res_s[0], res_p[1], x_special, x_main, w_out_b, ln2_r, ffn_w, l)
        if l + 1 < depth:
            x_special, x_main, ffn_w = _outffn(*ffn_args, next_ffn_f32=ffn_f32)
        else:
            y_special, y_main = _outffn(*ffn_args, gf=lnf_g[None, :])

    y_prompt = y_main.reshape(batch, seq, D_MODEL)
    y_sample = y_special[DEC_ROW0:DEC_ROW0 + DEC].reshape(DEC, 1, D_MODEL)
    kv_p = lambda ts: jnp.transpose(
        jnp.stack(ts).reshape(depth, batch, 2, HEAD_DIM, BLK), (0, 1, 4, 2, 3))
    s5_p = lambda ts: jnp.stack(ts).reshape(depth, batch, S5_GROUPS, S5_STATE)
    kt_s, vt_s, s5re_s, s5im_s, conv_s, ssd_s = outs_s
    kv_s = lambda t: jnp.transpose(t.reshape(depth, DEC, 2, HEAD_DIM, BLK), (0, 1, 4, 2, 3))
    s5_s = lambda t: jnp.transpose(t.reshape(depth, S5_GROUPS, S5_STATE, DEC), (0, 3, 1, 2))
    return (y_prompt, y_sample,
            kv_p(outs_p[0]), kv_p(outs_p[1]), s5_p(outs_p[2]), s5_p(outs_p[3]),
            jnp.stack(outs_p[4]), jnp.stack(outs_p[5]),
            kv_s(kt_s), kv_s(vt_s), s5_s(s5re_s), s5_s(s5im_s),
            jnp.transpose(conv_s, (0, 2, 1, 3)),
            jnp.transpose(ssd_s.reshape(depth, SSD_HEADS, SSD_HEAD_DIM, SSD_STATE, DEC),
                          (0, 4, 1, 2, 3)))
```

```python
import functools

import jax
import jax.numpy as jnp
import numpy as np
from jax import lax
from jax.experimental import pallas as pl
from jax.experimental.pallas import tpu as pltpu

F32 = jnp.float32
BF16 = jnp.bfloat16

D_MODEL = 1024
N_META = 16
HEAD_DIM = 64
ATT_WIDTH = 512
ATT_HEADS = 8
KV_WIDTH = 128
S5_WIDTH = 256
S5_GROUPS = 16
S5_GROUP_CH = 16
S5_STATE = 64
S5_LANES = S5_GROUPS * S5_STATE
SSD_WIDTH = 256
SSD_HEADS = 4
SSD_HEAD_DIM = 64
SSD_STATE = 64
SSD_CONV = 4
SSD_CONV_DIM = 512
FFN_HIDDEN = 2816
NORM_EPS = 1e-6
ROPE_THETA = 10000.0
PAST_LEN = 8192
N_IN = 1796

BLK = 128
FRONT = BLK - N_META
TM = 512
SPECIAL = 4 * BLK
DEC = 128
DEC_ROW0 = 2 * BLK
DCH = 16
PROJ_W = 1920
Q0, K0, V0, U0, Z0, XBC0, DT0 = 0, 512, 640, 768, 1024, 1280, 1792
FFN_CHUNK = 256
SEG = BLK // 8
SC_ABAR_RE, SC_ABAR_IM, SC_SEG1, SC_SEG2, SC_SEG4, SC_CARRY, SC_ROWS = 0, 8, 16, 32, 48, 64, 80
NEG_INF = float("-inf")
MIB = 1024 * 1024


def _dot(a, b):
    return jnp.dot(a, b, preferred_element_type=F32)


def _dot_nt(a, b):
    return lax.dot_general(a, b, (((1,), (1,)), ((), ())), preferred_element_type=F32)


def _sigmoid(x):
    return 1.0 / (1.0 + jnp.exp(-x))


def _silu(x):
    return x * _sigmoid(x)


def _softplus(x):
    return jnp.maximum(x, 0.0) + jnp.log1p(jnp.exp(-jnp.abs(x)))


def _gelu_tanh(x):
    return x * _sigmoid((2.0 * 0.7978845608028654) * (x + 0.044715 * (x * x * x)))


def _rmsnorm(x, g):
    return x * lax.rsqrt(jnp.mean(x * x, axis=-1, keepdims=True) + NORM_EPS) * g


def _iota(shape, dim):
    return lax.broadcasted_iota(jnp.int32, shape, dim)


def _rope(x, cosv, sinv):
    first = (_iota(x.shape, 1) & 63) < 32
    partner = jnp.where(first, pltpu.roll(x, 96, 1), pltpu.roll(x, 32, 1))
    return x * cosv + partner * sinv


def _lane_bcast(x, widths):
    rows = x.shape[0]
    return jnp.concatenate(
        [jnp.broadcast_to(x[:, h:h + 1], (rows, w)) for h, w in enumerate(widths)], axis=1)


def _split3(x):
    hi = x.astype(BF16)
    r1 = x - hi.astype(F32)
    mid = r1.astype(BF16)
    lo = (r1 - mid.astype(F32)).astype(BF16)
    return hi, mid, lo


def _s5_prep_kernel(are_ref, aim_ref, ldt_ref, btre_ref, btim_ref, cre_in, cim_in,
                    sc_ref, bre_ref, bim_ref, cret_ref, cimt_ref, bret_ref, bimt_ref, cre_ref, cim_ref,
                    abart_ref, stage):
    ar = are_ref[0]
    ai = aim_ref[0]
    dt = jnp.exp(ldt_ref[0])

    def power(k):
        kf = k.astype(F32)
        mag = jnp.exp((kf * dt) * ar)
        ang = (kf * dt) * ai
        return mag * jnp.cos(ang), mag * jnp.sin(ang)

    pre, pim = power(jnp.full((8, S5_LANES), 1, jnp.int32))
    sc_ref[0, SC_ABAR_RE:SC_ABAR_RE + 8, :] = pre
    sc_ref[0, SC_ABAR_IM:SC_ABAR_IM + 8, :] = pim
    row = _iota((8, S5_LANES), 0)
    for d, base in ((1, SC_SEG1), (2, SC_SEG2), (4, SC_SEG4)):
        sre, sim = power(jnp.full((8, S5_LANES), SEG * d, jnp.int32))
        sc_ref[0, base:base + 8, :] = jnp.where(row >= d, sre, 0.0)
        sc_ref[0, base + 8:base + 16, :] = jnp.where(row >= d, sim, 0.0)
    qre, qim = power(SEG * (row + 1))
    sc_ref[0, SC_CARRY:SC_CARRY + 8, :] = qre
    sc_ref[0, SC_CARRY + 8:SC_CARRY + 16, :] = qim
    abar_re = pre[0:1]
    abar_im = pim[0:1]
    den = ar * ar + ai * ai
    xr = abar_re - 1.0
    f_re = (xr * ar + abar_im * ai) / den
    f_im = (abar_im * ar - xr * ai) / den
    br = btre_ref[0]
    bi = btim_ref[0]
    bb_re = f_re * br - f_im * bi
    bb_im = f_re * bi + f_im * br
    lane_group = _iota((S5_GROUP_CH, S5_LANES), 1) >> 6
    for g in range(S5_GROUPS):
        rows = slice(g * S5_GROUP_CH, (g + 1) * S5_GROUP_CH)
        own = lane_group == g
        stage[0, rows, :] = jnp.where(own, bb_re, 0.0)
        stage[1, rows, :] = jnp.where(own, bb_im, 0.0)
        stage[2, rows, :] = jnp.where(own, jnp.concatenate([cre_in[0, rows, :]] * S5_GROUPS, axis=1), 0.0)
        stage[3, rows, :] = jnp.where(own, jnp.concatenate([cim_in[0, rows, :]] * S5_GROUPS, axis=1), 0.0)
    for k, (direct, transposed) in enumerate(((bre_ref, bret_ref), (bim_ref, bimt_ref),
                                              (cret_ref, cre_ref), (cimt_ref, cim_ref))):
        direct[0] = stage[k].astype(BF16)
        transposed[0] = stage[k].T.astype(BF16)
    seq_lanes = abart_ref.shape[-1]
    abart_ref[0, 0] = jnp.broadcast_to(abar_re, (seq_lanes, S5_LANES)).T
    abart_ref[0, 1] = jnp.broadcast_to(abar_im, (seq_lanes, S5_LANES)).T


def _s5_prepare(a_re, a_im, log_dt, b_re, b_im, c_re, c_im):
    depth = a_re.shape[0]
    flat = lambda t: t.reshape(depth, 1, S5_LANES)
    ldt = jnp.repeat(log_dt, S5_STATE, axis=-1).reshape(depth, 1, S5_LANES)
    bt = lambda t: jnp.transpose(t, (0, 3, 1, 2)).reshape(depth, S5_GROUP_CH, S5_LANES)
    rows_c = lambda t: t.reshape(depth, S5_WIDTH, S5_STATE)
    spec = lambda *shape: pl.BlockSpec((1,) + shape, lambda l: (l,) + (0,) * len(shape))
    wide, tall = (S5_WIDTH, S5_LANES), (S5_LANES, S5_WIDTH)
    shape = lambda s, dt: jax.ShapeDtypeStruct((depth,) + s, dt)
    return pl.pallas_call(
        _s5_prep_kernel,
        grid=(depth,),
        in_specs=[spec(1, S5_LANES)] * 3 + [spec(S5_GROUP_CH, S5_LANES)] * 2 + [spec(S5_WIDTH, S5_STATE)] * 2,
        out_specs=[spec(SC_ROWS, S5_LANES)] + [spec(*wide)] * 4 + [spec(*tall)] * 4 + [spec(2, S5_LANES, DEC)],
        out_shape=[shape((SC_ROWS, S5_LANES), F32)] + [shape(wide, BF16)] * 4 + [shape(tall, BF16)] * 4
        + [shape((2, S5_LANES, DEC), F32)],
        scratch_shapes=[pltpu.VMEM((4,) + wide, F32)],
        name="s5_prep",
    )(flat(a_re), flat(a_im), ldt, bt(b_re), bt(b_im), rows_c(c_re), rows_c(c_im))


def _run_interleaved(*gens):
    live = list(gens)
    while live:
        live = [g for g in live if next(g, "done") != "done"]


def _tile_rows(special_ref, main_ref):
    return jnp.where(pl.program_id(0) == 0, special_ref[...], main_ref[...])


def _layer_resident(layer, shape):
    return pl.BlockSpec((None,) + shape, lambda i: (layer,) + (0,) * len(shape),
                        pipeline_mode=pl.Buffered(1))


def _special_spec():
    return pl.BlockSpec((TM, D_MODEL), lambda i: (0, 0), pipeline_mode=pl.Buffered(1))


def _main_spec():
    return pl.BlockSpec((TM, D_MODEL), lambda i: (jnp.maximum(i - 1, 0), 0))


def _outffn_kernel(*refs, final):
    mixs_ref, mixm_ref, xs_ref, xm_ref, wo_ref, g2_ref, wg_ref, wu_ref, wd_ref = refs[:9]
    x1 = _tile_rows(xs_ref, xm_ref) + _dot(_tile_rows(mixs_ref, mixm_ref), wo_ref[...])
    h = _rmsnorm(x1, g2_ref[...]).astype(BF16)
    acc = jnp.zeros((TM, D_MODEL), F32)
    for j in range(FFN_HIDDEN // FFN_CHUNK):
        sl = slice(j * FFN_CHUNK, (j + 1) * FFN_CHUNK)
        gate = _dot(h, wg_ref[:, sl])
        up = _dot(h, wu_ref[:, sl])
        acc = acc + _dot((_silu(gate) * up).astype(BF16), wd_ref[sl, :])
    out = x1 + acc
    special_tile = pl.program_id(0) == 0
    if final:
        gf_ref, outs_ref, outm_ref = refs[9:]
        out = _rmsnorm(out, gf_ref[...])
    else:
        f32_slabs = refs[9:12]
        outs_ref, outm_ref = refs[12:14]
        bf16_slabs = refs[14:17]
        for src, dst in zip(f32_slabs, bf16_slabs):
            dst[...] = src[...].astype(BF16)
    outm_ref[...] = out

    @pl.when(special_tile)
    def _():
        outs_ref[...] = out


def _outffn(mix_special, mix_main, x_special, x_main, wo, g2, ffn_w, layer, next_ffn_f32=None, gf=None):
    main_rows = x_main.shape[0]
    rows = SPECIAL + main_rows
    tiles = rows // TM
    resident = functools.partial(_layer_resident, layer)
    whole = lambda a: pl.BlockSpec(a.shape, lambda i: (0, 0), pipeline_mode=pl.Buffered(1))
    mix_main = mix_main.reshape(main_rows, D_MODEL)

    in_specs = [_special_spec(), _main_spec(), _special_spec(), _main_spec(),
                resident((D_MODEL, D_MODEL)), resident((1, D_MODEL))] + [whole(w) for w in ffn_w]
    out_specs = [pl.BlockSpec((TM, D_MODEL), lambda i: (0, 0)), _main_spec()]
    out_shape = [jax.ShapeDtypeStruct((SPECIAL, D_MODEL), F32),
                 jax.ShapeDtypeStruct((main_rows, D_MODEL), F32)]
    args = (mix_special, mix_main, x_special, x_main, wo, g2, *ffn_w)
    if gf is None:
        steps = tiles
        slab_in, slab_out = [], []
        for w in next_ffn_f32:
            w_rows, w_cols = w.shape[1:]
            n = max(d for d in range(1, steps + 1) if w_rows % (16 * d) == 0)
            clamp = lambda i, n=n: jnp.minimum(i, n - 1)
            slab_in.append(pl.BlockSpec((None, w_rows // n, w_cols),
                                        lambda i, clamp=clamp: (layer + 1, clamp(i), 0)))
            slab_out.append(pl.BlockSpec((w_rows // n, w_cols), lambda i, clamp=clamp: (clamp(i), 0)))
        res = pl.pallas_call(
            functools.partial(_outffn_kernel, final=False), grid=(steps,),
            in_specs=in_specs + slab_in,
            out_specs=out_specs + slab_out,
            out_shape=out_shape + [jax.ShapeDtypeStruct(w.shape[1:], BF16) for w in next_ffn_f32],
            compiler_params=pltpu.CompilerParams(dimension_semantics=("arbitrary",),
                                                 vmem_limit_bytes=52 * MIB),
            name="outffn",
        )(*args, *next_ffn_f32)
        return res[0], res[1], tuple(res[2:])
    return pl.pallas_call(
        functools.partial(_outffn_kernel, final=True), grid=(tiles,),
        in_specs=in_specs + [pl.BlockSpec((1, D_MODEL), lambda i: (0, 0))],
        out_specs=out_specs, out_shape=out_shape,
        compiler_params=pltpu.CompilerParams(dimension_semantics=("arbitrary",),
                                             vmem_limit_bytes=52 * MIB),
        name="outffn_final",
    )(*args, gf)


def _s5_tail(y, u, s5d_ref, gluw_ref, glub_ref, s5g_ref):
    y5 = _gelu_tanh(y + s5d_ref[...] * u)
    gate = _dot(y5.astype(BF16), gluw_ref[...]) + glub_ref[...]
    return _rmsnorm(y5 * _sigmoid(gate), s5g_ref[...])


def _pair_block_diag(t):
    left = _iota(t.shape, 1) < 64
    return jnp.concatenate([jnp.where(left, t, 0.0), jnp.where(left, 0.0, t)], axis=0)


N_MIXER_PARAMS = 18
PROJ_SPLIT = 1


def _prompt_mixer_kernel(*refs, layer, batch):
    sinks_all, xs_ref, xm_ref, g1_ref, win_ref = refs[:5]
    shared = refs[5:5 + N_MIXER_PARAMS]
    n_in = 5 + N_MIXER_PARAMS
    mixs_ref, mixm_ref = refs[n_in:n_in + 2]
    outs = refs[n_in + 2:n_in + 8]
    scratch = refs[n_in + 8:-1]
    proj_scr = refs[-1]
    c = pl.program_id(0)
    rows = batch * BLK

    @pl.when(c == 0)
    def _():
        for s in scratch:
            s[...] = jnp.zeros_like(s)
        zero_rows = (_iota((rows, 1), 0) & (BLK - 1)) < FRONT
        h0 = jnp.where(zero_rows, 0.0, _rmsnorm(xs_ref[...].reshape(rows, D_MODEL), g1_ref[...]))
        proj_scr[...] = _dot(h0.astype(BF16), win_ref[...]).reshape(batch, BLK, PROJ_W)

    handover = {}

    def next_projection():
        hn = _rmsnorm(xm_ref[...].reshape(rows, D_MODEL), g1_ref[...]).astype(BF16)
        yield
        for k in range(PROJ_SPLIT):
            cols = slice(k * (PROJ_W // PROJ_SPLIT), (k + 1) * (PROJ_W // PROJ_SPLIT))
            handover[k] = _dot(hn, win_ref[:, cols])
            yield
            yield

    _run_interleaved(next_projection(), *(
        mixer(c, sinks_all.at[layer], proj_scr.at[b], *shared, mixm_ref.at[b],
              *(o.at[b] for o in outs), *(s.at[b] for s in scratch))
        for b in range(batch) for mixer in (_prompt_attention, _prompt_ssd, _prompt_s5)))
    for k in range(PROJ_SPLIT):
        cols = slice(k * (PROJ_W // PROJ_SPLIT), (k + 1) * (PROJ_W // PROJ_SPLIT))
        proj_scr[:, :, cols] = handover[k].reshape(batch, BLK, PROJ_W // PROJ_SPLIT)

    @pl.when(c == 0)
    def _():
        mixs_ref[0:batch] = mixm_ref[...]
        mixs_ref[batch:] = jnp.zeros((mixs_ref.shape[0] - batch, BLK, D_MODEL), mixs_ref.dtype)


def _prompt_attention(
        c, sinks_ref, proj_ref, cos_ref, sin_ref, attg_ref,
        bre_ref, bim_ref, cre_ref, cim_ref, sc_ref, s5d_ref, gluw_ref, glub_ref, s5g_ref,
        convw_ref, convb_ref, dtb_ref, alog_ref, dssd_ref, ssdg_ref,
        mix_ref, kout_ref, vout_ref, s5re_ref, s5im_ref, convout_ref, ssdout_ref,
        kprev, vprev, hre, him, carry_re, carry_im, xbuf, hssd, rows_scr):
    cosv = cos_ref[...]
    sinv = sin_ref[...]
    k_rot = _rope(proj_ref[:, K0:K0 + KV_WIDTH], cosv, sinv)
    vt_new = proj_ref[:, V0:V0 + KV_WIDTH].T
    kout_ref[...] = k_rot.T
    vout_ref[...] = vt_new
    kk = jnp.concatenate([kprev[...], k_rot], axis=0).astype(BF16)
    vvt = jnp.concatenate([vprev[...], vt_new], axis=1).astype(BF16)
    kprev[...] = k_rot
    vprev[...] = vt_new

    qts = [(_rope(proj_ref[:, Q0 + j * 128:Q0 + (j + 1) * 128], cosv, sinv) * (HEAD_DIM ** -0.5)).T
           for j in range(ATT_HEADS // 2)]
    krow = _iota((2 * BLK, BLK), 0)
    qcol = _iota((2 * BLK, BLK), 1)
    kpos = (c - 1) * BLK + krow - FRONT
    ok = (krow >= qcol) & (krow <= qcol + BLK) & (kpos >= 0)
    bias1 = jnp.where(ok, 0.0, NEG_INF)
    bias = jnp.concatenate([bias1] * 4, axis=1)
    zero_half = jnp.zeros((HEAD_DIM, BLK), F32)
    yield

    o_rows = []
    for g in range(2):
        blocks = []
        for r in range(4):
            h = 4 * g + r
            qh = qts[h // 2][(h % 2) * HEAD_DIM:(h % 2 + 1) * HEAD_DIM, :]
            blocks.append(jnp.concatenate([qh, zero_half] if g == 0 else [zero_half, qh], axis=0))
        qg = jnp.concatenate(blocks, axis=1).astype(BF16)
        s = _dot(kk, qg) + bias
        sink = jnp.concatenate(
            [jnp.full((1, BLK), sinks_ref[4 * g + r], F32) for r in range(4)], axis=1)
        yield
        m = jnp.maximum(jnp.max(s, axis=0, keepdims=True), sink)
        p = jnp.exp(s - m)
        yield
        inv_den = 1.0 / (jnp.sum(p, axis=0, keepdims=True) + jnp.exp(sink - m))
        og = _dot(vvt, p.astype(BF16))
        for r in range(4):
            cols = slice(r * BLK, (r + 1) * BLK)
            o_rows.append(og[g * HEAD_DIM:(g + 1) * HEAD_DIM, cols] * inv_den[:, cols])
        yield
    o_att = jnp.concatenate(o_rows, axis=0).T
    mix_ref[:, 0:ATT_WIDTH] = _rmsnorm(o_att, attg_ref[...]).astype(mix_ref.dtype)


def _prompt_s5(
        c, sinks_ref, proj_ref, cos_ref, sin_ref, attg_ref,
        bre_ref, bim_ref, cre_ref, cim_ref, sc_ref, s5d_ref, gluw_ref, glub_ref, s5g_ref,
        convw_ref, convb_ref, dtb_ref, alog_ref, dssd_ref, ssdg_ref,
        mix_ref, kout_ref, vout_ref, s5re_ref, s5im_ref, convout_ref, ssdout_ref,
        kprev, vprev, hre, him, carry_re, carry_im, xbuf, hssd, rows_scr):
    for j in range(S5_WIDTH // 128):
        rows_scr[j] = proj_ref[:, U0 + j * 128:U0 + (j + 1) * 128]
    u = jnp.concatenate(
        [jnp.concatenate([rows_scr[j, pl.ds(t, 8, stride=SEG), :] for j in range(S5_WIDTH // 128)], axis=1)
         for t in range(SEG)], axis=0)
    ub = u.astype(BF16)
    hre[...] = _dot(ub, bre_ref[...])
    him[...] = _dot(ub, bim_ref[...])
    yield
    first_seg = _iota((8, 128), 0) == 0
    for lt in range(S5_LANES // 128):
        ls = slice(lt * 128, (lt + 1) * 128)
        a_r = sc_ref[SC_ABAR_RE:SC_ABAR_RE + 8, ls]
        a_i = sc_ref[SC_ABAR_IM:SC_ABAR_IM + 8, ls]
        er = hre[0:8, ls]
        ei = him[0:8, ls]
        for t in range(1, SEG):
            rs = slice(8 * t, 8 * t + 8)
            er, ei = hre[rs, ls] + a_r * er - a_i * ei, him[rs, ls] + a_r * ei + a_i * er
        yield
        for d, b0 in ((1, SC_SEG1), (2, SC_SEG2), (4, SC_SEG4)):
            s_r = sc_ref[b0:b0 + 8, ls]
            s_i = sc_ref[b0 + 8:b0 + 16, ls]
            pr = pltpu.roll(er, d, 0)
            pi = pltpu.roll(ei, d, 0)
            er, ei = er + s_r * pr - s_i * pi, ei + s_r * pi + s_i * pr
        cr = carry_re[:, ls]
        ci = carry_im[:, ls]
        q_r = sc_ref[SC_CARRY:SC_CARRY + 8, ls]
        q_i = sc_ref[SC_CARRY + 8:SC_CARRY + 16, ls]
        tr = er + q_r * cr - q_i * ci
        ti = ei + q_r * ci + q_i * cr
        hr = jnp.where(first_seg, cr, pltpu.roll(tr, 1, 0))
        hi = jnp.where(first_seg, ci, pltpu.roll(ti, 1, 0))
        carry_re[:, ls] = jnp.broadcast_to(tr[7:8], (8, 128))
        carry_im[:, ls] = jnp.broadcast_to(ti[7:8], (8, 128))
        for t in range(SEG):
            rs = slice(8 * t, 8 * t + 8)
            hr, hi = hre[rs, ls] + a_r * hr - a_i * hi, him[rs, ls] + a_r * hi + a_i * hr
            hre[rs, ls] = hr
            him[rs, ls] = hi
        yield
    s5re_ref[...] = carry_re[0:1, :]
    s5im_ref[...] = carry_im[0:1, :]
    y_perm = _dot(hre[...].astype(BF16), cre_ref[...]) - _dot(him[...].astype(BF16), cim_ref[...])
    yield
    o_perm = _s5_tail(y_perm, u, s5d_ref, gluw_ref, glub_ref, s5g_ref)
    for j in range(S5_WIDTH // 128):
        for t in range(SEG):
            rows_scr[j, pl.ds(t, 8, stride=SEG), :] = o_perm[8 * t:8 * t + 8, j * 128:(j + 1) * 128]
        mix_ref[:, ATT_WIDTH + j * 128:ATT_WIDTH + (j + 1) * 128] = rows_scr[j].astype(mix_ref.dtype)


def _prompt_ssd(
        c, sinks_ref, proj_ref, cos_ref, sin_ref, attg_ref,
        bre_ref, bim_ref, cre_ref, cim_ref, sc_ref, s5d_ref, gluw_ref, glub_ref, s5g_ref,
        convw_ref, convb_ref, dtb_ref, alog_ref, dssd_ref, ssdg_ref,
        mix_ref, kout_ref, vout_ref, s5re_ref, s5im_ref, convout_ref, ssdout_ref,
        kprev, vprev, hre, him, carry_re, carry_im, xbuf, hssd, rows_scr):
    xbc = proj_ref[:, XBC0:XBC0 + SSD_CONV_DIM]
    xbuf[8:8 + BLK, :] = xbc
    conv = convb_ref[...]
    for j in range(SSD_CONV):
        conv = conv + xbuf[5 + j:5 + j + BLK, :] * convw_ref[j:j + 1, :]
    yield
    xc = _silu(conv)
    convout_ref[...] = xbc[BLK - (SSD_CONV - 1):BLK, :]
    xbuf[0:8, :] = xbc[BLK - 8:BLK, :]
    xs = xc[:, 0:SSD_WIDTH]
    bm = xc[:, SSD_WIDTH:SSD_WIDTH + 128]
    cm = xc[:, SSD_WIDTH + 128:SSD_WIDTH + 256]
    yield

    lane = _iota((BLK, 128), 1)
    row = _iota((BLK, 128), 0)
    head_r = _iota((8, BLK), 0)
    time_c = _iota((8, BLK), 1)
    live = (head_r < SSD_HEADS) & (time_c >= jnp.where(c == 0, FRONT, 0))
    raw_t = proj_ref[:, DT0:DT0 + 128].T[0:8, :]
    dt_t = jnp.where(live, _softplus(raw_t + dtb_ref[...]), 0.0)
    dta_t = dt_t * (-jnp.exp(alog_ref[...]))
    causal = lane <= row
    triu = jnp.where(row <= lane, 1.0, 0.0).astype(BF16)
    hi3, mid3, lo3 = _split3(dta_t)
    cs_t = _dot(hi3, triu) + _dot(mid3, triu) + _dot(lo3, triu)
    yield
    cs_last = cs_t[:, BLK - 1:BLK]
    packed = jnp.concatenate(
        [dt_t, dt_t * jnp.exp(cs_last - cs_t), jnp.exp(cs_t), cs_t, jnp.zeros((BLK - 32, BLK), F32)], axis=0)
    cols = packed.T
    heads64 = (SSD_HEAD_DIM,) * SSD_HEADS
    xd = xs * _lane_bcast(cols[:, 0:SSD_HEADS], heads64)
    dxd = xs * _lane_bcast(cols[:, 8:8 + SSD_HEADS], heads64)
    ecs = _lane_bcast(cols[:, 16:16 + SSD_HEADS], heads64)
    cs = cols[:, 24:24 + SSD_HEADS]
    yield

    bmb = bm.astype(BF16)
    left = lane < 64
    cb = (_dot_nt(jnp.where(left, cm, 0.0).astype(BF16), bmb),
          _dot_nt(jnp.where(left, 0.0, cm).astype(BF16), bmb))
    scores = []
    for h in range(SSD_HEADS):
        seg = cs[:, h:h + 1] - cs_t[h:h + 1, :]
        scores.append((cb[h // 2] * jnp.exp(jnp.where(causal, seg, NEG_INF))).astype(BF16))
        yield
    y_diag = jnp.concatenate(
        [_dot(jnp.concatenate([scores[2 * j], scores[2 * j + 1]], axis=1),
              _pair_block_diag(xd[:, j * 128:(j + 1) * 128]).astype(BF16)) for j in range(2)], axis=1)
    yield

    h_prev = hssd[...]
    y_off = _dot_nt(cm.astype(BF16), h_prev.astype(BF16)) * ecs
    yield
    states = _dot(dxd.T.astype(BF16), bmb)
    own = (_iota((SSD_WIDTH, 128), 0) >> 7) == (_iota((SSD_WIDTH, 128), 1) >> 6)
    cd = jnp.exp(cs_last)
    cdm = jnp.concatenate(
        [jnp.broadcast_to(cd[h:h + 1, :], (SSD_HEAD_DIM, 128)) for h in range(SSD_HEADS)], axis=0)
    h_new = cdm * h_prev + jnp.where(own, states, 0.0)
    hssd[...] = h_new
    for h in range(SSD_HEADS):
        g0 = (h // 2) * SSD_STATE
        ssdout_ref[h] = h_new[h * SSD_HEAD_DIM:(h + 1) * SSD_HEAD_DIM, g0:g0 + SSD_STATE]
    yield

    yssd = y_diag + y_off + dssd_ref[...] * xs
    yc = yssd * _silu(proj_ref[:, Z0:Z0 + SSD_WIDTH])
    mix_ref[:, ATT_WIDTH + S5_WIDTH:] = _rmsnorm(yc, ssdg_ref[...]).astype(mix_ref.dtype)


def _prompt_mixers(x_special, x_main, pp, tabs, batch, nc, layer):
    const = lambda shape: pl.BlockSpec((None,) + shape, lambda c: (layer,) + (0,) * len(shape))
    whole = lambda shape: pl.BlockSpec(shape, lambda c: (0,) * len(shape))
    in_specs = [
        pl.BlockSpec(memory_space=pltpu.SMEM),
        pl.BlockSpec((batch, BLK, D_MODEL), lambda c: (0, 0, 0)),
        pl.BlockSpec((batch, BLK, D_MODEL), lambda c: (0, jnp.minimum(c, nc - 2), 0)),
        const((1, D_MODEL)),
        pl.BlockSpec((None, D_MODEL, PROJ_W), lambda c: (layer, 0, 0), pipeline_mode=pl.Buffered(1)),
        pl.BlockSpec((BLK, 128), lambda c: (c, 0)),
        pl.BlockSpec((BLK, 128), lambda c: (c, 0)),
        const((1, ATT_WIDTH)),
        const((S5_WIDTH, S5_LANES)), const((S5_WIDTH, S5_LANES)),
        const((S5_LANES, S5_WIDTH)), const((S5_LANES, S5_WIDTH)),
        const((SC_ROWS, S5_LANES)), const((1, S5_WIDTH)), const((S5_WIDTH, S5_WIDTH)),
        const((1, S5_WIDTH)), const((1, S5_WIDTH)),
        const((SSD_CONV, SSD_CONV_DIM)), const((1, SSD_CONV_DIM)),
        const((8, BLK)), const((8, BLK)), const((1, SSD_WIDTH)), const((1, SSD_WIDTH)),
    ]
    state_shapes =[(BLK, KV_WIDTH), (BLK, KV_WIDTH), (1, S5_LANES), (1, S5_LANES),
                    (SSD_CONV - 1, SSD_CONV_DIM), (SSD_HEADS, SSD_HEAD_DIM, SSD_STATE)]
    out_specs = [
        pl.BlockSpec((SPECIAL // BLK, BLK, D_MODEL), lambda c: (0, 0, 0)),
        pl.BlockSpec((batch, BLK, D_MODEL), lambda c: (0, jnp.maximum(c - 1, 0), 0)),
    ] + [whole((batch,) + s) for s in state_shapes]
    out_shape = [
        jax.ShapeDtypeStruct((SPECIAL // BLK, BLK, D_MODEL), BF16),
        jax.ShapeDtypeStruct((batch, (nc - 1) * BLK, D_MODEL), BF16),
    ] + [jax.ShapeDtypeStruct((batch,) + s, F32) for s in state_shapes]
    scratch = [pltpu.VMEM((batch,) + s, F32) for s in (
        (BLK, KV_WIDTH), (BLK, KV_WIDTH), (BLK, S5_LANES), (BLK, S5_LANES),
        (8, S5_LANES), (8, S5_LANES), (BLK + 8, SSD_CONV_DIM), (SSD_WIDTH, 128),
        (S5_WIDTH // 128, BLK, 128))] + [pltpu.VMEM((batch, BLK, PROJ_W), F32)]
    res = pl.pallas_call(
        functools.partial(_prompt_mixer_kernel, layer=layer, batch=batch),
        grid=(nc,),
        in_specs=in_specs, out_specs=out_specs, out_shape=out_shape, scratch_shapes=scratch,
        compiler_params=pltpu.CompilerParams(dimension_semantics=("arbitrary",),
                                             vmem_limit_bytes=48 * MIB),
        name="prompt_mixers",
    )(pp["sinks"], x_special.reshape(SPECIAL // BLK, BLK, D_MODEL),
      x_main.reshape(batch, (nc - 1) * BLK, D_MODEL), pp["ln1"], pp["w_in"],
      tabs["cos_p"], tabs["sin_p"], pp["attn_out_g"],
      pp["bre"], pp["bim"], pp["cre"], pp["cim"], pp["sc"], pp["s5_d"], pp["glu_w"], pp["glu_b"],
      pp["s5_out_g"], pp["conv_w"], pp["conv_b"], pp["dt_bias_c"], pp["a_log_c"], pp["ssd_d"],
      pp["ssd_norm_g"])
    return [res[0].reshape(SPECIAL, D_MODEL)] + list(res[1:])


def _decode_mixer_kernel(*refs, layer, chained):
    n_in = 29 + (6 if chained else 0)
    (sinks_all, _, x_ref, cos_ref, sin_ref, attg_ref,
     bret_ref, bimt_ref, cret_ref, cimt_ref, abar_ref, s5d_ref, gluw_ref, glub_ref, s5g_ref,
     convw_ref, convb_ref, dtb_ref, alog_ref, dssd_ref, ssdg_ref, g1_ref, win_ref,
     kt_ref, vt_ref, s5re_in, s5im_in, conv_in, ssd_in) = refs[:29]
    (mix_ref, ktout_ref, vtout_ref, s5re_ref, s5im_ref, convout_ref, ssdout_ref,
     oatt_scr, xs_scr, xdt_scr, bt_scr, ct_scr, dcyt_scr, yt_scr, proj_ref) = refs[n_in:]
    sinks_ref = sinks_all.at[layer]
    i = pl.program_id(0)
    heads64 = (SSD_HEAD_DIM,) * SSD_HEADS

    @pl.when(i == 0)
    def _():
        mix_ref[DEC:, :] = jnp.zeros((mix_ref.shape[0] - DEC, D_MODEL), mix_ref.dtype)
        proj_ref[...] = _dot(_rmsnorm(x_ref[...], g1_ref[...]).astype(BF16), win_ref[...])

        u = proj_ref[:, U0:U0 + S5_WIDTH]
        ut = u.T.astype(BF16)
        a_re = abar_ref[0]
        a_im = abar_ref[1]
        h0r = s5re_in[...]
        h0i = s5im_in[...]
        hr = _dot(bret_ref[...], ut) + a_re * h0r - a_im * h0i
        hi = _dot(bimt_ref[...], ut) + a_re * h0i + a_im * h0r
        s5re_ref[...] = hr
        s5im_ref[...] = hi
        yt = _dot(cret_ref[...], hr.astype(BF16)) - _dot(cimt_ref[...], hi.astype(BF16))
        mix_ref[0:DEC, ATT_WIDTH:ATT_WIDTH + S5_WIDTH] = _s5_tail(
            yt.T, u, s5d_ref, gluw_ref, glub_ref, s5g_ref).astype(mix_ref.dtype)

        xbc = proj_ref[:, XBC0:XBC0 + SSD_CONV_DIM]
        conv = convb_ref[...]
        for j in range(SSD_CONV - 1):
            conv = conv + conv_in[j] * convw_ref[j:j + 1, :]
        conv = conv + xbc * convw_ref[SSD_CONV - 1:SSD_CONV, :]
        convout_ref[0] = conv_in[1]
        convout_ref[1] = conv_in[2]
        convout_ref[2] = xbc
        xc = _silu(conv)
        xs = xc[:, 0:SSD_WIDTH]
        dt = _softplus(proj_ref[:, DT0:DT0 + 128] + dtb_ref[...])
        decay = jnp.exp(dt * (-jnp.exp(alog_ref[...])))
        xs_scr[...] = xs
        xdt_scr[...] = (xs * _lane_bcast(dt, heads64)).T
        bt_scr[...] = xc[:, SSD_WIDTH:SSD_WIDTH + 128].T
        ct_scr[...] = xc[:, SSD_WIDTH + 128:SSD_WIDTH + 256].T
        dcyt_scr[...] = decay.T[0:8, :]

    r0 = pl.multiple_of(i * DCH, DCH)
    cosv = cos_ref[...]
    sinv = sin_ref[...]
    k_rot = _rope(proj_ref[pl.ds(r0, DCH), K0:K0 + KV_WIDTH], cosv, sinv)
    v_new = proj_ref[pl.ds(r0, DCH), V0:V0 + KV_WIDTH]
    pad_rows = jnp.zeros((BLK - DCH, KV_WIDTH), F32)
    knew_t = jnp.concatenate([k_rot, pad_rows], axis=0).T
    vnew_t = jnp.concatenate([v_new, pad_rows], axis=0).T
    last = _iota((KV_WIDTH, BLK), 1) == BLK - 1
    for b in range(DCH):
        ktout_ref[b] = jnp.where(last, knew_t[:, b:b + 1], pltpu.roll(kt_ref[b], BLK - 1, 1))
        vtout_ref[b] = jnp.where(last, vnew_t[:, b:b + 1], pltpu.roll(vt_ref[b], BLK - 1, 1))
    left = _iota((DCH, 128), 1) < 64
    qs = []
    for h in range(ATT_HEADS):
        j, e, g = h // 2, h % 2, h // 4
        qt = _rope(proj_ref[pl.ds(r0, DCH), Q0 + j * 128:Q0 + (j + 1) * 128], cosv, sinv)
        qt = qt * (HEAD_DIM ** -0.5)
        if e != g:
            qt = pltpu.roll(qt, 64, 1)
        qs.append(jnp.where(left == (g == 0), qt, 0.0))
    qx = jnp.concatenate(qs, axis=0).astype(BF16)
    kt_cat = jnp.concatenate([kt_ref[b] for b in range(DCH)], axis=1).astype(BF16)
    vt_cat = jnp.concatenate([vt_ref[b] for b in range(DCH)], axis=1).astype(BF16)
    s_old = _dot(qx, kt_cat)
    s_new = _dot_nt(qx, k_rot.astype(BF16))
    rseq = _iota((ATT_HEADS * DCH, DCH * BLK), 0) & (DCH - 1)
    same = rseq == (_iota((ATT_HEADS * DCH, DCH * BLK), 1) >> 7)
    same_new = (_iota((ATT_HEADS * DCH, DCH), 0) & (DCH - 1)) == _iota((ATT_HEADS * DCH, DCH), 1)
    s_old = jnp.where(same, s_old, NEG_INF)
    s_new = jnp.where(same_new, s_new, NEG_INF)
    sink = jnp.concatenate(
        [jnp.full((DCH, 1), sinks_ref[h], F32) for h in range(ATT_HEADS)], axis=0)
    m = jnp.maximum(jnp.maximum(jnp.max(s_old, axis=-1, keepdims=True),
                                jnp.max(s_new, axis=-1, keepdims=True)), sink)
    p_old = jnp.exp(s_old - m)
    p_new = jnp.exp(s_new - m)
    den = (jnp.sum(p_old, axis=-1, keepdims=True) + jnp.sum(p_new, axis=-1, keepdims=True)
           + jnp.exp(sink - m))
    o = (_dot_nt((p_old / den).astype(BF16), vt_cat)
         + _dot((p_new / den).astype(BF16), v_new.astype(BF16)))
    o_tiles = []
    for j in range(ATT_HEADS // 2):
        g = j // 2
        a = o[(2 * j) * DCH:(2 * j + 1) * DCH]
        bb = o[(2 * j + 1) * DCH:(2 * j + 2) * DCH]
        if g == 1:
            a = pltpu.roll(a, 64, 1)
        else:
            bb = pltpu.roll(bb, 64, 1)
        o_tiles.append(jnp.where(left, a, bb))
    oatt_scr[pl.ds(r0, DCH), :] = jnp.concatenate(o_tiles, axis=1)

    rows_per_step = SSD_WIDTH // (DEC // DCH)
    head = i // (SSD_HEAD_DIM // rows_per_step)
    g0 = pl.multiple_of((head // 2) * SSD_STATE, SSD_STATE)
    dcy = dcyt_scr[pl.ds(head, 1), :]
    btg = bt_scr[pl.ds(g0, SSD_STATE), :]
    ctg = ct_scr[pl.ds(g0, SSD_STATE), :]
    for rr in range(rows_per_step):
        row = i * rows_per_step + rr
        h_new = dcy * ssd_in[rr] + xdt_scr[pl.ds(row, 1), :] * btg
        ssdout_ref[rr] = h_new
        yt_scr[pl.ds(row, 1), :] = jnp.sum(ctg * h_new, axis=0, keepdims=True)

    @pl.when(i == DEC // DCH - 1)
    def _():
        mix_ref[0:DEC, 0:ATT_WIDTH] = _rmsnorm(oatt_scr[...], attg_ref[...]).astype(mix_ref.dtype)
        xs = xs_scr[...]
        yssd = yt_scr[...].T + dssd_ref[...] * xs
        yc = yssd * _silu(proj_ref[:, Z0:Z0 + SSD_WIDTH])
        mix_ref[0:DEC, ATT_WIDTH + S5_WIDTH:] = _rmsnorm(yc, ssdg_ref[...]).astype(mix_ref.dtype)


def _decode_mixers(mix, x_special, pp, tabs, states, prev, layer):
    depth = states[0].shape[0]
    nsteps = DEC // DCH
    chained = prev is not None
    const = lambda shape: pl.BlockSpec((None,) + shape, lambda i: (layer,) + (0,) * len(shape))
    plain = lambda shape: pl.BlockSpec(shape, lambda i: (0,) * len(shape))
    rows_per_step = SSD_WIDTH // nsteps
    state_specs = [
        pl.BlockSpec((None, DCH, KV_WIDTH, BLK), lambda i: (layer, i, 0, 0)),
        pl.BlockSpec((None, DCH, KV_WIDTH, BLK), lambda i: (layer, i, 0, 0)),
        const((S5_LANES, DEC)), const((S5_LANES, DEC)),
        const((SSD_CONV - 1, DEC, SSD_CONV_DIM)),
        pl.BlockSpec((None, rows_per_step, SSD_STATE, DEC), lambda i: (layer, i, 0, 0)),
    ]
    in_specs = [
        pl.BlockSpec(memory_space=pltpu.SMEM),
        pl.BlockSpec(memory_space=pl.ANY),
        pl.BlockSpec((DEC, D_MODEL), lambda i: (DEC_ROW0 // DEC, 0)),
        plain((1, 128)), plain((1, 128)),
        const((1, ATT_WIDTH)),
        const((S5_LANES, S5_WIDTH)), const((S5_LANES, S5_WIDTH)),
        const((S5_WIDTH, S5_LANES)), const((S5_WIDTH, S5_LANES)),
        const((2, S5_LANES, DEC)), const((1, S5_WIDTH)), const((S5_WIDTH, S5_WIDTH)),
        const((1, S5_WIDTH)), const((1, S5_WIDTH)),
        const((SSD_CONV, SSD_CONV_DIM)), const((1, SSD_CONV_DIM)),
        const((1, 128)), const((1, 128)), const((1, SSD_WIDTH)), const((1, SSD_WIDTH)),
        const((1, D_MODEL)), const((D_MODEL, PROJ_W)),
    ] + state_specs + ([pl.BlockSpec(memory_space=pl.ANY)] * 6 if chained else [])
    out_specs = [pl.BlockSpec((2 * DEC, D_MODEL), lambda i: (DEC_ROW0 // (2 * DEC), 0))] + state_specs
    out_shape = [jax.ShapeDtypeStruct(mix.shape, mix.dtype)] + [
        jax.ShapeDtypeStruct(s.shape, F32) for s in states]
    scratch = [
        pltpu.VMEM((DEC, ATT_WIDTH), F32), pltpu.VMEM((DEC, SSD_WIDTH), F32),
        pltpu.VMEM((SSD_WIDTH, DEC), F32), pltpu.VMEM((128, DEC), F32), pltpu.VMEM((128, DEC), F32),
        pltpu.VMEM((8, DEC), F32), pltpu.VMEM((SSD_WIDTH, DEC), F32),
        pltpu.VMEM((DEC, PROJ_W), F32),
    ]
    aliases = {1: 0}
    if chained:
        aliases.update({29 + k: 1 + k for k in range(6)})
    assert depth > layer
    return pl.pallas_call(
        functools.partial(_decode_mixer_kernel, layer=layer, chained=chained),
        grid=(nsteps,),
        in_specs=in_specs, out_specs=out_specs, out_shape=out_shape,
        scratch_shapes=scratch,
        input_output_aliases=aliases,
        compiler_params=pltpu.CompilerParams(dimension_semantics=("arbitrary",),
                                             vmem_limit_bytes=40 * MIB),
        name="decode_mixers",
    )(pp["sinks"], mix, x_special, tabs["cos_d"], tabs["sin_d"], pp["attn_out_g"],
      pp["bre_t"], pp["bim_t"], pp["cre_t"], pp["cim_t"], pp["abar_t"], pp["s5_d"], pp["glu_w"],
      pp["glu_b"], pp["s5_out_g"], pp["conv_w"], pp["conv_b"], pp["dt_bias"], pp["a_log"],
      pp["ssd_d"], pp["ssd_norm_g"], pp["ln1"], pp["w_in"], *states, *(prev if chained else ()))


def _rope_tables(nc):
    half = HEAD_DIM // 2
    inv = ROPE_THETA ** (-np.arange(half, dtype=np.float64) / half)

    def tab(pos):
        ang = pos.astype(np.float64)[:, None] * inv[None, :]
        cos = np.cos(ang).astype(np.float32)
        sin = np.sin(ang).astype(np.float32)
        return np.tile(cos, (1, 4)), np.concatenate([-sin, sin, -sin, sin], axis=1)

    cos_p, sin_p = tab(np.arange(nc * BLK, dtype=np.int32) - FRONT)
    cos_d, sin_d = tab(np.full((1,), PAST_LEN, dtype=np.int32))
    return {"cos_p": cos_p, "sin_p": sin_p, "cos_d": cos_d, "sin_d": sin_d}


def kernel(x_prompt, x_sample, cache_k, cache_v, state_s5_re, state_s5_im, state_ssd_conv, state_ssd,
           meta_tokens, ln1_g, w_in, attn_sinks, attn_out_g, s5_a_re, s5_a_im, s5_log_dt,
           s5_b_re, s5_b_im, s5_c_re, s5_c_im, s5_d, s5_glu_w, s5_glu_b, s5_out_g,
           ssd_conv_w, ssd_conv_b, ssd_dt_bias, ssd_a_log, ssd_d, ssd_norm_g, w_out,
           ln2_g, w_gate, w_up, w_down, lnf_g):
    batch, seq, _ = x_prompt.shape
    depth = w_in.shape[0]
    assert batch == 2 and x_sample.shape[0] == DEC and x_sample.shape[1] == 1
    assert seq % TM == 0 and cache_k.shape[2] == BLK
    nc = seq // BLK + 1

    zeros_front = jnp.zeros((FRONT, D_MODEL), F32)
    x_special = jnp.concatenate([zeros_front, meta_tokens, zeros_front, meta_tokens,
                                 x_sample.reshape(DEC, D_MODEL), jnp.zeros((BLK, D_MODEL), F32)], axis=0)
    x_main = x_prompt.reshape(batch * seq, D_MODEL)

    tabs = _rope_tables(nc)
    sc, bre, bim, cre_t, cim_t, bre_t, bim_t, cre, cim, abar_t = _s5_prepare(
        s5_a_re, s5_a_im, s5_log_dt, s5_b_re, s5_b_im, s5_c_re, s5_c_im)
    w_in_p = jnp.pad(w_in, ((0, 0), (0, 0), (0, PROJ_W - N_IN))).astype(BF16)
    w_out_b = w_out.astype(BF16)
    ffn_f32 = (w_gate, w_up, w_down)
    ffn_w = tuple(t[0].astype(BF16) for t in ffn_f32)
    pad_heads = lambda t: jnp.pad(t, ((0, 0), (0, 128 - SSD_HEADS)))[:, None, :]
    head_rows = lambda t: jnp.broadcast_to(
        jnp.pad(t, ((0, 0), (0, 8 - SSD_HEADS)))[:, :, None], (depth, 8, BLK))
    row = lambda t: t[:, None, :]
    pp = {
        "sinks": attn_sinks, "attn_out_g": row(attn_out_g), "ln1": row(ln1_g), "w_in": w_in_p,
        "bre": bre, "bim": bim, "cre": cre, "cim": cim, "sc": sc,
        "bre_t": bre_t, "bim_t": bim_t, "cre_t": cre_t, "cim_t": cim_t, "abar_t": abar_t,
        "s5_d": row(s5_d), "glu_w": s5_glu_w.astype(BF16), "glu_b": row(s5_glu_b),
        "s5_out_g": row(s5_out_g),
        "conv_w": ssd_conv_w, "conv_b": row(ssd_conv_b),
        "dt_bias": pad_heads(ssd_dt_bias), "a_log": pad_heads(ssd_a_log),
        "dt_bias_c": head_rows(ssd_dt_bias), "a_log_c": head_rows(ssd_a_log),
        "ssd_d": row(jnp.repeat(ssd_d, SSD_HEAD_DIM, axis=-1)), "ssd_norm_g": row(ssd_norm_g),
    }
    ln2_r = row(ln2_g)

    states = (
        jnp.transpose(cache_k, (0, 1, 3, 4, 2)).reshape(depth, DEC, KV_WIDTH, BLK),
        jnp.transpose(cache_v, (0, 1, 3, 4, 2)).reshape(depth, DEC, KV_WIDTH, BLK),
        jnp.transpose(state_s5_re, (0, 2, 3, 1)).reshape(depth, S5_LANES, DEC),
        jnp.transpose(state_s5_im, (0, 2, 3, 1)).reshape(depth, S5_LANES, DEC),
        jnp.transpose(state_ssd_conv, (0, 2, 1, 3)),
        jnp.transpose(state_ssd, (0, 2, 3, 4, 1)).reshape(depth, SSD_WIDTH, SSD_STATE, DEC),
    )

    outs_p = [[] for _ in range(6)]
    outs_s = None
    y_special = y_main = None
    for l in range(depth):
        res_p = _prompt_mixers(x_special, x_main, pp, tabs, batch, nc, l)
        res_s = _decode_mixers(res_p[0], x_special, pp, tabs, states, outs_s, l)
        outs_s = res_s[1:]
        for i in range(6):
            outs_p[i].append(res_p[i + 2])
        ffn_args = (res_s[0], res_p[1], x_special, x_main, w_out_b, ln2_r, ffn_w, l)
        if l + 1 < depth:
            x_special, x_main, ffn_w = _outffn(*ffn_args, next_ffn_f32=ffn_f32)
        else:
            y_special, y_main = _outffn(*ffn_args, gf=lnf_g[None, :])

    y_prompt = y_main.reshape(batch, seq, D_MODEL)
    y_sample = y_special[DEC_ROW0:DEC_ROW0 + DEC].reshape(DEC, 1, D_MODEL)
    kv_p = lambda ts: jnp.transpose(
        jnp.stack(ts).reshape(depth, batch, 2, HEAD_DIM, BLK), (0, 1, 4, 2, 3))
    s5_p = lambda ts: jnp.stack(ts).reshape(depth, batch, S5_GROUPS, S5_STATE)
    kt_s, vt_s, s5re_s, s5im_s, conv_s, ssd_s = outs_s
    kv_s = lambda t: jnp.transpose(t.reshape(depth, DEC, 2, HEAD_DIM, BLK), (0, 1, 4, 2, 3))
    s5_s = lambda t: jnp.transpose(t.reshape(depth, S5_GROUPS, S5_STATE, DEC), (0, 3, 1, 2))
    return (y_prompt, y_sample,
            kv_p(outs_p[0]), kv_p(outs_p[1]), s5_p(outs_p[2]), s5_p(outs_p[3]),
            jnp.stack(outs_p[4]), jnp.stack(outs_p[5]),
            kv_s(kt_s), kv_s(vt_s), s5_s(s5re_s), s5_s(s5im_s),
            jnp.transpose(conv_s, (0, 2, 1, 3)),
            jnp.transpose(ssd_s.reshape(depth, SSD_HEADS, SSD_HEAD_DIM, SSD_STATE, DEC),
                          (0, 4, 1, 2, 3)))
```

```python
import functools

import jax
import jax.numpy as jnp
import numpy as np
from jax import lax
from jax.experimental import pallas as pl
from jax.experimental.pallas import tpu as pltpu

F32 = jnp.float32
BF16 = jnp.bfloat16

D_MODEL = 1024
N_META = 16
HEAD_DIM = 64
ATT_WIDTH = 512
ATT_HEADS = 8
KV_WIDTH = 128
S5_WIDTH = 256
S5_GROUPS = 16
S5_GROUP_CH = 16
S5_STATE = 64
S5_LANES = S5_GROUPS * S5_STATE
SSD_WIDTH = 256
SSD_HEADS = 4
SSD_HEAD_DIM = 64
SSD_STATE = 64
SSD_CONV = 4
SSD_CONV_DIM = 512
FFN_HIDDEN = 2816
NORM_EPS = 1e-6
ROPE_THETA = 10000.0
PAST_LEN = 8192
N_IN = 1796

BLK = 128
FRONT = BLK - N_META
TM = 512
SPECIAL = 4 * BLK
DEC = 128
DEC_ROW0 = 2 * BLK
DCH = 16
PROJ_W = 1920
Q0, K0, V0, U0, Z0, XBC0, DT0 = 0, 512, 640, 768, 1024, 1280, 1792
FFN_CHUNK = 256
SEG = BLK // 8
SC_ABAR_RE, SC_ABAR_IM, SC_SEG1, SC_SEG2, SC_SEG4, SC_CARRY, SC_ROWS = 0, 8, 16, 32, 48, 64, 80
NEG_INF = float("-inf")
MIB = 1024 * 1024


def _dot(a, b):
    return jnp.dot(a, b, preferred_element_type=F32)


def _dot_nt(a, b):
    return lax.dot_general(a, b, (((1,), (1,)), ((), ())), preferred_element_type=F32)


def _sigmoid(x):
    return 1.0 / (1.0 + jnp.exp(-x))


def _silu(x):
    return x * _sigmoid(x)


def _softplus(x):
    return jnp.maximum(x, 0.0) + jnp.log1p(jnp.exp(-jnp.abs(x)))


def _gelu_tanh(x):
    return x * _sigmoid((2.0 * 0.7978845608028654) * (x + 0.044715 * (x * x * x)))


def _rmsnorm(x, g):
    return x * lax.rsqrt(jnp.mean(x * x, axis=-1, keepdims=True) + NORM_EPS) * g


def _iota(shape, dim):
    return lax.broadcasted_iota(jnp.int32, shape, dim)


def _rope(x, cosv, sinv):
    first = (_iota(x.shape, 1) & 63) < 32
    partner = jnp.where(first, pltpu.roll(x, 96, 1), pltpu.roll(x, 32, 1))
    return x * cosv + partner * sinv


def _lane_bcast(x, widths):
    rows = x.shape[0]
    return jnp.concatenate(
        [jnp.broadcast_to(x[:, h:h + 1], (rows, w)) for h, w in enumerate(widths)], axis=1)


def _split3(x):
    hi = x.astype(BF16)
    r1 = x - hi.astype(F32)
    mid = r1.astype(BF16)
    lo = (r1 - mid.astype(F32)).astype(BF16)
    return hi, mid, lo


def _s5_prep_kernel(are_ref, aim_ref, ldt_ref, btre_ref, btim_ref, cre_in, cim_in,
                    sc_ref, bre_ref, bim_ref, cret_ref, cimt_ref, bret_ref, bimt_ref, cre_ref, cim_ref,
                    abart_ref, stage):
    ar = are_ref[0]
    ai = aim_ref[0]
    dt = jnp.exp(ldt_ref[0])

    def power(k):
        kf = k.astype(F32)
        mag = jnp.exp((kf * dt) * ar)
        ang = (kf * dt) * ai
        return mag * jnp.cos(ang), mag * jnp.sin(ang)

    pre, pim = power(jnp.full((8, S5_LANES), 1, jnp.int32))
    sc_ref[0, SC_ABAR_RE:SC_ABAR_RE + 8, :] = pre
    sc_ref[0, SC_ABAR_IM:SC_ABAR_IM + 8, :] = pim
    row = _iota((8, S5_LANES), 0)
    for d, base in ((1, SC_SEG1), (2, SC_SEG2), (4, SC_SEG4)):
        sre, sim = power(jnp.full((8, S5_LANES), SEG * d, jnp.int32))
        sc_ref[0, base:base + 8, :] = jnp.where(row >= d, sre, 0.0)
        sc_ref[0, base + 8:base + 16, :] = jnp.where(row >= d, sim, 0.0)
    qre, qim = power(SEG * (row + 1))
    sc_ref[0, SC_CARRY:SC_CARRY + 8, :] = qre
    sc_ref[0, SC_CARRY + 8:SC_CARRY + 16, :] = qim
    abar_re = pre[0:1]
    abar_im = pim[0:1]
    den = ar * ar + ai * ai
    xr = abar_re - 1.0
    f_re = (xr * ar + abar_im * ai) / den
    f_im = (abar_im * ar - xr * ai) / den
    br = btre_ref[0]
    bi = btim_ref[0]
    bb_re = f_re * br - f_im * bi
    bb_im = f_re * bi + f_im * br
    lane_group = _iota((S5_GROUP_CH, S5_LANES), 1) >> 6
    for g in range(S5_GROUPS):
        rows = slice(g * S5_GROUP_CH, (g + 1) * S5_GROUP_CH)
        own = lane_group == g
        stage[0, rows, :] = jnp.where(own, bb_re, 0.0)
        stage[1, rows, :] = jnp.where(own, bb_im, 0.0)
        stage[2, rows, :] = jnp.where(own, jnp.concatenate([cre_in[0, rows, :]] * S5_GROUPS, axis=1), 0.0)
        stage[3, rows, :] = jnp.where(own, jnp.concatenate([cim_in[0, rows, :]] * S5_GROUPS, axis=1), 0.0)
    for k, (direct, transposed) in enumerate(((bre_ref, bret_ref), (bim_ref, bimt_ref),
                                              (cret_ref, cre_ref), (cimt_ref, cim_ref))):
        direct[0] = stage[k].astype(BF16)
        transposed[0] = stage[k].T.astype(BF16)
    seq_lanes = abart_ref.shape[-1]
    abart_ref[0, 0] = jnp.broadcast_to(abar_re, (seq_lanes, S5_LANES)).T
    abart_ref[0, 1] = jnp.broadcast_to(abar_im, (seq_lanes, S5_LANES)).T


def _s5_prepare(a_re, a_im, log_dt, b_re, b_im, c_re, c_im):
    depth = a_re.shape[0]
    flat = lambda t: t.reshape(depth, 1, S5_LANES)
    ldt = jnp.repeat(log_dt, S5_STATE, axis=-1).reshape(depth, 1, S5_LANES)
    bt = lambda t: jnp.transpose(t, (0, 3, 1, 2)).reshape(depth, S5_GROUP_CH, S5_LANES)
    rows_c = lambda t: t.reshape(depth, S5_WIDTH, S5_STATE)
    spec = lambda *shape: pl.BlockSpec((1,) + shape, lambda l: (l,) + (0,) * len(shape))
    wide, tall = (S5_WIDTH, S5_LANES), (S5_LANES, S5_WIDTH)
    shape = lambda s, dt: jax.ShapeDtypeStruct((depth,) + s, dt)
    return pl.pallas_call(
        _s5_prep_kernel,
        grid=(depth,),
        in_specs=[spec(1, S5_LANES)] * 3 + [spec(S5_GROUP_CH, S5_LANES)] * 2 + [spec(S5_WIDTH, S5_STATE)] * 2,
        out_specs=[spec(SC_ROWS, S5_LANES)] + [spec(*wide)] * 4 + [spec(*tall)] * 4 + [spec(2, S5_LANES, DEC)],
        out_shape=[shape((SC_ROWS, S5_LANES), F32)] + [shape(wide, BF16)] * 4 + [shape(tall, BF16)] * 4
        + [shape((2, S5_LANES, DEC), F32)],
        scratch_shapes=[pltpu.VMEM((4,) + wide, F32)],
        name="s5_prep",
    )(flat(a_re), flat(a_im), ldt, bt(b_re), bt(b_im), rows_c(c_re), rows_c(c_im))


def _pad_cast_kernel(w_ref, o_ref):
    full = (N_IN // 128) * 128
    o_ref[:, 0:full] = w_ref[:, 0:full].astype(BF16)
    tail = jnp.concatenate([w_ref[:, full:N_IN], jnp.zeros((w_ref.shape[0], PROJ_W - N_IN), F32)], axis=1)
    o_ref[:, full:PROJ_W] = tail.astype(BF16)


def _pad_cast_w_in(w_in):
    depth, rows, _ = w_in.shape
    return pl.pallas_call(
        _pad_cast_kernel,
        grid=(depth,),
        in_specs=[pl.BlockSpec((None, rows, N_IN), lambda l: (l, 0, 0))],
        out_specs=pl.BlockSpec((None, rows, PROJ_W), lambda l: (l, 0, 0)),
        out_shape=jax.ShapeDtypeStruct((depth, rows, PROJ_W), BF16),
        compiler_params=pltpu.CompilerParams(vmem_limit_bytes=40 * MIB),
        name="pad_cast_w_in",
    )(w_in)


def _run_interleaved(*gens):
    live = list(gens)
    while live:
        live = [g for g in live if next(g, "done") != "done"]


def _tile_rows(special_ref, main_ref):
    return jnp.where(pl.program_id(0) == 0, special_ref[...], main_ref[...])


def _layer_resident(layer, shape):
    return pl.BlockSpec((None,) + shape, lambda i: (layer,) + (0,) * len(shape),
                        pipeline_mode=pl.Buffered(1))


def _special_spec():
    return pl.BlockSpec((TM, D_MODEL), lambda i: (0, 0), pipeline_mode=pl.Buffered(1))


def _main_spec():
    return pl.BlockSpec((TM, D_MODEL), lambda i: (jnp.maximum(i - 1, 0), 0))


def _outffn_kernel(*refs, final):
    mixs_ref, mixm_ref, xs_ref, xm_ref, wo_ref, g2_ref, wg_ref, wu_ref, wd_ref = refs[:9]
    x1 = _tile_rows(xs_ref, xm_ref) + _dot(_tile_rows(mixs_ref, mixm_ref), wo_ref[...])
    h = _rmsnorm(x1, g2_ref[...]).astype(BF16)
    acc = jnp.zeros((TM, D_MODEL), F32)
    for j in range(FFN_HIDDEN // FFN_CHUNK):
        sl = slice(j * FFN_CHUNK, (j + 1) * FFN_CHUNK)
        gate = _dot(h, wg_ref[:, sl])
        up = _dot(h, wu_ref[:, sl])
        acc = acc + _dot((_silu(gate) * up).astype(BF16), wd_ref[sl, :])
    out = x1 + acc
    special_tile = pl.program_id(0) == 0
    if final:
        gf_ref, outs_ref, outm_ref = refs[9:]
        out = _rmsnorm(out, gf_ref[...])
    else:
        f32_slabs = refs[9:12]
        outs_ref, outm_ref = refs[12:14]
        bf16_slabs = refs[14:17]
        for src, dst in zip(f32_slabs, bf16_slabs):
            dst[...] = src[...].astype(BF16)
    outm_ref[...] = out

    @pl.when(special_tile)
    def _():
        outs_ref[...] = out


def _outffn(mix_special, mix_main, x_special, x_main, wo, g2, ffn_w, layer, next_ffn_f32=None, gf=None):
    main_rows = x_main.shape[0]
    rows = SPECIAL + main_rows
    tiles = rows // TM
    resident = functools.partial(_layer_resident, layer)
    whole = lambda a: pl.BlockSpec(a.shape, lambda i: (0, 0), pipeline_mode=pl.Buffered(1))
    mix_main = mix_main.reshape(main_rows, D_MODEL)

    in_specs = [_special_spec(), _main_spec(), _special_spec(), _main_spec(),
                resident((D_MODEL, D_MODEL)), resident((1, D_MODEL))] + [whole(w) for w in ffn_w]
    out_specs = [pl.BlockSpec((TM, D_MODEL), lambda i: (0, 0)), _main_spec()]
    out_shape = [jax.ShapeDtypeStruct((SPECIAL, D_MODEL), F32),
                 jax.ShapeDtypeStruct((main_rows, D_MODEL), F32)]
    args = (mix_special, mix_main, x_special, x_main, wo, g2, *ffn_w)
    if gf is None:
        steps = tiles
        slab_in, slab_out = [], []
        for w in next_ffn_f32:
            w_rows, w_cols = w.shape[1:]
            n = max(d for d in range(1, steps + 1) if w_rows % (16 * d) == 0)
            clamp = lambda i, n=n: jnp.minimum(i, n - 1)
            slab_in.append(pl.BlockSpec((None, w_rows // n, w_cols),
                                        lambda i, clamp=clamp: (layer + 1, clamp(i), 0)))
            slab_out.append(pl.BlockSpec((w_rows // n, w_cols), lambda i, clamp=clamp: (clamp(i), 0)))
        res = pl.pallas_call(
            functools.partial(_outffn_kernel, final=False), grid=(steps,),
            in_specs=in_specs + slab_in,
            out_specs=out_specs + slab_out,
            out_shape=out_shape + [jax.ShapeDtypeStruct(w.shape[1:], BF16) for w in next_ffn_f32],
            compiler_params=pltpu.CompilerParams(dimension_semantics=("arbitrary",),
                                                 vmem_limit_bytes=52 * MIB),
            name="outffn",
        )(*args, *next_ffn_f32)
        return res[0], res[1], tuple(res[2:])
    return pl.pallas_call(
        functools.partial(_outffn_kernel, final=True), grid=(tiles,),
        in_specs=in_specs + [pl.BlockSpec((1, D_MODEL), lambda i: (0, 0))],
        out_specs=out_specs, out_shape=out_shape,
        compiler_params=pltpu.CompilerParams(dimension_semantics=("arbitrary",),
                                             vmem_limit_bytes=52 * MIB),
        name="outffn_final",
    )(*args, gf)


def _s5_tail(y, u, s5d_ref, gluw_ref, glub_ref, s5g_ref):
    y5 = _gelu_tanh(y + s5d_ref[...] * u)
    gate = _dot(y5.astype(BF16), gluw_ref[...]) + glub_ref[...]
    return _rmsnorm(y5 * _sigmoid(gate), s5g_ref[...])


def _pair_block_diag(t):
    left = _iota(t.shape, 1) < 64
    return jnp.concatenate([jnp.where(left, t, 0.0), jnp.where(left, 0.0, t)], axis=0)


N_MIXER_PARAMS = 18
PROJ_SPLIT = 1


def _prompt_mixer_kernel(*refs, layer, batch):
    sinks_all, xs_ref, xm_ref, g1_ref, win_ref = refs[:5]
    shared = refs[5:5 + N_MIXER_PARAMS]
    n_in = 5 + N_MIXER_PARAMS
    mixs_ref, mixm_ref = refs[n_in:n_in + 2]
    outs = refs[n_in + 2:n_in + 8]
    scratch = refs[n_in + 8:-1]
    proj_scr = refs[-1]
    c = pl.program_id(0)
    rows = batch * BLK

    @pl.when(c == 0)
    def _():
        for s in scratch:
            s[...] = jnp.zeros_like(s)
        zero_rows = (_iota((rows, 1), 0) & (BLK - 1)) < FRONT
        h0 = jnp.where(zero_rows, 0.0, _rmsnorm(xs_ref[...].reshape(rows, D_MODEL), g1_ref[...]))
        proj_scr[...] = _dot(h0.astype(BF16), win_ref[...]).reshape(batch, BLK, PROJ_W)

    handover = {}

    def next_projection():
        hn = _rmsnorm(xm_ref[...].reshape(rows, D_MODEL), g1_ref[...]).astype(BF16)
        yield
        for k in range(PROJ_SPLIT):
            cols = slice(k * (PROJ_W // PROJ_SPLIT), (k + 1) * (PROJ_W // PROJ_SPLIT))
            handover[k] = _dot(hn, win_ref[:, cols])
            yield
            yield

    _run_interleaved(next_projection(), *(
        mixer(c, sinks_all.at[layer], proj_scr.at[b], *shared, mixm_ref.at[b],
              *(o.at[b] for o in outs), *(s.at[b] for s in scratch))
        for b in range(batch) for mixer in (_prompt_attention, _prompt_ssd, _prompt_s5)))
    for k in range(PROJ_SPLIT):
        cols = slice(k * (PROJ_W // PROJ_SPLIT), (k + 1) * (PROJ_W // PROJ_SPLIT))
        proj_scr[:, :, cols] = handover[k].reshape(batch, BLK, PROJ_W // PROJ_SPLIT)

    @pl.when(c == 0)
    def _():
        mixs_ref[0:batch] = mixm_ref[...]
        mixs_ref[batch:] = jnp.zeros((mixs_ref.shape[0] - batch, BLK, D_MODEL), mixs_ref.dtype)


def _prompt_attention(
        c, sinks_ref, proj_ref, cos_ref, sin_ref, attg_ref,
        bre_ref, bim_ref, cre_ref, cim_ref, sc_ref, s5d_ref, gluw_ref, glub_ref, s5g_ref,
        convw_ref, convb_ref, dtb_ref, alog_ref, dssd_ref, ssdg_ref,
        mix_ref, kout_ref, vout_ref, s5re_ref, s5im_ref, convout_ref, ssdout_ref,
        kprev, vprev, hre, him, carry_re, carry_im, xbuf, hssd, rows_scr):
    cosv = cos_ref[...]
    sinv = sin_ref[...]
    k_rot = _rope(proj_ref[:, K0:K0 + KV_WIDTH], cosv, sinv)
    vt_new = proj_ref[:, V0:V0 + KV_WIDTH].T
    kout_ref[...] = k_rot.T
    vout_ref[...] = vt_new
    kk = jnp.concatenate([kprev[...], k_rot], axis=0).astype(BF16)
    vvt = jnp.concatenate([vprev[...], vt_new], axis=1).astype(BF16)
    kprev[...] = k_rot
    vprev[...] = vt_new

    qts = [(_rope(proj_ref[:, Q0 + j * 128:Q0 + (j + 1) * 128], cosv, sinv) * (HEAD_DIM ** -0.5)).T
           for j in range(ATT_HEADS // 2)]
    krow = _iota((2 * BLK, BLK), 0)
    qcol = _iota((2 * BLK, BLK), 1)
    kpos = (c - 1) * BLK + krow - FRONT
    ok = (krow >= qcol) & (krow <= qcol + BLK) & (kpos >= 0)
    bias1 = jnp.where(ok, 0.0, NEG_INF)
    bias = jnp.concatenate([bias1] * 4, axis=1)
    zero_half = jnp.zeros((HEAD_DIM, BLK), F32)
    yield

    o_rows = []
    for g in range(2):
        blocks = []
        for r in range(4):
            h = 4 * g + r
            qh = qts[h // 2][(h % 2) * HEAD_DIM:(h % 2 + 1) * HEAD_DIM, :]
            blocks.append(jnp.concatenate([qh, zero_half] if g == 0 else [zero_half, qh], axis=0))
        qg = jnp.concatenate(blocks, axis=1).astype(BF16)
        s = _dot(kk, qg) + bias
        sink = jnp.concatenate(
            [jnp.full((1, BLK), sinks_ref[4 * g + r], F32) for r in range(4)], axis=1)
        yield
        m = jnp.maximum(jnp.max(s, axis=0, keepdims=True), sink)
        p = jnp.exp(s - m)
        yield
        inv_den = 1.0 / (jnp.sum(p, axis=0, keepdims=True) + jnp.exp(sink - m))
        og = _dot(vvt, p.astype(BF16))
        for r in range(4):
            cols = slice(r * BLK, (r + 1) * BLK)
            o_rows.append(og[g * HEAD_DIM:(g + 1) * HEAD_DIM, cols] * inv_den[:, cols])
        yield
    o_att = jnp.concatenate(o_rows, axis=0).T
    mix_ref[:, 0:ATT_WIDTH] = _rmsnorm(o_att, attg_ref[...]).astype(mix_ref.dtype)


def _prompt_s5(
        c, sinks_ref, proj_ref, cos_ref, sin_ref, attg_ref,
        bre_ref, bim_ref, cre_ref, cim_ref, sc_ref, s5d_ref, gluw_ref, glub_ref, s5g_ref,
        convw_ref, convb_ref, dtb_ref, alog_ref, dssd_ref, ssdg_ref,
        mix_ref, kout_ref, vout_ref, s5re_ref, s5im_ref, convout_ref, ssdout_ref,
        kprev, vprev, hre, him, carry_re, carry_im, xbuf, hssd, rows_scr):
    for j in range(S5_WIDTH // 128):
        rows_scr[j] = proj_ref[:, U0 + j * 128:U0 + (j + 1) * 128]
    u = jnp.concatenate(
        [jnp.concatenate([rows_scr[j, pl.ds(t, 8, stride=SEG), :] for j in range(S5_WIDTH // 128)], axis=1)
         for t in range(SEG)], axis=0)
    ub = u.astype(BF16)
    hre[...] = _dot(ub, bre_ref[...])
    him[...] = _dot(ub, bim_ref[...])
    yield
    first_seg = _iota((8, 128), 0) == 0
    for lt in range(S5_LANES // 128):
        ls = slice(lt * 128, (lt + 1) * 128)
        a_r = sc_ref[SC_ABAR_RE:SC_ABAR_RE + 8, ls]
        a_i = sc_ref[SC_ABAR_IM:SC_ABAR_IM + 8, ls]
        er = hre[0:8, ls]
        ei = him[0:8, ls]
        for t in range(1, SEG):
            rs = slice(8 * t, 8 * t + 8)
            er, ei = hre[rs, ls] + a_r * er - a_i * ei, him[rs, ls] + a_r * ei + a_i * er
        yield
        for d, b0 in ((1, SC_SEG1), (2, SC_SEG2), (4, SC_SEG4)):
            s_r = sc_ref[b0:b0 + 8, ls]
            s_i = sc_ref[b0 + 8:b0 + 16, ls]
            pr = pltpu.roll(er, d, 0)
            pi = pltpu.roll(ei, d, 0)
            er, ei = er + s_r * pr - s_i * pi, ei + s_r * pi + s_i * pr
        cr = carry_re[:, ls]
        ci = carry_im[:, ls]
        q_r = sc_ref[SC_CARRY:SC_CARRY + 8, ls]
        q_i = sc_ref[SC_CARRY + 8:SC_CARRY + 16, ls]
        tr = er + q_r * cr - q_i * ci
        ti = ei + q_r * ci + q_i * cr
        hr = jnp.where(first_seg, cr, pltpu.roll(tr, 1, 0))
        hi = jnp.where(first_seg, ci, pltpu.roll(ti, 1, 0))
        carry_re[:, ls] = jnp.broadcast_to(tr[7:8], (8, 128))
        carry_im[:, ls] = jnp.broadcast_to(ti[7:8], (8, 128))
        for t in range(SEG):
            rs = slice(8 * t, 8 * t + 8)
            hr, hi = hre[rs, ls] + a_r * hr - a_i * hi, him[rs, ls] + a_r * hi + a_i * hr
            hre[rs, ls] = hr
            him[rs, ls] = hi
        yield
    s5re_ref[...] = carry_re[0:1, :]
    s5im_ref[...] = carry_im[0:1, :]
    y_perm = _dot(hre[...].astype(BF16), cre_ref[...]) - _dot(him[...].astype(BF16), cim_ref[...])
    yield
    o_perm = _s5_tail(y_perm, u, s5d_ref, gluw_ref, glub_ref, s5g_ref)
    for j in range(S5_WIDTH // 128):
        for t in range(SEG):
            rows_scr[j, pl.ds(t, 8, stride=SEG), :] = o_perm[8 * t:8 * t + 8, j * 128:(j + 1) * 128]
        mix_ref[:, ATT_WIDTH + j * 128:ATT_WIDTH + (j + 1) * 128] = rows_scr[j].astype(mix_ref.dtype)


def _prompt_ssd(
        c, sinks_ref, proj_ref, cos_ref, sin_ref, attg_ref,
        bre_ref, bim_ref, cre_ref, cim_ref, sc_ref, s5d_ref, gluw_ref, glub_ref, s5g_ref,
        convw_ref, convb_ref, dtb_ref, alog_ref, dssd_ref, ssdg_ref,
        mix_ref, kout_ref, vout_ref, s5re_ref, s5im_ref, convout_ref, ssdout_ref,
        kprev, vprev, hre, him, carry_re, carry_im, xbuf, hssd, rows_scr):
    xbc = proj_ref[:, XBC0:XBC0 + SSD_CONV_DIM]
    xbuf[8:8 + BLK, :] = xbc
    conv = convb_ref[...]
    for j in range(SSD_CONV):
        conv = conv + xbuf[5 + j:5 + j + BLK, :] * convw_ref[j:j + 1, :]
    yield
    xc = _silu(conv)
    convout_ref[...] = xbc[BLK - (SSD_CONV - 1):BLK, :]
    xbuf[0:8, :] = xbc[BLK - 8:BLK, :]
    xs = xc[:, 0:SSD_WIDTH]
    bm = xc[:, SSD_WIDTH:SSD_WIDTH + 128]
    cm = xc[:, SSD_WIDTH + 128:SSD_WIDTH + 256]
    yield

    lane = _iota((BLK, 128), 1)
    row = _iota((BLK, 128), 0)
    head_r = _iota((8, BLK), 0)
    time_c = _iota((8, BLK), 1)
    live = (head_r < SSD_HEADS) & (time_c >= jnp.where(c == 0, FRONT, 0))
    raw_t = proj_ref[:, DT0:DT0 + 128].T[0:8, :]
    dt_t = jnp.where(live, _softplus(raw_t + dtb_ref[...]), 0.0)
    dta_t = dt_t * (-jnp.exp(alog_ref[...]))
    causal = lane <= row
    triu = jnp.where(row <= lane, 1.0, 0.0).astype(BF16)
    hi3, mid3, lo3 = _split3(dta_t)
    cs_t = _dot(hi3, triu) + _dot(mid3, triu) + _dot(lo3, triu)
    yield
    cs_last = cs_t[:, BLK - 1:BLK]
    packed = jnp.concatenate(
        [dt_t, dt_t * jnp.exp(cs_last - cs_t), jnp.exp(cs_t), cs_t, jnp.zeros((BLK - 32, BLK), F32)], axis=0)
    cols = packed.T
    heads64 = (SSD_HEAD_DIM,) * SSD_HEADS
    xd = xs * _lane_bcast(cols[:, 0:SSD_HEADS], heads64)
    dxd = xs * _lane_bcast(cols[:, 8:8 + SSD_HEADS], heads64)
    ecs = _lane_bcast(cols[:, 16:16 + SSD_HEADS], heads64)
    cs = cols[:, 24:24 + SSD_HEADS]
    yield

    bmb = bm.astype(BF16)
    left = lane < 64
    cb = (_dot_nt(jnp.where(left, cm, 0.0).astype(BF16), bmb),
          _dot_nt(jnp.where(left, 0.0, cm).astype(BF16), bmb))
    scores = []
    for h in range(SSD_HEADS):
        seg = cs[:, h:h + 1] - cs_t[h:h + 1, :]
        scores.append((cb[h // 2] * jnp.exp(jnp.where(causal, seg, NEG_INF))).astype(BF16))
        yield
    y_diag = jnp.concatenate(
        [_dot(jnp.concatenate([scores[2 * j], scores[2 * j + 1]], axis=1),
              _pair_block_diag(xd[:, j * 128:(j + 1) * 128]).astype(BF16)) for j in range(2)], axis=1)
    yield

    h_prev = hssd[...]
    y_off = _dot_nt(cm.astype(BF16), h_prev.astype(BF16)) * ecs
    yield
    states = _dot(dxd.T.astype(BF16), bmb)
    own = (_iota((SSD_WIDTH, 128), 0) >> 7) == (_iota((SSD_WIDTH, 128), 1) >> 6)
    cd = jnp.exp(cs_last)
    cdm = jnp.concatenate(
        [jnp.broadcast_to(cd[h:h + 1, :], (SSD_HEAD_DIM, 128)) for h in range(SSD_HEADS)], axis=0)
    h_new = cdm * h_prev + jnp.where(own, states, 0.0)
    hssd[...] = h_new
    for h in range(SSD_HEADS):
        g0 = (h // 2) * SSD_STATE
        ssdout_ref[h] = h_new[h * SSD_HEAD_DIM:(h + 1) * SSD_HEAD_DIM, g0:g0 + SSD_STATE]
    yield

    yssd = y_diag + y_off + dssd_ref[...] * xs
    yc = yssd * _silu(proj_ref[:, Z0:Z0 + SSD_WIDTH])
    mix_ref[:, ATT_WIDTH + S5_WIDTH:] = _rmsnorm(yc, ssdg_ref[...]).astype(mix_ref.dtype)


def _prompt_mixers(x_special, x_main, pp, tabs, batch, nc, layer):
    const = lambda shape: pl.BlockSpec((None,) + shape, lambda c: (layer,) + (0,) * len(shape))
    whole = lambda shape: pl.BlockSpec(shape, lambda c: (0,) * len(shape))
    in_specs = [
        pl.BlockSpec(memory_space=pltpu.SMEM),
        pl.BlockSpec((batch, BLK, D_MODEL), lambda c: (0, 0, 0)),
        pl.BlockSpec((batch, BLK, D_MODEL), lambda c: (0, jnp.minimum(c, nc - 2), 0)),
        const((1, D_MODEL)),
        pl.BlockSpec((None, D_MODEL, PROJ_W), lambda c: (layer, 0, 0), pipeline_mode=pl.Buffered(1)),
        pl.BlockSpec((BLK, 128), lambda c: (c, 0)),
        pl.BlockSpec((BLK, 128), lambda c: (c, 0)),
        const((1, ATT_WIDTH)),
        const((S5_WIDTH, S5_LANES)), const((S5_WIDTH, S5_LANES)),
        const((S5_LANES, S5_WIDTH)), const((S5_LANES, S5_WIDTH)),
        const((SC_ROWS, S5_LANES)), const((1, S5_WIDTH)), const((S5_WIDTH, S5_WIDTH)),
        const((1, S5_WIDTH)), const((1, S5_WIDTH)),
        const((SSD_CONV, SSD_CONV_DIM)), const((1, SSD_CONV_DIM)),
        const((8, BLK)), const((8, BLK)), const((1, SSD_WIDTH)), const((1, SSD_WIDTH)),
    ]
    state_shapes =[(BLK, KV_WIDTH), (BLK, KV_WIDTH), (1, S5_LANES), (1, S5_LANES),
                    (SSD_CONV - 1, SSD_CONV_DIM), (SSD_HEADS, SSD_HEAD_DIM, SSD_STATE)]
    out_specs = [
        pl.BlockSpec((SPECIAL // BLK, BLK, D_MODEL), lambda c: (0, 0, 0)),
        pl.BlockSpec((batch, BLK, D_MODEL), lambda c: (0, jnp.maximum(c - 1, 0), 0)),
    ] + [whole((batch,) + s) for s in state_shapes]
    out_shape = [
        jax.ShapeDtypeStruct((SPECIAL // BLK, BLK, D_MODEL), BF16),
        jax.ShapeDtypeStruct((batch, (nc - 1) * BLK, D_MODEL), BF16),
    ] + [jax.ShapeDtypeStruct((batch,) + s, F32) for s in state_shapes]
    scratch = [pltpu.VMEM((batch,) + s, F32) for s in (
        (BLK, KV_WIDTH), (BLK, KV_WIDTH), (BLK, S5_LANES), (BLK, S5_LANES),
        (8, S5_LANES), (8, S5_LANES), (BLK + 8, SSD_CONV_DIM), (SSD_WIDTH, 128),
        (S5_WIDTH // 128, BLK, 128))] + [pltpu.VMEM((batch, BLK, PROJ_W), F32)]
    res = pl.pallas_call(
        functools.partial(_prompt_mixer_kernel, layer=layer, batch=batch),
        grid=(nc,),
        in_specs=in_specs, out_specs=out_specs, out_shape=out_shape, scratch_shapes=scratch,
        compiler_params=pltpu.CompilerParams(dimension_semantics=("arbitrary",),
                                             vmem_limit_bytes=48 * MIB),
        name="prompt_mixers",
    )(pp["sinks"], x_special.reshape(SPECIAL // BLK, BLK, D_MODEL),
      x_main.reshape(batch, (nc - 1) * BLK, D_MODEL), pp["ln1"], pp["w_in"],
      tabs["cos_p"], tabs["sin_p"], pp["attn_out_g"],
      pp["bre"], pp["bim"], pp["cre"], pp["cim"], pp["sc"], pp["s5_d"], pp["glu_w"], pp["glu_b"],
      pp["s5_out_g"], pp["conv_w"], pp["conv_b"], pp["dt_bias_c"], pp["a_log_c"], pp["ssd_d"],
      pp["ssd_norm_g"])
    return [res[0].reshape(SPECIAL, D_MODEL)] + list(res[1:])


def _decode_mixer_kernel(*refs, layer, chained):
    n_in = 29 + (6 if chained else 0)
    (sinks_all, _, x_ref, cos_ref, sin_ref, attg_ref,
     bret_ref, bimt_ref, cret_ref, cimt_ref, abar_ref, s5d_ref, gluw_ref, glub_ref, s5g_ref,
     convw_ref, convb_ref, dtb_ref, alog_ref, dssd_ref, ssdg_ref, g1_ref, win_ref,
     kt_ref, vt_ref, s5re_in, s5im_in, conv_in, ssd_in) = refs[:29]
    (mix_ref, ktout_ref, vtout_ref, s5re_ref, s5im_ref, convout_ref, ssdout_ref,
     oatt_scr, xs_scr, xdt_scr, bt_scr, ct_scr, dcyt_scr, yt_scr, proj_ref) = refs[n_in:]
    sinks_ref = sinks_all.at[layer]
    i = pl.program_id(0)
    heads64 = (SSD_HEAD_DIM,) * SSD_HEADS

    @pl.when(i == 0)
    def _():
        mix_ref[DEC:, :] = jnp.zeros((mix_ref.shape[0] - DEC, D_MODEL), mix_ref.dtype)
        proj_ref[...] = _dot(_rmsnorm(x_ref[...], g1_ref[...]).astype(BF16), win_ref[...])

        u = proj_ref[:, U0:U0 + S5_WIDTH]
        ut = u.T.astype(BF16)
        a_re = abar_ref[0]
        a_im = abar_ref[1]
        h0r = s5re_in[...]
        h0i = s5im_in[...]
        hr = _dot(bret_ref[...], ut) + a_re * h0r - a_im * h0i
        hi = _dot(bimt_ref[...], ut) + a_re * h0i + a_im * h0r
        s5re_ref[...] = hr
        s5im_ref[...] = hi
        yt = _dot(cret_ref[...], hr.astype(BF16)) - _dot(cimt_ref[...], hi.astype(BF16))
        mix_ref[0:DEC, ATT_WIDTH:ATT_WIDTH + S5_WIDTH] = _s5_tail(
            yt.T, u, s5d_ref, gluw_ref, glub_ref, s5g_ref).astype(mix_ref.dtype)

        xbc = proj_ref[:, XBC0:XBC0 + SSD_CONV_DIM]
        conv = convb_ref[...]
        for j in range(SSD_CONV - 1):
            conv = conv + conv_in[j] * convw_ref[j:j + 1, :]
        conv = conv + xbc * convw_ref[SSD_CONV - 1:SSD_CONV, :]
        convout_ref[0] = conv_in[1]
        convout_ref[1] = conv_in[2]
        convout_ref[2] = xbc
        xc = _silu(conv)
        xs = xc[:, 0:SSD_WIDTH]
        dt = _softplus(proj_ref[:, DT0:DT0 + 128] + dtb_ref[...])
        decay = jnp.exp(dt * (-jnp.exp(alog_ref[...])))
        xs_scr[...] = xs
        xdt_scr[...] = (xs * _lane_bcast(dt, heads64)).T
        bt_scr[...] = xc[:, SSD_WIDTH:SSD_WIDTH + 128].T
        ct_scr[...] = xc[:, SSD_WIDTH + 128:SSD_WIDTH + 256].T
        dcyt_scr[...] = decay.T[0:8, :]

    r0 = pl.multiple_of(i * DCH, DCH)
    cosv = cos_ref[...]
    sinv = sin_ref[...]
    k_rot = _rope(proj_ref[pl.ds(r0, DCH), K0:K0 + KV_WIDTH], cosv, sinv)
    v_new = proj_ref[pl.ds(r0, DCH), V0:V0 + KV_WIDTH]
    pad_rows = jnp.zeros((BLK - DCH, KV_WIDTH), F32)
    knew_t = jnp.concatenate([k_rot, pad_rows], axis=0).T
    vnew_t = jnp.concatenate([v_new, pad_rows], axis=0).T
    last = _iota((KV_WIDTH, BLK), 1) == BLK - 1
    for b in range(DCH):
        ktout_ref[b] = jnp.where(last, knew_t[:, b:b + 1], pltpu.roll(kt_ref[b], BLK - 1, 1))
        vtout_ref[b] = jnp.where(last, vnew_t[:, b:b + 1], pltpu.roll(vt_ref[b], BLK - 1, 1))
    left = _iota((DCH, 128), 1) < 64
    qs = []
    for h in range(ATT_HEADS):
        j, e, g = h // 2, h % 2, h // 4
        qt = _rope(proj_ref[pl.ds(r0, DCH), Q0 + j * 128:Q0 + (j + 1) * 128], cosv, sinv)
        qt = qt * (HEAD_DIM ** -0.5)
        if e != g:
            qt = pltpu.roll(qt, 64, 1)
        qs.append(jnp.where(left == (g == 0), qt, 0.0))
    qx = jnp.concatenate(qs, axis=0).astype(BF16)
    kt_cat = jnp.concatenate([kt_ref[b] for b in range(DCH)], axis=1).astype(BF16)
    vt_cat = jnp.concatenate([vt_ref[b] for b in range(DCH)], axis=1).astype(BF16)
    s_old = _dot(qx, kt_cat)
    s_new = _dot_nt(qx, k_rot.astype(BF16))
    rseq = _iota((ATT_HEADS * DCH, DCH * BLK), 0) & (DCH - 1)
    same = rseq == (_iota((ATT_HEADS * DCH, DCH * BLK), 1) >> 7)
    same_new = (_iota((ATT_HEADS * DCH, DCH), 0) & (DCH - 1)) == _iota((ATT_HEADS * DCH, DCH), 1)
    s_old = jnp.where(same, s_old, NEG_INF)
    s_new = jnp.where(same_new, s_new, NEG_INF)
    sink = jnp.concatenate(
        [jnp.full((DCH, 1), sinks_ref[h], F32) for h in range(ATT_HEADS)], axis=0)
    m = jnp.maximum(jnp.maximum(jnp.max(s_old, axis=-1, keepdims=True),
                                jnp.max(s_new, axis=-1, keepdims=True)), sink)
    p_old = jnp.exp(s_old - m)
    p_new = jnp.exp(s_new - m)
    den = (jnp.sum(p_old, axis=-1, keepdims=True) + jnp.sum(p_new, axis=-1, keepdims=True)
           + jnp.exp(sink - m))
    o = (_dot_nt((p_old / den).astype(BF16), vt_cat)
         + _dot((p_new / den).astype(BF16), v_new.astype(BF16)))
    o_tiles = []
    for j in range(ATT_HEADS // 2):
        g = j // 2
        a = o[(2 * j) * DCH:(2 * j + 1) * DCH]
        bb = o[(2 * j + 1) * DCH:(2 * j + 2) * DCH]
        if g == 1:
            a = pltpu.roll(a, 64, 1)
        else:
            bb = pltpu.roll(bb, 64, 1)
        o_tiles.append(jnp.where(left, a, bb))
    oatt_scr[pl.ds(r0, DCH), :] = jnp.concatenate(o_tiles, axis=1)

    rows_per_step = SSD_WIDTH // (DEC // DCH)
    head = i // (SSD_HEAD_DIM // rows_per_step)
    g0 = pl.multiple_of((head // 2) * SSD_STATE, SSD_STATE)
    dcy = dcyt_scr[pl.ds(head, 1), :]
    btg = bt_scr[pl.ds(g0, SSD_STATE), :]
    ctg = ct_scr[pl.ds(g0, SSD_STATE), :]
    for rr in range(rows_per_step):
        row = i * rows_per_step + rr
        h_new = dcy * ssd_in[rr] + xdt_scr[pl.ds(row, 1), :] * btg
        ssdout_ref[rr] = h_new
        yt_scr[pl.ds(row, 1), :] = jnp.sum(ctg * h_new, axis=0, keepdims=True)

    @pl.when(i == DEC // DCH - 1)
    def _():
        mix_ref[0:DEC, 0:ATT_WIDTH] = _rmsnorm(oatt_scr[...], attg_ref[...]).astype(mix_ref.dtype)
        xs = xs_scr[...]
        yssd = yt_scr[...].T + dssd_ref[...] * xs
        yc = yssd * _silu(proj_ref[:, Z0:Z0 + SSD_WIDTH])
        mix_ref[0:DEC, ATT_WIDTH + S5_WIDTH:] = _rmsnorm(yc, ssdg_ref[...]).astype(mix_ref.dtype)


def _decode_mixers(mix, x_special, pp, tabs, states, prev, layer):
    depth = states[0].shape[0]
    nsteps = DEC // DCH
    chained = prev is not None
    const = lambda shape: pl.BlockSpec((None,) + shape, lambda i: (layer,) + (0,) * len(shape))
    plain = lambda shape: pl.BlockSpec(shape, lambda i: (0,) * len(shape))
    rows_per_step = SSD_WIDTH // nsteps
    state_specs = [
        pl.BlockSpec((None, DCH, KV_WIDTH, BLK), lambda i: (layer, i, 0, 0)),
        pl.BlockSpec((None, DCH, KV_WIDTH, BLK), lambda i: (layer, i, 0, 0)),
        const((S5_LANES, DEC)), const((S5_LANES, DEC)),
        const((SSD_CONV - 1, DEC, SSD_CONV_DIM)),
        pl.BlockSpec((None, rows_per_step, SSD_STATE, DEC), lambda i: (layer, i, 0, 0)),
    ]
    in_specs = [
        pl.BlockSpec(memory_space=pltpu.SMEM),
        pl.BlockSpec(memory_space=pl.ANY),
        pl.BlockSpec((DEC, D_MODEL), lambda i: (DEC_ROW0 // DEC, 0)),
        plain((1, 128)), plain((1, 128)),
        const((1, ATT_WIDTH)),
        const((S5_LANES, S5_WIDTH)), const((S5_LANES, S5_WIDTH)),
        const((S5_WIDTH, S5_LANES)), const((S5_WIDTH, S5_LANES)),
        const((2, S5_LANES, DEC)), const((1, S5_WIDTH)), const((S5_WIDTH, S5_WIDTH)),
        const((1, S5_WIDTH)), const((1, S5_WIDTH)),
        const((SSD_CONV, SSD_CONV_DIM)), const((1, SSD_CONV_DIM)),
        const((1, 128)), const((1, 128)), const((1, SSD_WIDTH)), const((1, SSD_WIDTH)),
        const((1, D_MODEL)), const((D_MODEL, PROJ_W)),
    ] + state_specs + ([pl.BlockSpec(memory_space=pl.ANY)] * 6 if chained else [])
    out_specs = [pl.BlockSpec((2 * DEC, D_MODEL), lambda i: (DEC_ROW0 // (2 * DEC), 0))] + state_specs
    out_shape = [jax.ShapeDtypeStruct(mix.shape, mix.dtype)] + [
        jax.ShapeDtypeStruct(s.shape, F32) for s in states]
    scratch = [
        pltpu.VMEM((DEC, ATT_WIDTH), F32), pltpu.VMEM((DEC, SSD_WIDTH), F32),
        pltpu.VMEM((SSD_WIDTH, DEC), F32), pltpu.VMEM((128, DEC), F32), pltpu.VMEM((128, DEC), F32),
        pltpu.VMEM((8, DEC), F32), pltpu.VMEM((SSD_WIDTH, DEC), F32),
        pltpu.VMEM((DEC, PROJ_W), F32),
    ]
    aliases = {1: 0}
    if chained:
        aliases.update({29 + k: 1 + k for k in range(6)})
    assert depth > layer
    return pl.pallas_call(
        functools.partial(_decode_mixer_kernel, layer=layer, chained=chained),
        grid=(nsteps,),
        in_specs=in_specs, out_specs=out_specs, out_shape=out_shape,
        scratch_shapes=scratch,
        input_output_aliases=aliases,
        compiler_params=pltpu.CompilerParams(dimension_semantics=("arbitrary",),
                                             vmem_limit_bytes=40 * MIB),
        name="decode_mixers",
    )(pp["sinks"], mix, x_special, tabs["cos_d"], tabs["sin_d"], pp["attn_out_g"],
      pp["bre_t"], pp["bim_t"], pp["cre_t"], pp["cim_t"], pp["abar_t"], pp["s5_d"], pp["glu_w"],
      pp["glu_b"], pp["s5_out_g"], pp["conv_w"], pp["conv_b"], pp["dt_bias"], pp["a_log"],
      pp["ssd_d"], pp["ssd_norm_g"], pp["ln1"], pp["w_in"], *states, *(prev if chained else ()))


def _rope_tables(nc):
    half = HEAD_DIM // 2
    inv = ROPE_THETA ** (-np.arange(half, dtype=np.float64) / half)

    def tab(pos):
        ang = pos.astype(np.float64)[:, None] * inv[None, :]
        cos = np.cos(ang).astype(np.float32)
        sin = np.sin(ang).astype(np.float32)
        return np.tile(cos, (1, 4)), np.concatenate([-sin, sin, -sin, sin], axis=1)

    cos_p, sin_p = tab(np.arange(nc * BLK, dtype=np.int32) - FRONT)
    cos_d, sin_d = tab(np.full((1,), PAST_LEN, dtype=np.int32))
    return {"cos_p": cos_p, "sin_p": sin_p, "cos_d": cos_d, "sin_d": sin_d}


def kernel(x_prompt, x_sample, cache_k, cache_v, state_s5_re, state_s5_im, state_ssd_conv, state_ssd,
           meta_tokens, ln1_g, w_in, attn_sinks, attn_out_g, s5_a_re, s5_a_im, s5_log_dt,
           s5_b_re, s5_b_im, s5_c_re, s5_c_im, s5_d, s5_glu_w, s5_glu_b, s5_out_g,
           ssd_conv_w, ssd_conv_b, ssd_dt_bias, ssd_a_log, ssd_d, ssd_norm_g, w_out,
           ln2_g, w_gate, w_up, w_down, lnf_g):
    batch, seq, _ = x_prompt.shape
    depth = w_in.shape[0]
    assert batch == 2 and x_sample.shape[0] == DEC and x_sample.shape[1] == 1
    assert seq % TM == 0 and cache_k.shape[2] == BLK
    nc = seq // BLK + 1

    zeros_front = jnp.zeros((FRONT, D_MODEL), F32)
    x_special = jnp.concatenate([zeros_front, meta_tokens, zeros_front, meta_tokens,
                                 x_sample.reshape(DEC, D_MODEL), jnp.zeros((BLK, D_MODEL), F32)], axis=0)
    x_main = x_prompt.reshape(batch * seq, D_MODEL)

    tabs = _rope_tables(nc)
    sc, bre, bim, cre_t, cim_t, bre_t, bim_t, cre, cim, abar_t = _s5_prepare(
        s5_a_re, s5_a_im, s5_log_dt, s5_b_re, s5_b_im, s5_c_re, s5_c_im)
    w_in_p = _pad_cast_w_in(w_in)
    w_out_b = w_out.astype(BF16)
    ffn_f32 = (w_gate, w_up, w_down)
    ffn_w = tuple(t[0].astype(BF16) for t in ffn_f32)
    pad_heads = lambda t: jnp.pad(t, ((0, 0), (0, 128 - SSD_HEADS)))[:, None, :]
    head_rows = lambda t: jnp.broadcast_to(
        jnp.pad(t, ((0, 0), (0, 8 - SSD_HEADS)))[:, :, None], (depth, 8, BLK))
    row = lambda t: t[:, None, :]
    pp = {
        "sinks": attn_sinks, "attn_out_g": row(attn_out_g), "ln1": row(ln1_g), "w_in": w_in_p,
        "bre": bre, "bim": bim, "cre": cre, "cim": cim, "sc": sc,
        "bre_t": bre_t, "bim_t": bim_t, "cre_t": cre_t, "cim_t": cim_t, "abar_t": abar_t,
        "s5_d": row(s5_d), "glu_w": s5_glu_w.astype(BF16), "glu_b": row(s5_glu_b),
        "s5_out_g": row(s5_out_g),
        "conv_w": ssd_conv_w, "conv_b": row(ssd_conv_b),
        "dt_bias": pad_heads(ssd_dt_bias), "a_log": pad_heads(ssd_a_log),
        "dt_bias_c": head_rows(ssd_dt_bias), "a_log_c": head_rows(ssd_a_log),
        "ssd_d": row(jnp.repeat(ssd_d, SSD_HEAD_DIM, axis=-1)), "ssd_norm_g": row(ssd_norm_g),
    }
    ln2_r = row(ln2_g)

    states = (
        jnp.transpose(cache_k, (0, 1, 3, 4, 2)).reshape(depth, DEC, KV_WIDTH, BLK),
        jnp.transpose(cache_v, (0, 1, 3, 4, 2)).reshape(depth, DEC, KV_WIDTH, BLK),
        jnp.transpose(state_s5_re, (0, 2, 3, 1)).reshape(depth, S5_LANES, DEC),
        jnp.transpose(state_s5_im, (0, 2, 3, 1)).reshape(depth, S5_LANES, DEC),
        jnp.transpose(state_ssd_conv, (0, 2, 1, 3)),
        jnp.transpose(state_ssd, (0, 2, 3, 4, 1)).reshape(depth, SSD_WIDTH, SSD_STATE, DEC),
    )

    outs_p = [[] for _ in range(6)]
    outs_s = None
    y_special = y_main = None
    for l in range(depth):
        res_p = _prompt_mixers(x_special, x_main, pp, tabs, batch, nc, l)
        res_s = _decode_mixers(res_p[0], x_special, pp, tabs, states, outs_s, l)
        outs_s = res_s[1:]
        for i in range(6):
            outs_p[i].append(res_p[i + 2])
        ffn_args = (res_s[0], res_p[1], x_special, x_main, w_out_b, ln2_r, ffn_w, l)
        if l + 1 < depth:
            x_special, x_main, ffn_w = _outffn(*ffn_args, next_ffn_f32=ffn_f32)
        else:
            y_special, y_main = _outffn(*ffn_args, gf=lnf_g[None, :])

    y_prompt = y_main.reshape(batch, seq, D_MODEL)
    y_sample = y_special[DEC_ROW0:DEC_ROW0 + DEC].reshape(DEC, 1, D_MODEL)
    kv_p = lambda ts: jnp.transpose(
        jnp.stack(ts).reshape(depth, batch, 2, HEAD_DIM, BLK), (0, 1, 4, 2, 3))
    s5_p = lambda ts: jnp.stack(ts).reshape(depth, batch, S5_GROUPS, S5_STATE)
    kt_s, vt_s, s5re_s, s5im_s, conv_s, ssd_s = outs_s
    kv_s = lambda t: jnp.transpose(t.reshape(depth, DEC, 2, HEAD_DIM, BLK), (0, 1, 4, 2, 3))
    s5_s = lambda t: jnp.transpose(t.reshape(depth, S5_GROUPS, S5_STATE, DEC), (0, 3, 1, 2))
    return (y_prompt, y_sample,
            kv_p(outs_p[0]), kv_p(outs_p[1]), s5_p(outs_p[2]), s5_p(outs_p[3]),
            jnp.stack(outs_p[4]), jnp.stack(outs_p[5]),
            kv_s(kt_s), kv_s(vt_s), s5_s(s5re_s), s5_s(s5im_s),
            jnp.transpose(conv_s, (0, 2, 1, 3)),
            jnp.transpose(ssd_s.reshape(depth, SSD_HEADS, SSD_HEAD_DIM, SSD_STATE, DEC),
                          (0, 4, 1, 2, 3)))
```
